```python
import math
import jax, jax.numpy as jnp
from jax import lax
import numpy as np

D_MODEL = 1024
BATCH = 4
SEQ = 4096
DEPTH = 2

PLE_DIM = 256
HEAD_DIM = 64
ROPE_THETA = 10000.0
Q_BLOCK = 128
NEG = -1e30
BIG = 1e30
EPS = 1e-6
H_A = 8
W_A = H_A * HEAD_DIM
FORGET_BIAS_MEAN = 4.0
H_B = 8
G_B = 2
W_B = H_B * HEAD_DIM
KV_B = G_B * HEAD_DIM
L_CMP = 32
CMP_STRIDE = 16
CMP_HIDDEN = 256
L_SEL = 64
TOP_N = 16
SEL_Q_BLOCK = 64
WINDOW = 512
H_C = 4
DH_C = HEAD_DIM
W_C = H_C * 2 * DH_C
N_BRANCH = 3
SPLIT_SIZES = (W_A, W_A, W_A, H_A, W_A,
               W_B, KV_B, KV_B, KV_B, KV_B, KV_B, KV_B, 3 * H_B, W_B,
               2 * H_C * DH_C, 2 * H_C * DH_C, W_C, W_C,
               N_BRANCH * D_MODEL)
N_IN = sum(SPLIT_SIZES)

kernel_name = 'hybrid_fox_nsa_diff_gated_merge'


def rmsnorm(x, g):
    xf = x.astype(jnp.float32)
    y = xf * lax.rsqrt(jnp.mean(xf * xf, axis=-1, keepdims=True) + EPS)
    return (y * g.astype(jnp.float32)).astype(x.dtype)


def rope(t, positions):
    half = t.shape[-1] // 2
    inv_freq = ROPE_THETA ** (-jnp.arange(half, dtype=jnp.float32) / half)
    ang = positions.astype(jnp.float32)[:, :, None, None] * inv_freq
    cos, sin = jnp.cos(ang), jnp.sin(ang)
    tf = t.astype(jnp.float32)
    t1, t2 = tf[..., :half], tf[..., half:]
    return jnp.concatenate([t1 * cos - t2 * sin, t2 * cos + t1 * sin], axis=-1).astype(t.dtype)


def split_cols(proj):
    points = np.cumsum(np.array(SPLIT_SIZES))[:-1].tolist()
    return jnp.split(proj, points, axis=-1)


def forgetting_attention(q, k, v, logf):
    B, S, H, Dh = q.shape
    nb = S // Q_BLOCK
    scale = Dh ** -0.5
    c = jnp.cumsum(logf, axis=1)
    c_keys = c.transpose(0, 2, 1)
    q_blocks = q.reshape(B, nb, Q_BLOCK, H, Dh).transpose(1, 0, 2, 3, 4)
    c_blocks = c.reshape(B, nb, Q_BLOCK, H).transpose(1, 0, 3, 2)
    kpos = jnp.arange(S)

    def block(args):
        n, qi, ci = args
        s = jnp.einsum('bqhd,bkhd->bhqk', qi, k, preferred_element_type=jnp.float32) * scale
        s = s + ci[..., None] - c_keys[:, :, None, :]
        qpos = n * Q_BLOCK + jnp.arange(Q_BLOCK)
        mask = kpos[None, :] <= qpos[:, None]
        prob = jax.nn.softmax(jnp.where(mask, s, NEG), axis=-1)
        return jnp.einsum('bhqk,bkhd->bqhd', prob.astype(v.dtype), v)

    out = lax.map(block, (jnp.arange(nb), q_blocks, c_blocks))
    return out.transpose(1, 0, 2, 3, 4).reshape(B, S, H, Dh)


def compress_blocks(t, pe, w1, b1, w2):
    B, S, G, Dh = t.shape
    n_c = (S - L_CMP) // CMP_STRIDE + 1
    idx = jnp.arange(n_c)[:, None] * CMP_STRIDE + jnp.arange(L_CMP)[None, :]
    blocks = t[:, idx] + pe[None, None, :, None, :]
    flat = blocks.transpose(0, 1, 3, 2, 4).reshape(B, n_c, G, L_CMP * Dh)
    hid = jax.nn.silu(flat @ w1 + b1)
    return hid @ w2


def nsa_attention(q, q_rot, k_cmp, v_cmp, k_sel, v_sel, k_win, v_win, gates):
    B, S, H, Dh = q.shape
    G = k_sel.shape[2]
    hpg = H // G
    scale = Dh ** -0.5
    n_c = k_cmp.shape[1]
    n_sel = S // L_SEL
    top_n = min(TOP_N, n_sel)
    t = jnp.arange(S)

    qg = q.reshape(B, S, G, hpg, Dh)
    s_c = jnp.einsum('bsghd,bcgd->bghsc', qg, k_cmp, preferred_element_type=jnp.float32) * scale
    c_end = jnp.arange(n_c) * CMP_STRIDE + L_CMP - 1
    mask_c = c_end[None, :] <= t[:, None]
    p_c = jax.nn.softmax(jnp.where(mask_c, s_c, NEG), axis=-1) * mask_c
    o_cmp = jnp.einsum('bghsc,bcgd->bsghd', p_c.astype(v_cmp.dtype), v_cmp).reshape(B, S, H, Dh)

    c_start = jnp.arange(n_c) * CMP_STRIDE
    j_start = jnp.arange(n_sel) * L_SEL
    cover = ((c_start[:, None] < j_start[None, :] + L_SEL)
             & (c_start[:, None] + L_CMP > j_start[None, :])).astype(jnp.float32)
    imp = jnp.einsum('bghsc,cj->bgsj', p_c, cover)
    jj = jnp.arange(n_sel)[None, :]
    tb = (t // L_SEL)[:, None]
    valid = jj <= tb
    forced = (jj == 0) | (jj == tb) | (jj == tb - 1)
    score = jnp.where(forced, BIG, jnp.where(valid, imp, -BIG))
    _, sel_idx = lax.top_k(score, top_n)

    ks_blk = k_sel.reshape(B, n_sel, L_SEL, G, Dh).transpose(0, 3, 1, 2, 4)
    vs_blk = v_sel.reshape(B, n_sel, L_SEL, G, Dh).transpose(0, 3, 1, 2, 4)
    nqb = S // SEL_Q_BLOCK
    q_sb = q_rot.reshape(B, nqb, SEL_Q_BLOCK, G, hpg, Dh).transpose(1, 0, 3, 2, 4, 5)
    idx_sb = sel_idx.reshape(B, G, nqb, SEL_Q_BLOCK, top_n).transpose(2, 0, 1, 3, 4)
    bi = jnp.arange(B)[:, None, None, None]
    gi = jnp.arange(G)[None, :, None, None]

    def sel_block(args):
        n, qi, ii = args
        kg = ks_blk[bi, gi, ii]
        vg = vs_blk[bi, gi, ii]
        s = jnp.einsum('bgqhd,bgqnld->bghqnl', qi, kg, preferred_element_type=jnp.float32) * scale
        kpos = ii[..., None] * L_SEL + jnp.arange(L_SEL)
        qpos = n * SEL_Q_BLOCK + jnp.arange(SEL_Q_BLOCK)
        mask = kpos <= qpos[None, None, :, None, None]
        s = jnp.where(mask[:, :, None], s, NEG).reshape(B, G, hpg, SEL_Q_BLOCK, top_n * L_SEL)
        prob = jax.nn.softmax(s, axis=-1).reshape(B, G, hpg, SEL_Q_BLOCK, top_n, L_SEL)
        return jnp.einsum('bghqnl,bgqnld->bqghd', prob.astype(vg.dtype), vg)

    o_sel = lax.map(sel_block, (jnp.arange(nqb), q_sb, idx_sb))
    o_sel = o_sel.transpose(1, 0, 2, 3, 4, 5).reshape(B, S, H, Dh)

    kw_pad = jnp.pad(k_win, ((0, 0), (WINDOW, 0), (0, 0), (0, 0)))
    vw_pad = jnp.pad(v_win, ((0, 0), (WINDOW, 0), (0, 0), (0, 0)))
    nwb = S // Q_BLOCK
    q_wb = q_rot.reshape(B, nwb, Q_BLOCK, G, hpg, Dh).transpose(1, 0, 2, 3, 4, 5)
    ii_ = jnp.arange(Q_BLOCK)[:, None]
    jw = jnp.arange(Q_BLOCK + WINDOW)[None, :]
    band = (jw > ii_) & (jw <= ii_ + WINDOW)

    def win_block(args):
        n, qi = args
        kk = lax.dynamic_slice_in_dim(kw_pad, n * Q_BLOCK, Q_BLOCK + WINDOW, axis=1)
        vv = lax.dynamic_slice_in_dim(vw_pad, n * Q_BLOCK, Q_BLOCK + WINDOW, axis=1)
        s = jnp.einsum('bqghd,bkgd->bghqk', qi, kk, preferred_element_type=jnp.float32) * scale
        mask = band & (n * Q_BLOCK + jw - WINDOW >= 0)
        prob = jax.nn.softmax(jnp.where(mask, s, NEG), axis=-1)
        return jnp.einsum('bghqk,bkgd->bqghd', prob.astype(vv.dtype), vv)

    o_win = lax.map(win_block, (jnp.arange(nwb), q_wb))
    o_win = o_win.transpose(1, 0, 2, 3, 4, 5).reshape(B, S, H, Dh)

    return gates[..., 0:1] * o_cmp + gates[..., 1:2] * o_sel + gates[..., 2:3] * o_win


def differential_attention(q, k, v, lam, lam_init, subln_g):
    B, S, H, _, Dh = q.shape
    nb = S // Q_BLOCK
    scale = Dh ** -0.5
    q_blocks = q.reshape(B, nb, Q_BLOCK, H, 2, Dh).transpose(1, 0, 2, 3, 4, 5)
    kpos = jnp.arange(S)

    def block(args):
        n, qi = args
        s = jnp.einsum('bqhmd,bkhmd->bhmqk', qi, k, preferred_element_type=jnp.float32) * scale
        qpos = n * Q_BLOCK + jnp.arange(Q_BLOCK)
        mask = kpos[None, :] <= qpos[:, None]
        prob = jax.nn.softmax(jnp.where(mask, s, NEG), axis=-1)
        w = prob[:, :, 0] - lam * prob[:, :, 1]
        return jnp.einsum('bhqk,bkhe->bqhe', w.astype(v.dtype), v)

    out = lax.map(block, (jnp.arange(nb), q_blocks))
    out = out.transpose(1, 0, 2, 3, 4).reshape(B, S, H, 2 * Dh)
    return rmsnorm(out, subln_g) * (1.0 - lam_init)


def setup_inputs(seed: int = 0) -> dict:
    key = jax.random.key(seed)
    ks = jax.random.split(key, 24)
    f32 = jnp.float32

    def nrm(k, shape, scale):
        return jax.random.normal(k, shape, f32) * scale

    x = nrm(ks[0], (BATCH, SEQ, D_MODEL), 1.0)
    p = nrm(ks[1], (DEPTH, BATCH, SEQ, PLE_DIM), 1.0)
    offset = jax.random.randint(ks[2], (BATCH, 1), 0, 1024, dtype=jnp.int32)
    positions = offset + jnp.arange(SEQ, dtype=jnp.int32)[None, :]
    return {
        'x': x,
        'p': p,
        'positions': positions,
        'norm_g': 1.0 + nrm(ks[3], (DEPTH, D_MODEL), 0.02),
        'w_in': nrm(ks[4], (DEPTH, D_MODEL, N_IN), D_MODEL ** -0.5),
        'b_forget': FORGET_BIAS_MEAN + nrm(ks[5], (DEPTH, H_A), 0.5),
        'cmp_pe_k': nrm(ks[6], (DEPTH, L_CMP, HEAD_DIM), 0.1),
        'cmp_w1_k': nrm(ks[7], (DEPTH, L_CMP * HEAD_DIM, CMP_HIDDEN), (L_CMP * HEAD_DIM) ** -0.5),
        'cmp_b1_k': nrm(ks[8], (DEPTH, CMP_HIDDEN), 0.01),
        'cmp_w2_k': nrm(ks[9], (DEPTH, CMP_HIDDEN, HEAD_DIM), CMP_HIDDEN ** -0.5),
        'cmp_pe_v': nrm(ks[10], (DEPTH, L_CMP, HEAD_DIM), 0.1),
        'cmp_w1_v': nrm(ks[11], (DEPTH, L_CMP * HEAD_DIM, CMP_HIDDEN), (L_CMP * HEAD_DIM) ** -0.5),
        'cmp_b1_v': nrm(ks[12], (DEPTH, CMP_HIDDEN), 0.01),
        'cmp_w2_v': nrm(ks[13], (DEPTH, CMP_HIDDEN, HEAD_DIM), CMP_HIDDEN ** -0.5),
        'diff_lam': nrm(ks[14], (DEPTH, 4, DH_C), 0.1),
        'diff_subln_g': 1.0 + nrm(ks[15], (DEPTH, 2 * DH_C), 0.02),
        'w_br_a': nrm(ks[16], (DEPTH, W_A, D_MODEL), W_A ** -0.5),
        'w_br_b': nrm(ks[17], (DEPTH, W_B, D_MODEL), W_B ** -0.5),
        'w_br_c': nrm(ks[18], (DEPTH, W_C, D_MODEL), W_C ** -0.5),
        'w_out': nrm(ks[19], (DEPTH, D_MODEL, D_MODEL), D_MODEL ** -0.5),
        'w_ple': nrm(ks[20], (DEPTH, PLE_DIM, D_MODEL), PLE_DIM ** -0.5),
        'w_ple_gate': nrm(ks[21], (DEPTH, D_MODEL, D_MODEL), D_MODEL ** -0.5),
        'final_g': 1.0 + nrm(ks[22], (D_MODEL,), 0.02),
    }


def reference(x, p, positions, norm_g, w_in, b_forget, cmp_pe_k, cmp_w1_k, cmp_b1_k, cmp_w2_k,
              cmp_pe_v, cmp_w1_v, cmp_b1_v, cmp_w2_v, diff_lam, diff_subln_g,
              w_br_a, w_br_b, w_br_c, w_out, w_ple, w_ple_gate, final_g):
    B, S, D = x.shape
    for i in range(DEPTH):
        h = rmsnorm(x, norm_g[i])
        proj = jnp.einsum('bsd,dn->bsn', h, w_in[i])
        (qa, ka, va, fa, za,
         qb, kcb, vcb, ksb, vsb, kwb, vwb, gb, zb,
         qc, kc, vc, zc, mg) = split_cols(proj)

        logf = jax.nn.log_sigmoid(fa.astype(jnp.float32) + b_forget[i].astype(jnp.float32))
        ya = forgetting_attention(qa.reshape(B, S, H_A, HEAD_DIM), ka.reshape(B, S, H_A, HEAD_DIM),
                                  va.reshape(B, S, H_A, HEAD_DIM), logf)
        ya = ya.reshape(B, S, W_A) * jax.nn.silu(za)

        qb = qb.reshape(B, S, H_B, HEAD_DIM)
        qb_rot = rope(qb, positions)
        k_cmp = compress_blocks(kcb.reshape(B, S, G_B, HEAD_DIM), cmp_pe_k[i], cmp_w1_k[i], cmp_b1_k[i], cmp_w2_k[i])
        v_cmp = compress_blocks(vcb.reshape(B, S, G_B, HEAD_DIM), cmp_pe_v[i], cmp_w1_v[i], cmp_b1_v[i], cmp_w2_v[i])
        k_sel = rope(ksb.reshape(B, S, G_B, HEAD_DIM), positions)
        k_win = rope(kwb.reshape(B, S, G_B, HEAD_DIM), positions)
        gates = jax.nn.sigmoid(gb.reshape(B, S, H_B, 3))
        yb = nsa_attention(qb, qb_rot, k_cmp, v_cmp, k_sel, vsb.reshape(B, S, G_B, HEAD_DIM),
                           k_win, vwb.reshape(B, S, G_B, HEAD_DIM), gates)
        yb = yb.reshape(B, S, W_B) * jax.nn.silu(zb)

        qc = rope(qc.reshape(B, S, 2 * H_C, DH_C), positions).reshape(B, S, H_C, 2, DH_C)
        kc = rope(kc.reshape(B, S, 2 * H_C, DH_C), positions).reshape(B, S, H_C, 2, DH_C)
        lam_init = 0.8 - 0.6 * math.exp(-0.3 * i)
        lq1, lk1, lq2, lk2 = diff_lam[i].astype(jnp.float32)
        lam = jnp.exp(jnp.sum(lq1 * lk1)) - jnp.exp(jnp.sum(lq2 * lk2)) + lam_init
        yc = differential_attention(qc, kc, vc.reshape(B, S, H_C, 2 * DH_C), lam, lam_init, diff_subln_g[i])
        yc = yc.reshape(B, S, W_C) * jax.nn.silu(zc)

        g = jax.nn.sigmoid(mg).reshape(B, S, N_BRANCH, D)
        merged = (g[:, :, 0] * (ya @ w_br_a[i]) + g[:, :, 1] * (yb @ w_br_b[i])
                  + g[:, :, 2] * (yc @ w_br_c[i]))
        x = x + merged @ w_out[i]

        x = x + jax.nn.sigmoid(x @ w_ple_gate[i]) * (p[i] @ w_ple[i])
    return rmsnorm(x, final_g)
```

```python
import functools
import math

import numpy as np
import jax
import jax.numpy as jnp
from jax import lax
from jax.experimental import pallas as pl
from jax.experimental.pallas import tpu as pltpu

F32 = jnp.float32
BF16 = jnp.bfloat16

LANES = 128
HEAD_DIM = 64
NEG = -1e30
BIG = 1e30
LOWEST = -3e38
EPS = 1e-6
ROPE_THETA = 10000.0
H_A = 8
H_B = 8
G_B = 2
H_C = 4
L_CMP = 32
CMP_STRIDE = 16
CMP_HIDDEN = 256
L_SEL = 64
TOP_N = 16
WINDOW = 512
N_BRANCH = 3
D_MODEL = 1024
W_A = H_A * HEAD_DIM
W_B = H_B * HEAD_DIM
W_C = H_C * 2 * HEAD_DIM
KV_B = G_B * HEAD_DIM
SPLIT_SIZES = (W_A, W_A, W_A, H_A, W_A,
               W_B, KV_B, KV_B, KV_B, KV_B, KV_B, KV_B, 3 * H_B, W_B,
               2 * H_C * HEAD_DIM, 2 * H_C * HEAD_DIM, W_C, W_C,
               N_BRANCH * D_MODEL)
SCALE = HEAD_DIM ** -0.5

VMEM_LIMIT = 48 * 1024 * 1024

TQ = 256
PROJ_TM = 1024
POST_TM = 256


def _cparams(sem):
    return pltpu.CompilerParams(dimension_semantics=sem, vmem_limit_bytes=VMEM_LIMIT)


def _dot(a, b):
    return jnp.dot(a, b, preferred_element_type=F32)


def _dot_nt(a, b):
    return lax.dot_general(a, b, (((1,), (1,)), ((), ())), preferred_element_type=F32)


def _split3(x):
    hi = x.astype(BF16)
    r1 = x - hi.astype(F32)
    mid = r1.astype(BF16)
    lo = (r1 - mid.astype(F32)).astype(BF16)
    return hi, mid, lo


def _dot_exact01(x, m01):
    hi, mid, lo = _split3(x)
    return _dot(hi, m01) + _dot(mid, m01) + _dot(lo, m01)


def _rmsnorm_kernel(x_ref, g_ref, o_ref):
    x = x_ref[...]
    y = x * lax.rsqrt(jnp.mean(x * x, axis=-1, keepdims=True) + EPS)
    o_ref[...] = (y * g_ref[...]).astype(o_ref.dtype)


def _rmsnorm(x, g, out_dtype, tm=512):
    T, D = x.shape
    return pl.pallas_call(
        _rmsnorm_kernel,
        grid=(T // tm,),
        in_specs=[pl.BlockSpec((tm, D), lambda i: (i, 0)),
                  pl.BlockSpec((1, D), lambda i: (0, 0))],
        out_specs=pl.BlockSpec((tm, D), lambda i: (i, 0)),
        out_shape=jax.ShapeDtypeStruct((T, D), out_dtype),
        compiler_params=_cparams(("parallel",)),
        name="rmsnorm",
    )(x, g.reshape(1, D))


def _rope_table_kernel(pos_ref, invf_ref, sign_ref, cos_ref, sin_ref):
    ang = pos_ref[...] * invf_ref[...]
    cos_ref[...] = jnp.cos(ang)
    sin_ref[...] = jnp.sin(ang) * sign_ref[...]


def _rope_tables(pos_f32, tm=512):
    T = pos_f32.shape[0]
    half = HEAD_DIM // 2
    inv_freq = ROPE_THETA ** (-jnp.arange(half, dtype=F32) / half)
    invf = jnp.tile(inv_freq, LANES // half).reshape(1, LANES)
    sign = jnp.tile(jnp.concatenate([-jnp.ones((half,), F32), jnp.ones((half,), F32)]),
                    LANES // HEAD_DIM).reshape(1, LANES)
    return pl.pallas_call(
        _rope_table_kernel,
        grid=(T // tm,),
        in_specs=[pl.BlockSpec((tm, 1), lambda i: (i, 0)),
                  pl.BlockSpec((1, LANES), lambda i: (0, 0)),
                  pl.BlockSpec((1, LANES), lambda i: (0, 0))],
        out_specs=[pl.BlockSpec((tm, LANES), lambda i: (i, 0)),
                   pl.BlockSpec((tm, LANES), lambda i: (i, 0))],
        out_shape=[jax.ShapeDtypeStruct((T, LANES), F32),
                   jax.ShapeDtypeStruct((T, LANES), F32)],
        compiler_params=_cparams(("parallel",)),
        name="rope_table",
    )(pos_f32.reshape(T, 1), invf, sign)


def _proj_kernel(h_ref, w_ref, *rest, rope):
    acc = _dot(h_ref[...], w_ref[...])
    if rope:
        cos_ref, sin_ref, o_ref = rest
        cos = cos_ref[...]
        sin = sin_ref[...]
        lane = lax.broadcasted_iota(jnp.int32, cos.shape, 1)
        first = (lane & (HEAD_DIM - 1)) < (HEAD_DIM // 2)
        for c in range(acc.shape[1] // LANES):
            t = acc[:, c * LANES:(c + 1) * LANES]
            sw = jnp.where(first, pltpu.roll(t, LANES - HEAD_DIM // 2, 1),
                           pltpu.roll(t, HEAD_DIM // 2, 1))
            o_ref[:, c * LANES:(c + 1) * LANES] = (t * cos + sw * sin).astype(o_ref.dtype)
    else:
        (o_ref,) = rest
        o_ref[...] = acc.astype(o_ref.dtype)


def _proj(h, w, out_dtype, tn, rope_tabs=None, name="proj"):
    T, D = h.shape
    N = w.shape[1]
    tm = min(PROJ_TM, T)
    assert T % tm == 0 and N % tn == 0
    in_specs = [pl.BlockSpec((tm, D), lambda i, j: (i, 0)),
                pl.BlockSpec((D, tn), lambda i, j: (0, j))]
    args = [h, w]
    if rope_tabs is not None:
        in_specs += [pl.BlockSpec((tm, LANES), lambda i, j: (i, 0)),
                     pl.BlockSpec((tm, LANES), lambda i, j: (i, 0))]
        args += list(rope_tabs)
    return pl.pallas_call(
        functools.partial(_proj_kernel, rope=rope_tabs is not None),
        grid=(T // tm, N // tn),
        in_specs=in_specs,
        out_specs=pl.BlockSpec((tm, tn), lambda i, j: (i, j)),
        out_shape=jax.ShapeDtypeStruct((T, N), out_dtype),
        compiler_params=_cparams(("parallel", "arbitrary")),
        name=name,
    )(*args)


def _cumsum_kernel(fg_ref, b_ref, c_ref, carry_sc, *, tiles_per_seq):
    i = pl.program_id(0)

    @pl.when(i % tiles_per_seq == 0)
    def _():
        carry_sc[...] = jnp.zeros_like(carry_sc)

    z = fg_ref[...] + b_ref[...]
    logf = jnp.minimum(z, 0.0) - jnp.log1p(jnp.exp(-jnp.abs(z)))
    tm = z.shape[0]
    r = lax.broadcasted_iota(jnp.int32, (tm, tm), 0)
    c = lax.broadcasted_iota(jnp.int32, (tm, tm), 1)
    tri = jnp.where(c <= r, 1.0, 0.0).astype(BF16)
    cs = _dot_exact01_left(tri, logf) + carry_sc[...]
    c_ref[...] = cs
    carry_sc[...] = cs[tm - 1:tm, :]


def _dot_exact01_left(m01, x):
    hi, mid, lo = _split3(x)
    return _dot(m01, hi) + _dot(m01, mid) + _dot(m01, lo)


def _forget_cumsum(o3, fg_blk, b_pad, S, tm=256):
    T = o3.shape[0]
    return pl.pallas_call(
        functools.partial(_cumsum_kernel, tiles_per_seq=S // tm),
        grid=(T // tm,),
        in_specs=[pl.BlockSpec((tm, LANES), lambda i: (i, fg_blk)),
                  pl.BlockSpec((1, LANES), lambda i: (0, 0))],
        out_specs=pl.BlockSpec((tm, LANES), lambda i: (i, 0)),
        out_shape=jax.ShapeDtypeStruct((T, LANES), F32),
        scratch_shapes=[pltpu.VMEM((1, LANES), F32)],
        compiler_params=_cparams(("arbitrary",)),
        name="forget_cumsum",
    )(o3, b_pad)


def _flash_kernel(*refs, mode, tq, window, lam_init, n_sel):
    it = iter(refs)
    q_ref = next(it)
    mm_ref = next(it) if mode == "sel" else None
    ka_ref = next(it)
    kb_ref = next(it) if mode == "sel" else None
    v_ref = next(it)
    if mode == "fox":
        cq_ref = next(it)
        ck_ref = next(it)
    if mode == "diff":
        lam_ref = next(it)
        g_ref = next(it)
    o_ref = next(it)
    m_sc = next(it)
    l_sc = next(it)
    acc_sc = next(it)
    if mode == "sel":
        kpa_sc = next(it)
        kpb_sc = next(it)

    tk = tq
    qi = pl.program_id(2)
    lane = lax.broadcasted_iota(jnp.int32, (tq, LANES), 1)
    half0 = lane < HEAD_DIM

    if mode == "sel":
        @pl.when(qi == 0)
        def _():
            S = ka_ref.shape[0]
            row = lax.broadcasted_iota(jnp.int32, (S, LANES), 0)
            ln = lax.broadcasted_iota(jnp.int32, (S, LANES), 1)
            blk = lax.shift_right_logical(row, int(math.log2(L_SEL)))
            ea = jnp.where(ln - HEAD_DIM == blk, BIG, 0.0)
            eb = jnp.where(ln == blk, BIG, 0.0)
            kpa_sc[...] = jnp.where(ln >= HEAD_DIM, ea, ka_ref[...].astype(F32)).astype(BF16)
            kpb_sc[...] = jnp.where(ln < HEAD_DIM, eb, kb_ref[...].astype(F32)).astype(BF16)
        kp = (kpa_sc, kpb_sc)
    else:
        kp = (ka_ref, ka_ref)

    q2 = q_ref[...].astype(F32)
    if mode == "sel":
        other = mm_ref[...].astype(F32)
    else:
        other = jnp.zeros_like(q2)
    qm = (jnp.where(half0, q2, other).astype(BF16), jnp.where(half0, other, q2).astype(BF16))

    if mode == "fox":
        cq = (cq_ref[0, 0, :, 0:1], cq_ref[0, 0, :, 1:2])

    m_sc[...] = jnp.full_like(m_sc, NEG)
    l_sc[...] = jnp.zeros_like(l_sc)
    acc_sc[...] = jnp.zeros_like(acc_sc)

    def step(j, masked):
        koff = pl.multiple_of(j * tk, tk)
        if masked:
            r = lax.broadcasted_iota(jnp.int32, (tq, tk), 0) + qi * tq
            c = lax.broadcasted_iota(jnp.int32, (tq, tk), 1) + j * tk
            ok = c <= r
            if mode == "win":
                ok = ok & (c > r - window)
        for a in range(2):
            k = kp[a][pl.ds(koff, tk), :]
            s = _dot_nt(qm[a], k)
            if mode == "fox":
                s = s + cq[a] - ck_ref[0, 0, a:a + 1, pl.ds(koff, tk)]
            if masked:
                s = jnp.where(ok, s, NEG)
            m_old = m_sc[a]
            m_new = jnp.maximum(m_old, jnp.max(s, axis=1, keepdims=True))
            alpha = jnp.exp(m_old - m_new)
            p = jnp.exp(s - m_new)
            l_sc[a] = alpha * l_sc[a] + jnp.sum(p, axis=1, keepdims=True)
            v = v_ref[pl.ds(koff, tk), :]
            acc_sc[a] = alpha * acc_sc[a] + _dot(p.astype(BF16), v)
            m_sc[a] = m_new

    def full_body(j, carry):
        step(j, False)
        return carry

    if mode == "win":
        nw = window // tk

        @pl.when(qi >= nw)
        def _():
            step(qi - nw, True)

        lax.fori_loop(jnp.maximum(qi - nw + 1, 0), qi, full_body, 0)
    else:
        lax.fori_loop(0, qi, full_body, 0)
    step(qi, True)

    o0 = acc_sc[0] / l_sc[0]
    o1 = acc_sc[1] / l_sc[1]
    if mode == "diff":
        lq = lam_ref[...]
        lam = (jnp.exp(jnp.sum(lq[0:1] * lq[1:2], keepdims=True))
               - jnp.exp(jnp.sum(lq[2:3] * lq[3:4], keepdims=True)) + lam_init)
        d = o0 - lam * o1
        y = d * lax.rsqrt(jnp.mean(d * d, axis=-1, keepdims=True) + EPS)
        o_ref[...] = (y * g_ref[...]) * (1.0 - lam_init)
    else:
        o_ref[...] = jnp.where(half0, o0, o1)


def _flash(mode, B, S, n_blk, q, q_blk0, ka, ka_blk, v, v_blk, *, kb=None, kb_blk=None,
           mm=None, cq=None, ck=None, lam=None, subln_g=None, lam_init=0.0):
    tq = min(TQ, S)
    nq = S // tq
    T = B * S
    in_specs = [pl.BlockSpec((tq, LANES), lambda b, h, i: (b * nq + i, q_blk0 + h))]
    args = [q]
    if mode == "sel":
        in_specs.append(pl.BlockSpec((tq, LANES), lambda b, h, i: (b * nq + i, h // 2)))
        args.append(mm)
    in_specs.append(pl.BlockSpec((S, LANES), lambda b, h, i: (b, ka_blk(h))))
    args.append(ka)
    if mode == "sel":
        in_specs.append(pl.BlockSpec((S, LANES), lambda b, h, i: (b, kb_blk(h))))
        args.append(kb)
    in_specs.append(pl.BlockSpec((S, LANES), lambda b, h, i: (b, v_blk(h))))
    args.append(v)
    if mode == "fox":
        in_specs.append(pl.BlockSpec((1, 1, tq, 2), lambda b, h, i: (b, h, i, 0)))
        in_specs.append(pl.BlockSpec((1, 1, 2, S), lambda b, h, i: (b, h, 0, 0)))
        args += [cq, ck]
    if mode == "diff":
        in_specs.append(pl.BlockSpec(lam.shape, lambda b, h, i: (0, 0)))
        in_specs.append(pl.BlockSpec((1, LANES), lambda b, h, i: (0, 0)))
        args += [lam, subln_g]
    scratch = [pltpu.VMEM((2, tq, 1), F32), pltpu.VMEM((2, tq, 1), F32),
               pltpu.VMEM((2, tq, LANES), F32)]
    if mode == "sel":
        scratch += [pltpu.VMEM((S, LANES), BF16), pltpu.VMEM((S, LANES), BF16)]
    return pl.pallas_call(
        functools.partial(_flash_kernel, mode=mode, tq=tq, window=WINDOW, lam_init=lam_init,
                          n_sel=S // L_SEL),
        grid=(B, n_blk, nq),
        in_specs=in_specs,
        out_specs=pl.BlockSpec((tq, LANES), lambda b, h, i: (b * nq + i, h)),
        out_shape=jax.ShapeDtypeStruct((T, n_blk * LANES), F32),
        scratch_shapes=scratch,
        compiler_params=_cparams(("parallel", "parallel", "arbitrary")),
        name="flash_" + mode,
    )(*args)


def _compress_kernel(t_ref, pe_ref, w1_ref, b1_ref, w2_ref, o_ref):
    t = t_ref[0, 0, 0]
    R, half = t.shape
    pe = pe_ref[0]
    xa = (t + pe[:, :half]).astype(BF16)
    xb = (t + pe[:, half:]).astype(BF16)
    a = _dot(xa, w1_ref[0, :half, :])
    b = _dot(xb, w1_ref[0, half:, :])
    hp = a + pltpu.roll(b, R - 1, 0) + b1_ref[0]
    hid = hp * jax.nn.sigmoid(hp)
    o_ref[0, 0] = _dot(hid.astype(BF16), w2_ref[0]).astype(o_ref.dtype)


def _compress(tkv, pe, w1, b1, w2dup):
    _, B, G, R, W = tkv.shape
    return pl.pallas_call(
        _compress_kernel,
        grid=(2, B, G),
        in_specs=[pl.BlockSpec((1, 1, 1, R, W), lambda s, b, g: (s, b, g, 0, 0)),
                  pl.BlockSpec((1, 1, 2 * W), lambda s, b, g: (s, 0, 0)),
                  pl.BlockSpec((1, 2 * W, CMP_HIDDEN), lambda s, b, g: (s, 0, 0)),
                  pl.BlockSpec((1, 1, CMP_HIDDEN), lambda s, b, g: (s, 0, 0)),
                  pl.BlockSpec((1, CMP_HIDDEN, LANES), lambda s, b, g: (s, 0, 0))],
        out_specs=pl.BlockSpec((1, 1, R, LANES), lambda s, b, g: (s, b, 0, g)),
        out_shape=jax.ShapeDtypeStruct((2, B, R, G * LANES), BF16),
        compiler_params=_cparams(("parallel", "parallel", "parallel")),
        name="nsa_compress",
    )(tkv, pe, w1, b1, w2dup)


def _cmp_kernel(q_ref, kc_ref, vc_ref, cov_ref, o_ref, mm_ref, *, tq, n_sel, top_n):
    qi = pl.program_id(2)
    kc = kc_ref[0, 0]
    vc = vc_ref[0, 0]
    R = kc.shape[0]
    lane = lax.broadcasted_iota(jnp.int32, (tq, LANES), 1)
    half0 = lane < HEAD_DIM
    t_r = lax.broadcasted_iota(jnp.int32, (tq, R), 0) + qi * tq
    c_end = lax.broadcasted_iota(jnp.int32, (tq, R), 1) * CMP_STRIDE + (L_CMP - 1)
    ok = c_end <= t_r
    psum = jnp.zeros((tq, R), F32)
    outs = []
    for pr in range(2):
        q2 = q_ref[:, pr * LANES:(pr + 1) * LANES].astype(F32)
        zero = jnp.zeros_like(q2)
        pair = []
        for a in range(2):
            qm = (jnp.where(half0, q2, zero) if a == 0 else jnp.where(half0, zero, q2)).astype(BF16)
            s = jnp.where(ok, _dot_nt(qm, kc), NEG)
            m = jnp.max(s, axis=1, keepdims=True)
            e = jnp.where(ok, jnp.exp(s - m), 0.0)
            l = jnp.sum(e, axis=1, keepdims=True)
            p = e / jnp.where(l > 0.0, l, 1.0)
            psum = psum + p
            pair.append(_dot(p.astype(BF16), vc))
        outs.append(jnp.where(half0, pair[0], pair[1]))
    o_ref[...] = jnp.concatenate(outs, axis=1)

    imp = _dot_exact01(psum, cov_ref[...])
    jl = lane & (HEAD_DIM - 1)
    t_q = lax.broadcasted_iota(jnp.int32, (tq, LANES), 0) + qi * tq
    tb = lax.shift_right_logical(t_q, int(math.log2(L_SEL)))
    valid = jl <= tb
    forced = (jl == 0) | (jl == tb) | (jl == tb - 1)
    score = jnp.where(forced, BIG, jnp.where(valid, imp, -BIG))
    score = jnp.where(half0 & (jl < n_sel), score, LOWEST)
    mm = jnp.full((tq, LANES), -1.0, F32)
    for _ in range(top_n):
        idx = jnp.argmax(score, axis=1, keepdims=True).astype(jnp.int32)
        mm = jnp.where(jl == idx, 0.0, mm)
        score = jnp.where(lane == idx, LOWEST, score)
    mm_ref[...] = mm.astype(mm_ref.dtype)


def _cmp_topk(B, S, q, q_blk0, ckv, cover_dup):
    tq = min(TQ, S)
    nq = S // tq
    T = B * S
    R = ckv.shape[2]
    n_sel = S // L_SEL
    gw = 2 * LANES
    return pl.pallas_call(
        functools.partial(_cmp_kernel, tq=tq, n_sel=n_sel, top_n=min(TOP_N, n_sel)),
        grid=(B, G_B, nq),
        in_specs=[pl.BlockSpec((tq, gw), lambda b, g, i: (b * nq + i, q_blk0 // 2 + g)),
                  pl.BlockSpec((1, 1, R, LANES), lambda b, g, i: (0, b, 0, g)),
                  pl.BlockSpec((1, 1, R, LANES), lambda b, g, i: (1, b, 0, g)),
                  pl.BlockSpec((R, LANES), lambda b, g, i: (0, 0))],
        out_specs=[pl.BlockSpec((tq, gw), lambda b, g, i: (b * nq + i, g)),
                   pl.BlockSpec((tq, LANES), lambda b, g, i: (b * nq + i, g))],
        out_shape=[jax.ShapeDtypeStruct((T, G_B * gw), F32),
                   jax.ShapeDtypeStruct((T, G_B * LANES), BF16)],
        compiler_params=_cparams(("parallel", "parallel", "parallel")),
        name="nsa_cmp_topk",
    )(q, ckv, ckv, cover_dup)


def _post_kernel(aa_ref, oc_ref, os_ref, ow_ref, ac_ref, mg_ref, za_ref, zb_ref, zc_ref, fg_ref,
                 x_ref, p_ref, wa_ref, wb_ref, wc_ref, wo_ref, wg_ref, wp_ref, e_ref, gn_ref,
                 *outs, last):
    def silu(z):
        return z * jax.nn.sigmoid(z)

    W = W_B
    ge = _dot_exact01(jax.nn.sigmoid(fg_ref[...]), e_ref[...])
    ya = aa_ref[...] * silu(za_ref[...])
    yb = (ge[:, :W] * oc_ref[...] + ge[:, W:2 * W] * os_ref[...]
          + ge[:, 2 * W:] * ow_ref[...]) * silu(zb_ref[...])
    yc = ac_ref[...] * silu(zc_ref[...])
    pa = _dot(ya.astype(BF16), wa_ref[...])
    pb = _dot(yb.astype(BF16), wb_ref[...])
    pc = _dot(yc.astype(BF16), wc_ref[...])
    D = pa.shape[1]
    merged = (jax.nn.sigmoid(mg_ref[:, :D]) * pa + jax.nn.sigmoid(mg_ref[:, D:2 * D]) * pb
              + jax.nn.sigmoid(mg_ref[:, 2 * D:]) * pc)
    x1 = x_ref[...] + _dot(merged.astype(BF16), wo_ref[...])
    gate = jax.nn.sigmoid(_dot(x1.astype(BF16), wg_ref[...]))
    x2 = x1 + gate * _dot(p_ref[...].astype(BF16), wp_ref[...])
    y = x2 * lax.rsqrt(jnp.mean(x2 * x2, axis=-1, keepdims=True) + EPS) * gn_ref[...]
    if last:
        outs[0][...] = y
    else:
        outs[0][...] = x2
        outs[1][...] = y.astype(BF16)


def _post(aa, oc, osel, ow, ac, o3, x, p, wa, wb, wc, wo, wg, wp, e_mat, g_next, last):
    T, D = x.shape
    tm = min(POST_TM, T)
    W = W_A
    row = lambda i: (i, 0)
    const = lambda i: (0, 0)
    in_specs = [pl.BlockSpec((tm, W), row)] * 5 + [
        pl.BlockSpec((tm, N_BRANCH * D), lambda i: (i, 0)),
        pl.BlockSpec((tm, W), lambda i: (i, 6)),
        pl.BlockSpec((tm, W), lambda i: (i, 7)),
        pl.BlockSpec((tm, W), lambda i: (i, 8)),
        pl.BlockSpec((tm, LANES), lambda i: (i, 36)),
        pl.BlockSpec((tm, D), row),
        pl.BlockSpec((tm, p.shape[1]), row),
        pl.BlockSpec(wa.shape, const), pl.BlockSpec(wb.shape, const), pl.BlockSpec(wc.shape, const),
        pl.BlockSpec(wo.shape, const), pl.BlockSpec(wg.shape, const), pl.BlockSpec(wp.shape, const),
        pl.BlockSpec(e_mat.shape, const), pl.BlockSpec((1, D), const)]
    if last:
        out_specs = [pl.BlockSpec((tm, D), row)]
        out_shape = [jax.ShapeDtypeStruct((T, D), F32)]
    else:
        out_specs = [pl.BlockSpec((tm, D), row), pl.BlockSpec((tm, D), row)]
        out_shape = [jax.ShapeDtypeStruct((T, D), F32), jax.ShapeDtypeStruct((T, D), BF16)]
    return pl.pallas_call(
        functools.partial(_post_kernel, last=last),
        grid=(T // tm,),
        in_specs=in_specs,
        out_specs=out_specs,
        out_shape=out_shape,
        compiler_params=_cparams(("parallel",)),
        name="post",
    )(aa, oc, osel, ow, ac, o3, o3, o3, o3, o3, x, p, wa, wb, wc, wo, wg, wp, e_mat, g_next)


def _gate_expand_matrix():
    e = np.zeros((LANES, N_BRANCH * W_B), np.float32)
    for h in range(H_B):
        for r in range(N_BRANCH):
            e[H_A + h * N_BRANCH + r, r * W_B + h * HEAD_DIM:r * W_B + (h + 1) * HEAD_DIM] = 1.0
    return jnp.asarray(e, BF16)


def _cover_matrix(R, n_sel):
    c_start = np.arange(R)[:, None] * CMP_STRIDE
    j_start = np.arange(HEAD_DIM)[None, :] * L_SEL
    cov = ((c_start < j_start + L_SEL) & (c_start + L_CMP > j_start)
           & (np.arange(HEAD_DIM)[None, :] < n_sel)).astype(np.float32)
    return jnp.asarray(np.concatenate([cov, cov], axis=1), BF16)


def _layer_weights(w_in_i):
    points = np.cumsum(np.array(SPLIT_SIZES))[:-1].tolist()
    (qa, ka, va, fa, za, qb, kcb, vcb, ksb, vsb, kwb, vwb, gb, zb,
     qc, kc, vc, zc, mg) = jnp.split(w_in_i, points, axis=-1)
    D = w_in_i.shape[0]
    z64 = jnp.zeros((D, HEAD_DIM), w_in_i.dtype)

    def dup(w):
        return jnp.concatenate([w[:, :HEAD_DIM], w[:, :HEAD_DIM], w[:, HEAD_DIM:], w[:, HEAD_DIM:]], axis=1)

    w1 = jnp.concatenate([qa * SCALE, ka, va, qb * SCALE, vc, dup(vsb), dup(vwb)], axis=1)
    ksa = jnp.concatenate([ksb[:, :HEAD_DIM], z64, ksb[:, HEAD_DIM:], z64], axis=1)
    ksb2 = jnp.concatenate([z64, ksb[:, :HEAD_DIM], z64, ksb[:, HEAD_DIM:]], axis=1)
    w2 = jnp.concatenate([qb * SCALE, qc * SCALE, kc, dup(kwb), ksa, ksb2], axis=1)
    pad = jnp.zeros((D, LANES - H_A - 3 * H_B), w_in_i.dtype)
    w3 = jnp.concatenate([mg, za, zb, zc, fa, gb, pad, kcb, vcb], axis=1)
    return w1.astype(BF16), w2.astype(BF16), w3.astype(BF16)


O1_QA, O1_KA, O1_VA, O1_QBU, O1_VC, O1_VS, O1_VW = 0, 4, 8, 12, 16, 20, 22
O2_QBR, O2_QC, O2_KC, O2_KW, O2_KSA, O2_KSB = 0, 4, 8, 12, 14, 16
O3_FG = 36
O3_KCB = 37 * LANES


def kernel(x, p, positions, norm_g, w_in, b_forget, cmp_pe_k, cmp_w1_k, cmp_b1_k, cmp_w2_k,
           cmp_pe_v, cmp_w1_v, cmp_b1_v, cmp_w2_v, diff_lam, diff_subln_g,
           w_br_a, w_br_b, w_br_c, w_out, w_ple, w_ple_gate, final_g):
    B, S, D = x.shape
    depth = w_in.shape[0]
    T = B * S
    R = S // CMP_STRIDE
    n_sel = S // L_SEL
    assert n_sel <= HEAD_DIM and S % min(TQ, S) == 0

    xf = x.reshape(T, D)
    cos_t, sin_t = _rope_tables(positions.astype(F32).reshape(T))
    e_mat = _gate_expand_matrix()
    cover = _cover_matrix(R, n_sel)
    h = _rmsnorm(xf, norm_g[0], BF16)

    for i in range(depth):
        w1, w2, w3 = _layer_weights(w_in[i])
        o1 = _proj(h, w1, BF16, 768, name="proj_plain")
        o2 = _proj(h, w2, BF16, 768, rope_tabs=(cos_t, sin_t), name="proj_rope")
        o3 = _proj(h, w3, F32, 384, name="proj_f32")

        b_pad = jnp.zeros((1, LANES), F32).at[0, :H_A].set(b_forget[i])
        c = _forget_cumsum(o3, O3_FG, b_pad, S)
        c8 = c[:, :H_A].reshape(B, S, H_A // 2, 2)
        cq = c8.transpose(0, 2, 1, 3)
        ck = c8.transpose(0, 2, 3, 1)
        att_a = _flash("fox", B, S, H_A // 2, o1, O1_QA, o1, lambda h_: O1_KA + h_,
                       o1, lambda h_: O1_VA + h_, cq=cq, ck=ck)

        lam_init = 0.8 - 0.6 * math.exp(-0.3 * i)
        att_c = _flash("diff", B, S, H_C, o2, O2_QC, o2, lambda h_: O2_KC + h_,
                       o1, lambda h_: O1_VC + h_, lam=diff_lam[i],
                       subln_g=diff_subln_g[i].reshape(1, LANES), lam_init=lam_init)

        kv = o3[:, O3_KCB:O3_KCB + 2 * LANES].reshape(B, S, 2, G_B, HEAD_DIM)
        tkv = kv.transpose(2, 0, 3, 1, 4).reshape(2, B, G_B, R, CMP_STRIDE * HEAD_DIM)
        pe = jnp.stack([cmp_pe_k[i].reshape(1, -1), cmp_pe_v[i].reshape(1, -1)])
        cw1 = jnp.stack([cmp_w1_k[i], cmp_w1_v[i]]).astype(BF16)
        cb1 = jnp.stack([cmp_b1_k[i].reshape(1, -1), cmp_b1_v[i].reshape(1, -1)])
        cw2 = jnp.stack([cmp_w2_k[i], cmp_w2_v[i]])
        cw2 = jnp.concatenate([cw2, cw2], axis=-1).astype(BF16)
        ckv = _compress(tkv, pe, cw1, cb1, cw2)
        o_cmp, mm = _cmp_topk(B, S, o1, O1_QBU, ckv, cover)
        o_sel = _flash("sel", B, S, H_B // 2, o2, O2_QBR, o2, lambda h_: O2_KSA + h_ // 2,
                       o1, lambda h_: O1_VS + h_ // 2, kb=o2, kb_blk=lambda h_: O2_KSB + h_ // 2, mm=mm)
        o_win = _flash("win", B, S, H_B // 2, o2, O2_QBR, o2, lambda h_: O2_KW + h_ // 2,
                       o1, lambda h_: O1_VW + h_ // 2)

        last = i == depth - 1
        g_next = (final_g if last else norm_g[i + 1]).reshape(1, D)
        res = _post(att_a, o_cmp, o_sel, o_win, att_c, o3, xf, p[i].reshape(T, -1),
                    w_br_a[i].astype(BF16), w_br_b[i].astype(BF16), w_br_c[i].astype(BF16),
                    w_out[i].astype(BF16), w_ple_gate[i].astype(BF16), w_ple[i].astype(BF16),
                    e_mat, g_next, last)
        if last:
            return res[0].reshape(B, S, D)
        xf, h = res
```

```python
import functools
import math

import numpy as np
import jax
import jax.numpy as jnp
from jax import lax
from jax.experimental import pallas as pl
from jax.experimental.pallas import tpu as pltpu

F32 = jnp.float32
BF16 = jnp.bfloat16

LANES = 128
HEAD_DIM = 64
NEG = -1e30
BIG = 1e30
LOWEST = -3e38
EPS = 1e-6
ROPE_THETA = 10000.0
H_A = 8
H_B = 8
G_B = 2
H_C = 4
L_CMP = 32
CMP_STRIDE = 16
CMP_HIDDEN = 256
L_SEL = 64
TOP_N = 16
WINDOW = 512
N_BRANCH = 3
D_MODEL = 1024
W_A = H_A * HEAD_DIM
W_B = H_B * HEAD_DIM
W_C = H_C * 2 * HEAD_DIM
KV_B = G_B * HEAD_DIM
SPLIT_SIZES = (W_A, W_A, W_A, H_A, W_A,
               W_B, KV_B, KV_B, KV_B, KV_B, KV_B, KV_B, 3 * H_B, W_B,
               2 * H_C * HEAD_DIM, 2 * H_C * HEAD_DIM, W_C, W_C,
               N_BRANCH * D_MODEL)
SCALE = HEAD_DIM ** -0.5

VMEM_LIMIT = 48 * 1024 * 1024

TQ = 256
TK = 1024
PROJ_TM = 1024
POST_TM = 256


def _cparams(sem):
    return pltpu.CompilerParams(dimension_semantics=sem, vmem_limit_bytes=VMEM_LIMIT)


def _dot(a, b):
    return jnp.dot(a, b, preferred_element_type=F32)


def _dot_nt(a, b):
    return lax.dot_general(a, b, (((1,), (1,)), ((), ())), preferred_element_type=F32)


def _split3(x):
    hi = x.astype(BF16)
    r1 = x - hi.astype(F32)
    mid = r1.astype(BF16)
    lo = (r1 - mid.astype(F32)).astype(BF16)
    return hi, mid, lo


def _dot_exact01(x, m01):
    hi, mid, lo = _split3(x)
    return _dot(hi, m01) + _dot(mid, m01) + _dot(lo, m01)


def _rmsnorm_kernel(x_ref, g_ref, o_ref):
    x = x_ref[...]
    y = x * lax.rsqrt(jnp.mean(x * x, axis=-1, keepdims=True) + EPS)
    o_ref[...] = (y * g_ref[...]).astype(o_ref.dtype)


def _rmsnorm(x, g, out_dtype, tm=512):
    T, D = x.shape
    return pl.pallas_call(
        _rmsnorm_kernel,
        grid=(T // tm,),
        in_specs=[pl.BlockSpec((tm, D), lambda i: (i, 0)),
                  pl.BlockSpec((1, D), lambda i: (0, 0))],
        out_specs=pl.BlockSpec((tm, D), lambda i: (i, 0)),
        out_shape=jax.ShapeDtypeStruct((T, D), out_dtype),
        compiler_params=_cparams(("parallel",)),
        name="rmsnorm",
    )(x, g.reshape(1, D))


def _rope_table_kernel(pos_ref, invf_ref, sign_ref, cos_ref, sin_ref):
    ang = pos_ref[...] * invf_ref[...]
    cos_ref[...] = jnp.cos(ang)
    sin_ref[...] = jnp.sin(ang) * sign_ref[...]


def _rope_tables(pos_f32, tm=512):
    T = pos_f32.shape[0]
    half = HEAD_DIM // 2
    inv_freq = ROPE_THETA ** (-jnp.arange(half, dtype=F32) / half)
    invf = jnp.tile(inv_freq, LANES // half).reshape(1, LANES)
    sign = jnp.tile(jnp.concatenate([-jnp.ones((half,), F32), jnp.ones((half,), F32)]),
                    LANES // HEAD_DIM).reshape(1, LANES)
    return pl.pallas_call(
        _rope_table_kernel,
        grid=(T // tm,),
        in_specs=[pl.BlockSpec((tm, 1), lambda i: (i, 0)),
                  pl.BlockSpec((1, LANES), lambda i: (0, 0)),
                  pl.BlockSpec((1, LANES), lambda i: (0, 0))],
        out_specs=[pl.BlockSpec((tm, LANES), lambda i: (i, 0)),
                   pl.BlockSpec((tm, LANES), lambda i: (i, 0))],
        out_shape=[jax.ShapeDtypeStruct((T, LANES), F32),
                   jax.ShapeDtypeStruct((T, LANES), F32)],
        compiler_params=_cparams(("parallel",)),
        name="rope_table",
    )(pos_f32.reshape(T, 1), invf, sign)


def _proj_kernel(h_ref, w_ref, *rest, rope):
    acc = _dot(h_ref[...], w_ref[...])
    if rope:
        cos_ref, sin_ref, o_ref = rest
        cos = cos_ref[...]
        sin = sin_ref[...]
        lane = lax.broadcasted_iota(jnp.int32, cos.shape, 1)
        first = (lane & (HEAD_DIM - 1)) < (HEAD_DIM // 2)
        for c in range(acc.shape[1] // LANES):
            t = acc[:, c * LANES:(c + 1) * LANES]
            sw = jnp.where(first, pltpu.roll(t, LANES - HEAD_DIM // 2, 1),
                           pltpu.roll(t, HEAD_DIM // 2, 1))
            o_ref[:, c * LANES:(c + 1) * LANES] = (t * cos + sw * sin).astype(o_ref.dtype)
    else:
        (o_ref,) = rest
        o_ref[...] = acc.astype(o_ref.dtype)


def _proj(h, w, out_dtype, tn, rope_tabs=None, name="proj"):
    T, D = h.shape
    N = w.shape[1]
    tm = min(PROJ_TM, T)
    assert T % tm == 0 and N % tn == 0
    in_specs = [pl.BlockSpec((tm, D), lambda i, j: (i, 0)),
                pl.BlockSpec((D, tn), lambda i, j: (0, j))]
    args = [h, w]
    if rope_tabs is not None:
        in_specs += [pl.BlockSpec((tm, LANES), lambda i, j: (i, 0)),
                     pl.BlockSpec((tm, LANES), lambda i, j: (i, 0))]
        args += list(rope_tabs)
    return pl.pallas_call(
        functools.partial(_proj_kernel, rope=rope_tabs is not None),
        grid=(T // tm, N // tn),
        in_specs=in_specs,
        out_specs=pl.BlockSpec((tm, tn), lambda i, j: (i, j)),
        out_shape=jax.ShapeDtypeStruct((T, N), out_dtype),
        compiler_params=_cparams(("parallel", "arbitrary")),
        name=name,
    )(*args)


def _cumsum_kernel(fg_ref, b_ref, c_ref, carry_sc, *, tiles_per_seq):
    i = pl.program_id(0)

    @pl.when(i % tiles_per_seq == 0)
    def _():
        carry_sc[...] = jnp.zeros_like(carry_sc)

    z = fg_ref[...] + b_ref[...]
    logf = jnp.minimum(z, 0.0) - jnp.log1p(jnp.exp(-jnp.abs(z)))
    tm = z.shape[0]
    r = lax.broadcasted_iota(jnp.int32, (tm, tm), 0)
    c = lax.broadcasted_iota(jnp.int32, (tm, tm), 1)
    tri = jnp.where(c <= r, 1.0, 0.0).astype(BF16)
    cs = _dot_exact01_left(tri, logf) + carry_sc[...]
    c_ref[...] = cs
    carry_sc[...] = cs[tm - 1:tm, :]


def _dot_exact01_left(m01, x):
    hi, mid, lo = _split3(x)
    return _dot(m01, hi) + _dot(m01, mid) + _dot(m01, lo)


def _forget_cumsum(o3, fg_blk, b_pad, S, tm=256):
    T = o3.shape[0]
    return pl.pallas_call(
        functools.partial(_cumsum_kernel, tiles_per_seq=S // tm),
        grid=(T // tm,),
        in_specs=[pl.BlockSpec((tm, LANES), lambda i: (i, fg_blk)),
                  pl.BlockSpec((1, LANES), lambda i: (0, 0))],
        out_specs=pl.BlockSpec((tm, LANES), lambda i: (i, 0)),
        out_shape=jax.ShapeDtypeStruct((T, LANES), F32),
        scratch_shapes=[pltpu.VMEM((1, LANES), F32)],
        compiler_params=_cparams(("arbitrary",)),
        name="forget_cumsum",
    )(o3, b_pad)


def _flash_kernel(*refs, mode, tq, tk, window, lam_init):
    it = iter(refs)
    q_ref = next(it)
    mm_ref = next(it) if mode == "sel" else None
    ka_ref = next(it)
    kb_ref = next(it) if mode == "sel" else None
    v_ref = next(it)
    if mode == "fox":
        ck_ref = next(it)
    if mode == "diff":
        lam_ref = next(it)
        g_ref = next(it)
    o_ref = next(it)
    m_sc = next(it)
    l_sc = next(it)
    acc_sc = next(it)
    if mode == "sel":
        kpa_sc = next(it)
        kpb_sc = next(it)

    qi = pl.program_id(2)
    lane = lax.broadcasted_iota(jnp.int32, (tq, LANES), 1)
    half0 = lane < HEAD_DIM

    if mode == "sel":
        @pl.when(qi == 0)
        def _():
            S = ka_ref.shape[0]
            row = lax.broadcasted_iota(jnp.int32, (S, LANES), 0)
            ln = lax.broadcasted_iota(jnp.int32, (S, LANES), 1)
            blk = lax.shift_right_logical(row, int(math.log2(L_SEL)))
            ea = jnp.where(ln - HEAD_DIM == blk, BIG, 0.0)
            eb = jnp.where(ln == blk, BIG, 0.0)
            kpa_sc[...] = jnp.where(ln >= HEAD_DIM, ea, ka_ref[...].astype(F32)).astype(BF16)
            kpb_sc[...] = jnp.where(ln < HEAD_DIM, eb, kb_ref[...].astype(F32)).astype(BF16)
        kp = (kpa_sc, kpb_sc)
    else:
        kp = (ka_ref, ka_ref)

    q2 = q_ref[...].astype(F32)
    if mode == "sel":
        other = mm_ref[...].astype(F32)
    else:
        other = jnp.zeros_like(q2)
    qm = (jnp.where(half0, q2, other).astype(BF16), jnp.where(half0, other, q2).astype(BF16))

    m_sc[...] = jnp.full_like(m_sc, NEG)
    l_sc[...] = jnp.zeros_like(l_sc)
    acc_sc[...] = jnp.zeros_like(acc_sc)
    nch = tk // LANES
    q0 = qi * tq

    def step(start, masked):
        koff = pl.multiple_of(start, tq)
        if masked:
            d0 = (lax.broadcasted_iota(jnp.int32, (tq, LANES), 0)
                  - lax.broadcasted_iota(jnp.int32, (tq, LANES), 1)) + (q0 - start)
        for a in range(2):
            k = kp[a][pl.ds(koff, tk), :]
            s = _dot_nt(qm[a], k)
            ch = [s[:, c * LANES:(c + 1) * LANES] for c in range(nch)]
            if mode == "fox":
                ck = ck_ref[0, 0, a:a + 1, pl.ds(koff, tk)]
                ch = [ch[c] - ck[:, c * LANES:(c + 1) * LANES] for c in range(nch)]
            if masked:
                for c in range(nch):
                    d = d0 - c * LANES
                    ok = d >= 0
                    if mode == "win":
                        ok = ok & (d < window)
                    ch[c] = jnp.where(ok, ch[c], NEG)
            mx = functools.reduce(jnp.maximum, ch)
            m_old = m_sc[a]
            m_new = jnp.maximum(m_old, jnp.max(mx, axis=1, keepdims=True))
            alpha = jnp.exp(m_old - m_new)
            ps = [jnp.exp(c_ - m_new) for c_ in ch]
            l_sc[a] = alpha * l_sc[a] + functools.reduce(jnp.add, ps)
            p = jnp.concatenate([p_.astype(BF16) for p_ in ps], axis=1)
            v = v_ref[pl.ds(koff, tk), :]
            acc_sc[a] = alpha * acc_sc[a] + _dot(p, v)
            m_sc[a] = m_new

    if mode == "win":
        step(jnp.maximum(q0 + tq - tk, 0), True)
    else:
        n_full = q0 // tk

        def full_body(j, carry):
            step(j * tk, False)
            return carry

        lax.fori_loop(0, n_full, full_body, 0)
        step(n_full * tk, True)

    o0 = acc_sc[0] / jnp.sum(l_sc[0], axis=1, keepdims=True)
    o1 = acc_sc[1] / jnp.sum(l_sc[1], axis=1, keepdims=True)
    if mode == "diff":
        lq = lam_ref[...]
        lam = (jnp.exp(jnp.sum(lq[0:1] * lq[1:2], keepdims=True))
               - jnp.exp(jnp.sum(lq[2:3] * lq[3:4], keepdims=True)) + lam_init)
        d = o0 - lam * o1
        y = d * lax.rsqrt(jnp.mean(d * d, axis=-1, keepdims=True) + EPS)
        o_ref[...] = (y * g_ref[...]) * (1.0 - lam_init)
    else:
        o_ref[...] = jnp.where(half0, o0, o1)


def _flash(mode, B, S, n_blk, q, q_blk0, ka, ka_blk, v, v_blk, *, kb=None, kb_blk=None,
           mm=None, ck=None, lam=None, subln_g=None, lam_init=0.0):
    tq = min(TQ, S)
    tk = min(TK, S)
    nq = S // tq
    T = B * S
    assert S % tk == 0 and tk % tq == 0 and tk >= WINDOW + tq
    in_specs = [pl.BlockSpec((tq, LANES), lambda b, h, i: (b * nq + i, q_blk0 + h))]
    args = [q]
    if mode == "sel":
        in_specs.append(pl.BlockSpec((tq, LANES), lambda b, h, i: (b * nq + i, h // 2)))
        args.append(mm)
    in_specs.append(pl.BlockSpec((S, LANES), lambda b, h, i: (b, ka_blk(h))))
    args.append(ka)
    if mode == "sel":
        in_specs.append(pl.BlockSpec((S, LANES), lambda b, h, i: (b, kb_blk(h))))
        args.append(kb)
    in_specs.append(pl.BlockSpec((S, LANES), lambda b, h, i: (b, v_blk(h))))
    args.append(v)
    if mode == "fox":
        in_specs.append(pl.BlockSpec((1, 1, 2, S), lambda b, h, i: (b, h, 0, 0)))
        args += [ck]
    if mode == "diff":
        in_specs.append(pl.BlockSpec(lam.shape, lambda b, h, i: (0, 0)))
        in_specs.append(pl.BlockSpec((1, LANES), lambda b, h, i: (0, 0)))
        args += [lam, subln_g]
    scratch = [pltpu.VMEM((2, tq, LANES), F32), pltpu.VMEM((2, tq, LANES), F32),
               pltpu.VMEM((2, tq, LANES), F32)]
    if mode == "sel":
        scratch += [pltpu.VMEM((S, LANES), BF16), pltpu.VMEM((S, LANES), BF16)]
    return pl.pallas_call(
        functools.partial(_flash_kernel, mode=mode, tq=tq, tk=tk, window=WINDOW, lam_init=lam_init),
        grid=(B, n_blk, nq),
        in_specs=in_specs,
        out_specs=pl.BlockSpec((tq, LANES), lambda b, h, i: (b * nq + i, h)),
        out_shape=jax.ShapeDtypeStruct((T, n_blk * LANES), F32),
        scratch_shapes=scratch,
        compiler_params=_cparams(("parallel", "parallel", "arbitrary")),
        name="flash_" + mode,
    )(*args)


def _compress_kernel(t_ref, pe_ref, w1_ref, b1_ref, w2_ref, o_ref):
    t = t_ref[0, 0, 0]
    R, half = t.shape
    pe = pe_ref[0]
    xa = (t + pe[:, :half]).astype(BF16)
    xb = (t + pe[:, half:]).astype(BF16)
    a = _dot(xa, w1_ref[0, :half, :])
    b = _dot(xb, w1_ref[0, half:, :])
    hp = a + pltpu.roll(b, R - 1, 0) + b1_ref[0]
    hid = hp * jax.nn.sigmoid(hp)
    o_ref[0, 0] = _dot(hid.astype(BF16), w2_ref[0]).astype(o_ref.dtype)


def _compress(tkv, pe, w1, b1, w2dup):
    _, B, G, R, W = tkv.shape
    return pl.pallas_call(
        _compress_kernel,
        grid=(2, B, G),
        in_specs=[pl.BlockSpec((1, 1, 1, R, W), lambda s, b, g: (s, b, g, 0, 0)),
                  pl.BlockSpec((1, 1, 2 * W), lambda s, b, g: (s, 0, 0)),
                  pl.BlockSpec((1, 2 * W, CMP_HIDDEN), lambda s, b, g: (s, 0, 0)),
                  pl.BlockSpec((1, 1, CMP_HIDDEN), lambda s, b, g: (s, 0, 0)),
                  pl.BlockSpec((1, CMP_HIDDEN, LANES), lambda s, b, g: (s, 0, 0))],
        out_specs=pl.BlockSpec((1, 1, R, LANES), lambda s, b, g: (s, b, 0, g)),
        out_shape=jax.ShapeDtypeStruct((2, B, R, G * LANES), BF16),
        compiler_params=_cparams(("parallel", "parallel", "parallel")),
        name="nsa_compress",
    )(tkv, pe, w1, b1, w2dup)


def _cmp_kernel(q_ref, kc_ref, vc_ref, cov_ref, o_ref, mm_ref, *, tq, n_sel, top_n):
    qi = pl.program_id(2)
    kc = kc_ref[0, 0]
    vc = vc_ref[0, 0]
    R = kc.shape[0]
    lane = lax.broadcasted_iota(jnp.int32, (tq, LANES), 1)
    half0 = lane < HEAD_DIM
    t_r = lax.broadcasted_iota(jnp.int32, (tq, R), 0) + qi * tq
    c_end = lax.broadcasted_iota(jnp.int32, (tq, R), 1) * CMP_STRIDE + (L_CMP - 1)
    ok = c_end <= t_r
    psum = jnp.zeros((tq, R), F32)
    outs = []
    for pr in range(2):
        q2 = q_ref[:, pr * LANES:(pr + 1) * LANES].astype(F32)
        zero = jnp.zeros_like(q2)
        pair = []
        for a in range(2):
            qm = (jnp.where(half0, q2, zero) if a == 0 else jnp.where(half0, zero, q2)).astype(BF16)
            s = jnp.where(ok, _dot_nt(qm, kc), NEG)
            m = jnp.max(s, axis=1, keepdims=True)
            e = jnp.where(ok, jnp.exp(s - m), 0.0)
            l = jnp.sum(e, axis=1, keepdims=True)
            p = e / jnp.where(l > 0.0, l, 1.0)
            psum = psum + p
            pair.append(_dot(p.astype(BF16), vc))
        outs.append(jnp.where(half0, pair[0], pair[1]))
    o_ref[...] = jnp.concatenate(outs, axis=1)

    imp = _dot_exact01(psum, cov_ref[...])
    jl = lane & (HEAD_DIM - 1)
    t_q = lax.broadcasted_iota(jnp.int32, (tq, LANES), 0) + qi * tq
    tb = lax.shift_right_logical(t_q, int(math.log2(L_SEL)))
    valid = jl <= tb
    forced = (jl == 0) | (jl == tb) | (jl == tb - 1)
    score = jnp.where(forced, BIG, jnp.where(valid, imp, -BIG))
    score = jnp.where(half0 & (jl < n_sel), score, LOWEST)
    mm = jnp.full((tq, LANES), -1.0, F32)
    for _ in range(top_n):
        idx = jnp.argmax(score, axis=1, keepdims=True).astype(jnp.int32)
        mm = jnp.where(jl == idx, 0.0, mm)
        score = jnp.where(lane == idx, LOWEST, score)
    mm_ref[...] = mm.astype(mm_ref.dtype)


def _cmp_topk(B, S, q, q_blk0, ckv, cover_dup):
    tq = min(TQ, S)
    nq = S // tq
    T = B * S
    R = ckv.shape[2]
    n_sel = S // L_SEL
    gw = 2 * LANES
    return pl.pallas_call(
        functools.partial(_cmp_kernel, tq=tq, n_sel=n_sel, top_n=min(TOP_N, n_sel)),
        grid=(B, G_B, nq),
        in_specs=[pl.BlockSpec((tq, gw), lambda b, g, i: (b * nq + i, q_blk0 // 2 + g)),
                  pl.BlockSpec((1, 1, R, LANES), lambda b, g, i: (0, b, 0, g)),
                  pl.BlockSpec((1, 1, R, LANES), lambda b, g, i: (1, b, 0, g)),
                  pl.BlockSpec((R, LANES), lambda b, g, i: (0, 0))],
        out_specs=[pl.BlockSpec((tq, gw), lambda b, g, i: (b * nq + i, g)),
                   pl.BlockSpec((tq, LANES), lambda b, g, i: (b * nq + i, g))],
        out_shape=[jax.ShapeDtypeStruct((T, G_B * gw), F32),
                   jax.ShapeDtypeStruct((T, G_B * LANES), BF16)],
        compiler_params=_cparams(("parallel", "parallel", "parallel")),
        name="nsa_cmp_topk",
    )(q, ckv, ckv, cover_dup)


def _post_kernel(aa_ref, oc_ref, os_ref, ow_ref, ac_ref, mg_ref, za_ref, zb_ref, zc_ref, fg_ref,
                 x_ref, p_ref, wa_ref, wb_ref, wc_ref, wo_ref, wg_ref, wp_ref, e_ref, gn_ref,
                 *outs, last):
    def silu(z):
        return z * jax.nn.sigmoid(z)

    W = W_B
    ge = _dot_exact01(jax.nn.sigmoid(fg_ref[...]), e_ref[...])
    ya = aa_ref[...] * silu(za_ref[...])
    yb = (ge[:, :W] * oc_ref[...] + ge[:, W:2 * W] * os_ref[...]
          + ge[:, 2 * W:] * ow_ref[...]) * silu(zb_ref[...])
    yc = ac_ref[...] * silu(zc_ref[...])
    pa = _dot(ya.astype(BF16), wa_ref[...])
    pb = _dot(yb.astype(BF16), wb_ref[...])
    pc = _dot(yc.astype(BF16), wc_ref[...])
    D = pa.shape[1]
    merged = (jax.nn.sigmoid(mg_ref[:, :D]) * pa + jax.nn.sigmoid(mg_ref[:, D:2 * D]) * pb
              + jax.nn.sigmoid(mg_ref[:, 2 * D:]) * pc)
    x1 = x_ref[...] + _dot(merged.astype(BF16), wo_ref[...])
    gate = jax.nn.sigmoid(_dot(x1.astype(BF16), wg_ref[...]))
    x2 = x1 + gate * _dot(p_ref[...].astype(BF16), wp_ref[...])
    y = x2 * lax.rsqrt(jnp.mean(x2 * x2, axis=-1, keepdims=True) + EPS) * gn_ref[...]
    if last:
        outs[0][...] = y
    else:
        outs[0][...] = x2
        outs[1][...] = y.astype(BF16)


def _post(aa, oc, osel, ow, ac, o3, x, p, wa, wb, wc, wo, wg, wp, e_mat, g_next, last):
    T, D = x.shape
    tm = min(POST_TM, T)
    W = W_A
    row = lambda i: (i, 0)
    const = lambda i: (0, 0)
    in_specs = [pl.BlockSpec((tm, W), row)] * 5 + [
        pl.BlockSpec((tm, N_BRANCH * D), lambda i: (i, 0)),
        pl.BlockSpec((tm, W), lambda i: (i, 6)),
        pl.BlockSpec((tm, W), lambda i: (i, 7)),
        pl.BlockSpec((tm, W), lambda i: (i, 8)),
        pl.BlockSpec((tm, LANES), lambda i: (i, 36)),
        pl.BlockSpec((tm, D), row),
        pl.BlockSpec((tm, p.shape[1]), row),
        pl.BlockSpec(wa.shape, const), pl.BlockSpec(wb.shape, const), pl.BlockSpec(wc.shape, const),
        pl.BlockSpec(wo.shape, const), pl.BlockSpec(wg.shape, const), pl.BlockSpec(wp.shape, const),
        pl.BlockSpec(e_mat.shape, const), pl.BlockSpec((1, D), const)]
    if last:
        out_specs = [pl.BlockSpec((tm, D), row)]
        out_shape = [jax.ShapeDtypeStruct((T, D), F32)]
    else:
        out_specs = [pl.BlockSpec((tm, D), row), pl.BlockSpec((tm, D), row)]
        out_shape = [jax.ShapeDtypeStruct((T, D), F32), jax.ShapeDtypeStruct((T, D), BF16)]
    return pl.pallas_call(
        functools.partial(_post_kernel, last=last),
        grid=(T // tm,),
        in_specs=in_specs,
        out_specs=out_specs,
        out_shape=out_shape,
        compiler_params=_cparams(("parallel",)),
        name="post",
    )(aa, oc, osel, ow, ac, o3, o3, o3, o3, o3, x, p, wa, wb, wc, wo, wg, wp, e_mat, g_next)


def _gate_expand_matrix():
    e = np.zeros((LANES, N_BRANCH * W_B), np.float32)
    for h in range(H_B):
        for r in range(N_BRANCH):
            e[H_A + h * N_BRANCH + r, r * W_B + h * HEAD_DIM:r * W_B + (h + 1) * HEAD_DIM] = 1.0
    return jnp.asarray(e, BF16)


def _cover_matrix(R, n_sel):
    c_start = np.arange(R)[:, None] * CMP_STRIDE
    j_start = np.arange(HEAD_DIM)[None, :] * L_SEL
    cov = ((c_start < j_start + L_SEL) & (c_start + L_CMP > j_start)
           & (np.arange(HEAD_DIM)[None, :] < n_sel)).astype(np.float32)
    return jnp.asarray(np.concatenate([cov, cov], axis=1), BF16)


def _layer_weights(w_in_i):
    points = np.cumsum(np.array(SPLIT_SIZES))[:-1].tolist()
    (qa, ka, va, fa, za, qb, kcb, vcb, ksb, vsb, kwb, vwb, gb, zb,
     qc, kc, vc, zc, mg) = jnp.split(w_in_i, points, axis=-1)
    D = w_in_i.shape[0]
    z64 = jnp.zeros((D, HEAD_DIM), w_in_i.dtype)

    def dup(w):
        return jnp.concatenate([w[:, :HEAD_DIM], w[:, :HEAD_DIM], w[:, HEAD_DIM:], w[:, HEAD_DIM:]], axis=1)

    w1 = jnp.concatenate([qa * SCALE, ka, va, qb * SCALE, vc, dup(vsb), dup(vwb)], axis=1)
    ksa = jnp.concatenate([ksb[:, :HEAD_DIM], z64, ksb[:, HEAD_DIM:], z64], axis=1)
    ksb2 = jnp.concatenate([z64, ksb[:, :HEAD_DIM], z64, ksb[:, HEAD_DIM:]], axis=1)
    w2 = jnp.concatenate([qb * SCALE, qc * SCALE, kc, dup(kwb), ksa, ksb2], axis=1)
    pad = jnp.zeros((D, LANES - H_A - 3 * H_B), w_in_i.dtype)
    w3 = jnp.concatenate([mg, za, zb, zc, fa, gb, pad, kcb, vcb], axis=1)
    return w1.astype(BF16), w2.astype(BF16), w3.astype(BF16)


O1_QA, O1_KA, O1_VA, O1_QBU, O1_VC, O1_VS, O1_VW = 0, 4, 8, 12, 16, 20, 22
O2_QBR, O2_QC, O2_KC, O2_KW, O2_KSA, O2_KSB = 0, 4, 8, 12, 14, 16
O3_FG = 36
O3_KCB = 37 * LANES


def kernel(x, p, positions, norm_g, w_in, b_forget, cmp_pe_k, cmp_w1_k, cmp_b1_k, cmp_w2_k,
           cmp_pe_v, cmp_w1_v, cmp_b1_v, cmp_w2_v, diff_lam, diff_subln_g,
           w_br_a, w_br_b, w_br_c, w_out, w_ple, w_ple_gate, final_g):
    B, S, D = x.shape
    depth = w_in.shape[0]
    T = B * S
    R = S // CMP_STRIDE
    n_sel = S // L_SEL
    assert n_sel <= HEAD_DIM and S % min(TQ, S) == 0

    xf = x.reshape(T, D)
    cos_t, sin_t = _rope_tables(positions.astype(F32).reshape(T))
    e_mat = _gate_expand_matrix()
    cover = _cover_matrix(R, n_sel)
    h = _rmsnorm(xf, norm_g[0], BF16)

    for i in range(depth):
        w1, w2, w3 = _layer_weights(w_in[i])
        o1 = _proj(h, w1, BF16, 768, name="proj_plain")
        o2 = _proj(h, w2, BF16, 768, rope_tabs=(cos_t, sin_t), name="proj_rope")
        o3 = _proj(h, w3, F32, 384, name="proj_f32")

        b_pad = jnp.zeros((1, LANES), F32).at[0, :H_A].set(b_forget[i])
        c = _forget_cumsum(o3, O3_FG, b_pad, S)
        ck = c[:, :H_A].reshape(B, S, H_A // 2, 2).transpose(0, 2, 3, 1)
        att_a = _flash("fox", B, S, H_A // 2, o1, O1_QA, o1, lambda h_: O1_KA + h_,
                       o1, lambda h_: O1_VA + h_, ck=ck)

        lam_init = 0.8 - 0.6 * math.exp(-0.3 * i)
        att_c = _flash("diff", B, S, H_C, o2, O2_QC, o2, lambda h_: O2_KC + h_,
                       o1, lambda h_: O1_VC + h_, lam=diff_lam[i],
                       subln_g=diff_subln_g[i].reshape(1, LANES), lam_init=lam_init)

        kv = o3[:, O3_KCB:O3_KCB + 2 * LANES].reshape(B, S, 2, G_B, HEAD_DIM)
        tkv = kv.transpose(2, 0, 3, 1, 4).reshape(2, B, G_B, R, CMP_STRIDE * HEAD_DIM)
        pe = jnp.stack([cmp_pe_k[i].reshape(1, -1), cmp_pe_v[i].reshape(1, -1)])
        cw1 = jnp.stack([cmp_w1_k[i], cmp_w1_v[i]]).astype(BF16)
        cb1 = jnp.stack([cmp_b1_k[i].reshape(1, -1), cmp_b1_v[i].reshape(1, -1)])
        cw2 = jnp.stack([cmp_w2_k[i], cmp_w2_v[i]])
        cw2 = jnp.concatenate([cw2, cw2], axis=-1).astype(BF16)
        ckv = _compress(tkv, pe, cw1, cb1, cw2)
        o_cmp, mm = _cmp_topk(B, S, o1, O1_QBU, ckv, cover)
        o_sel = _flash("sel", B, S, H_B // 2, o2, O2_QBR, o2, lambda h_: O2_KSA + h_ // 2,
                       o1, lambda h_: O1_VS + h_ // 2, kb=o2, kb_blk=lambda h_: O2_KSB + h_ // 2, mm=mm)
        o_win = _flash("win", B, S, H_B // 2, o2, O2_QBR, o2, lambda h_: O2_KW + h_ // 2,
                       o1, lambda h_: O1_VW + h_ // 2)

        last = i == depth - 1
        g_next = (final_g if last else norm_g[i + 1]).reshape(1, D)
        res = _post(att_a, o_cmp, o_sel, o_win, att_c, o3, xf, p[i].reshape(T, -1),
                    w_br_a[i].astype(BF16), w_br_b[i].astype(BF16), w_br_c[i].astype(BF16),
                    w_out[i].astype(BF16), w_ple_gate[i].astype(BF16), w_ple[i].astype(BF16),
                    e_mat, g_next, last)
        if last:
            return res[0].reshape(B, S, D)
        xf, h = res
```

```python
import functools
import math

import numpy as np
import jax
import jax.numpy as jnp
from jax import lax
from jax.experimental import pallas as pl
from jax.experimental.pallas import tpu as pltpu

F32 = jnp.float32
BF16 = jnp.bfloat16

LANES = 128
HEAD_DIM = 64
NEG = -1e30
BIG = 1e30
LOWEST = -3e38
EPS = 1e-6
ROPE_THETA = 10000.0
H_A = 8
H_B = 8
G_B = 2
H_C = 4
L_CMP = 32
CMP_STRIDE = 16
CMP_HIDDEN = 256
L_SEL = 64
TOP_N = 16
WINDOW = 512
N_BRANCH = 3
D_MODEL = 1024
W_A = H_A * HEAD_DIM
W_B = H_B * HEAD_DIM
W_C = H_C * 2 * HEAD_DIM
KV_B = G_B * HEAD_DIM
SPLIT_SIZES = (W_A, W_A, W_A, H_A, W_A,
               W_B, KV_B, KV_B, KV_B, KV_B, KV_B, KV_B, 3 * H_B, W_B,
               2 * H_C * HEAD_DIM, 2 * H_C * HEAD_DIM, W_C, W_C,
               N_BRANCH * D_MODEL)
SCALE = HEAD_DIM ** -0.5
LOG2E = math.log2(math.e)

VMEM_LIMIT = 48 * 1024 * 1024

TQ = 512
TK = 1024
FLASH_ROWS = 128
PROJ_TM = 1024
POST_TM = 256


def _cparams(sem):
    return pltpu.CompilerParams(dimension_semantics=sem, vmem_limit_bytes=VMEM_LIMIT)


def _dot(a, b):
    return jnp.dot(a, b, preferred_element_type=F32)


def _dot_nt(a, b):
    return lax.dot_general(a, b, (((1,), (1,)), ((), ())), preferred_element_type=F32)


def _split3(x):
    hi = x.astype(BF16)
    r1 = x - hi.astype(F32)
    mid = r1.astype(BF16)
    lo = (r1 - mid.astype(F32)).astype(BF16)
    return hi, mid, lo


def _dot_exact01(x, m01):
    hi, mid, lo = _split3(x)
    return _dot(hi, m01) + _dot(mid, m01) + _dot(lo, m01)


def _rmsnorm_kernel(x_ref, g_ref, o_ref):
    x = x_ref[...]
    y = x * lax.rsqrt(jnp.mean(x * x, axis=-1, keepdims=True) + EPS)
    o_ref[...] = (y * g_ref[...]).astype(o_ref.dtype)


def _rmsnorm(x, g, out_dtype, tm=512):
    T, D = x.shape
    return pl.pallas_call(
        _rmsnorm_kernel,
        grid=(T // tm,),
        in_specs=[pl.BlockSpec((tm, D), lambda i: (i, 0)),
                  pl.BlockSpec((1, D), lambda i: (0, 0))],
        out_specs=pl.BlockSpec((tm, D), lambda i: (i, 0)),
        out_shape=jax.ShapeDtypeStruct((T, D), out_dtype),
        compiler_params=_cparams(("parallel",)),
        name="rmsnorm",
    )(x, g.reshape(1, D))


def _rope_table_kernel(pos_ref, invf_ref, sign_ref, cos_ref, sin_ref):
    ang = pos_ref[...] * invf_ref[...]
    cos_ref[...] = jnp.cos(ang)
    sin_ref[...] = jnp.sin(ang) * sign_ref[...]


def _rope_tables(pos_f32, tm=512):
    T = pos_f32.shape[0]
    half = HEAD_DIM // 2
    inv_freq = ROPE_THETA ** (-jnp.arange(half, dtype=F32) / half)
    invf = jnp.tile(inv_freq, LANES // half).reshape(1, LANES)
    sign = jnp.tile(jnp.concatenate([-jnp.ones((half,), F32), jnp.ones((half,), F32)]),
                    LANES // HEAD_DIM).reshape(1, LANES)
    return pl.pallas_call(
        _rope_table_kernel,
        grid=(T // tm,),
        in_specs=[pl.BlockSpec((tm, 1), lambda i: (i, 0)),
                  pl.BlockSpec((1, LANES), lambda i: (0, 0)),
                  pl.BlockSpec((1, LANES), lambda i: (0, 0))],
        out_specs=[pl.BlockSpec((tm, LANES), lambda i: (i, 0)),
                   pl.BlockSpec((tm, LANES), lambda i: (i, 0))],
        out_shape=[jax.ShapeDtypeStruct((T, LANES), F32),
                   jax.ShapeDtypeStruct((T, LANES), F32)],
        compiler_params=_cparams(("parallel",)),
        name="rope_table",
    )(pos_f32.reshape(T, 1), invf, sign)


def _proj_kernel(h_ref, w_ref, *rest, rope):
    acc = _dot(h_ref[...], w_ref[...])
    if rope:
        cos_ref, sin_ref, o_ref = rest
        cos = cos_ref[...]
        sin = sin_ref[...]
        lane = lax.broadcasted_iota(jnp.int32, cos.shape, 1)
        first = (lane & (HEAD_DIM - 1)) < (HEAD_DIM // 2)
        for c in range(acc.shape[1] // LANES):
            t = acc[:, c * LANES:(c + 1) * LANES]
            sw = jnp.where(first, pltpu.roll(t, LANES - HEAD_DIM // 2, 1),
                           pltpu.roll(t, HEAD_DIM // 2, 1))
            o_ref[:, c * LANES:(c + 1) * LANES] = (t * cos + sw * sin).astype(o_ref.dtype)
    else:
        (o_ref,) = rest
        o_ref[...] = acc.astype(o_ref.dtype)


def _proj(h, w, out_dtype, tn, rope_tabs=None, name="proj"):
    T, D = h.shape
    N = w.shape[1]
    tm = min(PROJ_TM, T)
    assert T % tm == 0 and N % tn == 0
    in_specs = [pl.BlockSpec((tm, D), lambda i, j: (i, 0)),
                pl.BlockSpec((D, tn), lambda i, j: (0, j))]
    args = [h, w]
    if rope_tabs is not None:
        in_specs += [pl.BlockSpec((tm, LANES), lambda i, j: (i, 0)),
                     pl.BlockSpec((tm, LANES), lambda i, j: (i, 0))]
        args += list(rope_tabs)
    return pl.pallas_call(
        functools.partial(_proj_kernel, rope=rope_tabs is not None),
        grid=(T // tm, N // tn),
        in_specs=in_specs,
        out_specs=pl.BlockSpec((tm, tn), lambda i, j: (i, j)),
        out_shape=jax.ShapeDtypeStruct((T, N), out_dtype),
        compiler_params=_cparams(("parallel", "arbitrary")),
        name=name,
    )(*args)


def _cumsum_kernel(fg_ref, b_ref, c_ref, carry_sc, *, tiles_per_seq):
    i = pl.program_id(0)

    @pl.when(i % tiles_per_seq == 0)
    def _():
        carry_sc[...] = jnp.zeros_like(carry_sc)

    z = fg_ref[...] + b_ref[...]
    logf = jnp.minimum(z, 0.0) - jnp.log1p(jnp.exp(-jnp.abs(z)))
    tm = z.shape[0]
    r = lax.broadcasted_iota(jnp.int32, (tm, tm), 0)
    c = lax.broadcasted_iota(jnp.int32, (tm, tm), 1)
    tri = jnp.where(c <= r, 1.0, 0.0).astype(BF16)
    cs = _dot_exact01_left(tri, logf) + carry_sc[...]
    c_ref[...] = cs * LOG2E
    carry_sc[...] = cs[tm - 1:tm, :]


def _dot_exact01_left(m01, x):
    hi, mid, lo = _split3(x)
    return _dot(m01, hi) + _dot(m01, mid) + _dot(m01, lo)


def _forget_cumsum(o3, fg_blk, b_pad, S, tm=256):
    T = o3.shape[0]
    return pl.pallas_call(
        functools.partial(_cumsum_kernel, tiles_per_seq=S // tm),
        grid=(T // tm,),
        in_specs=[pl.BlockSpec((tm, LANES), lambda i: (i, fg_blk)),
                  pl.BlockSpec((1, LANES), lambda i: (0, 0))],
        out_specs=pl.BlockSpec((tm, LANES), lambda i: (i, 0)),
        out_shape=jax.ShapeDtypeStruct((T, LANES), F32),
        scratch_shapes=[pltpu.VMEM((1, LANES), F32)],
        compiler_params=_cparams(("arbitrary",)),
        name="forget_cumsum",
    )(o3, b_pad)


def _flash_kernel(*refs, mode, tq, tk, window, lam_init):
    it = iter(refs)
    q_ref = next(it)
    mm_ref = next(it) if mode == "sel" else None
    k_ref = next(it)
    v_ref = next(it)
    if mode == "fox":
        ck_ref = next(it)
    if mode == "diff":
        lam_ref = next(it)
        g_ref = next(it)
    o_ref = next(it)
    m_sc = next(it)
    acc_sc = next(it)
    v1_sc = next(it)
    s0_sc = next(it)
    s1_sc = next(it)
    if mode == "sel":
        ke_sc = next(it)

    qi = pl.program_id(2)
    S = k_ref.shape[0]
    lane = lax.broadcasted_iota(jnp.int32, (tq, LANES), 1)
    half0 = lane < HEAD_DIM

    @pl.when(qi == 0)
    def _():
        v1_sc[:, :LANES] = v_ref[...]
        v1_sc[:, LANES:] = jnp.ones((S, LANES), BF16)
        if mode == "sel":
            row = lax.broadcasted_iota(jnp.int32, (S, LANES), 0)
            ln = lax.broadcasted_iota(jnp.int32, (S, LANES), 1)
            blk = lax.shift_right_logical(row, int(math.log2(L_SEL)))
            ke_sc[:, :LANES] = k_ref[...]
            ke_sc[:, LANES:] = jnp.where(ln == blk, BIG, 0.0).astype(BF16)

    kk = ke_sc if mode == "sel" else k_ref

    q2 = q_ref[...].astype(F32)
    zero = jnp.zeros_like(q2)
    qa = jnp.where(half0, q2, zero).astype(BF16)
    qb = jnp.where(half0, zero, q2).astype(BF16)
    if mode == "sel":
        mmh = jnp.where(half0, mm_ref[...].astype(F32), zero).astype(BF16)
        qa = jnp.concatenate([qa, mmh], axis=1)
        qb = jnp.concatenate([qb, mmh], axis=1)
    qm = (qa, qb)

    m_sc[...] = jnp.full_like(m_sc, NEG)
    acc_sc[...] = jnp.zeros_like(acc_sc)
    q0 = qi * tq
    rb = LANES
    nrb = tq // rb
    dsq = (lax.broadcasted_iota(jnp.int32, (rb, LANES), 0)
           - lax.broadcasted_iota(jnp.int32, (rb, LANES), 1))

    nch = tq // LANES

    def qk(start, s_sc):
        koff = pl.multiple_of(start, tq)
        kt = kk[pl.ds(koff, tq), :]
        for a in range(2):
            s = _dot_nt(qm[a], kt)
            if mode == "fox":
                s = s - ck_ref[0, 0, a:a + 1, pl.ds(koff, tq)]
            s_sc[a] = s

    def soft_pv(s_sc, start, kind):
        koff = pl.multiple_of(start, tq)
        vt = v1_sc[pl.ds(koff, tq), :]
        for a in range(2):
            for rbi in range(nrb):
                rows = slice(rbi * rb, (rbi + 1) * rb)
                if kind == "full":
                    c_lo, c_hi, c_edge = 0, nch, None
                elif kind == "diag":
                    c_lo, c_hi, c_edge = 0, rbi + 1, rbi
                else:
                    c_lo, c_hi, c_edge = rbi, nch, rbi
                ch = [s_sc[a, rows, c * LANES:(c + 1) * LANES] for c in range(c_lo, c_hi)]
                if c_edge is not None:
                    ok = dsq >= 0 if kind == "diag" else dsq < 0
                    ch[c_edge - c_lo] = jnp.where(ok, ch[c_edge - c_lo], NEG)
                mx = functools.reduce(jnp.maximum, ch)
                m_old = m_sc[a, rows]
                m_new = jnp.maximum(m_old, jnp.max(mx, axis=1, keepdims=True))
                alpha = jnp.exp2(m_old - m_new)
                p = jnp.concatenate([jnp.exp2(c_ - m_new).astype(BF16) for c_ in ch], axis=1)
                m_sc[a, rows] = m_new
                al = jnp.concatenate([alpha, alpha], axis=1)
                acc_sc[a, rows] = (al * acc_sc[a, rows]
                                   + _dot(p, vt[c_lo * LANES:c_hi * LANES]))

    if mode == "win":
        @pl.when(qi > 0)
        def _():
            qk(q0 - window, s0_sc)
            qk(q0, s1_sc)
            soft_pv(s0_sc, q0 - window, "prev")
            soft_pv(s1_sc, q0, "diag")

        @pl.when(qi == 0)
        def _():
            qk(q0, s1_sc)
            soft_pv(s1_sc, q0, "diag")
    else:
        qk(0, s0_sc)

        def pair_body(i, carry):
            t0 = 2 * i * tq
            qk(t0 + tq, s1_sc)
            soft_pv(s0_sc, t0, "full")
            qk(t0 + 2 * tq, s0_sc)
            soft_pv(s1_sc, t0 + tq, "full")
            return carry

        lax.fori_loop(0, qi // 2, pair_body, 0)

        @pl.when(qi % 2 == 1)
        def _():
            qk(q0, s1_sc)
            soft_pv(s0_sc, q0 - tq, "full")
            soft_pv(s1_sc, q0, "diag")

        @pl.when(qi % 2 == 0)
        def _():
            soft_pv(s0_sc, q0, "diag")

    o0 = acc_sc[0, :, :LANES] / acc_sc[0, :, LANES:]
    o1 = acc_sc[1, :, :LANES] / acc_sc[1, :, LANES:]
    if mode == "diff":
        lq = lam_ref[...]
        lam = (jnp.exp(jnp.sum(lq[0:1] * lq[1:2], keepdims=True))
               - jnp.exp(jnp.sum(lq[2:3] * lq[3:4], keepdims=True)) + lam_init)
        d = o0 - lam * o1
        y = d * lax.rsqrt(jnp.mean(d * d, axis=-1, keepdims=True) + EPS)
        o_ref[...] = (y * g_ref[...]) * (1.0 - lam_init)
    else:
        o_ref[...] = jnp.where(half0, o0, o1)


def _flash(mode, B, S, n_blk, q, q_blk0, k, k_blk, v, v_blk, *,
           mm=None, ck=None, lam=None, subln_g=None, lam_init=0.0):
    tq = min(TQ, S)
    tk = min(TK, S)
    nq = S // tq
    T = B * S
    assert S % tk == 0 and tk == 2 * tq and WINDOW == tq and tq % LANES == 0
    in_specs = [pl.BlockSpec((tq, LANES), lambda b, h, i: (b * nq + i, q_blk0 + h))]
    args = [q]
    if mode == "sel":
        in_specs.append(pl.BlockSpec((tq, LANES), lambda b, h, i: (b * nq + i, h // 2)))
        args.append(mm)
    in_specs.append(pl.BlockSpec((S, LANES), lambda b, h, i: (b, k_blk(h))))
    args.append(k)
    in_specs.append(pl.BlockSpec((S, LANES), lambda b, h, i: (b, v_blk(h))))
    args.append(v)
    if mode == "fox":
        in_specs.append(pl.BlockSpec((1, 1, 2, S), lambda b, h, i: (b, h, 0, 0)))
        args += [ck]
    if mode == "diff":
        in_specs.append(pl.BlockSpec(lam.shape, lambda b, h, i: (0, 0)))
        in_specs.append(pl.BlockSpec((1, LANES), lambda b, h, i: (0, 0)))
        args += [lam, subln_g]
    scratch = [pltpu.VMEM((2, tq, LANES), F32), pltpu.VMEM((2, tq, 2 * LANES), F32),
               pltpu.VMEM((S, 2 * LANES), BF16),
               pltpu.VMEM((2, tq, tq), F32), pltpu.VMEM((2, tq, tq), F32)]
    if mode == "sel":
        scratch += [pltpu.VMEM((S, 2 * LANES), BF16)]
    return pl.pallas_call(
        functools.partial(_flash_kernel, mode=mode, tq=tq, tk=tk, window=WINDOW, lam_init=lam_init),
        grid=(B, n_blk, nq),
        in_specs=in_specs,
        out_specs=pl.BlockSpec((tq, LANES), lambda b, h, i: (b * nq + i, h)),
        out_shape=jax.ShapeDtypeStruct((T, n_blk * LANES), F32),
        scratch_shapes=scratch,
        compiler_params=_cparams(("parallel", "parallel", "arbitrary")),
        name="flash_" + mode,
    )(*args)


def _compress_kernel(t_ref, pe_ref, w1_ref, b1_ref, w2_ref, o_ref):
    t = t_ref[0, 0, 0]
    R, half = t.shape
    pe = pe_ref[0]
    xa = (t + pe[:, :half]).astype(BF16)
    xb = (t + pe[:, half:]).astype(BF16)
    a = _dot(xa, w1_ref[0, :half, :])
    b = _dot(xb, w1_ref[0, half:, :])
    hp = a + pltpu.roll(b, R - 1, 0) + b1_ref[0]
    hid = hp * jax.nn.sigmoid(hp)
    o_ref[0, 0] = _dot(hid.astype(BF16), w2_ref[0]).astype(o_ref.dtype)


def _compress(tkv, pe, w1, b1, w2dup):
    _, B, G, R, W = tkv.shape
    return pl.pallas_call(
        _compress_kernel,
        grid=(2, B, G),
        in_specs=[pl.BlockSpec((1, 1, 1, R, W), lambda s, b, g: (s, b, g, 0, 0)),
                  pl.BlockSpec((1, 1, 2 * W), lambda s, b, g: (s, 0, 0)),
                  pl.BlockSpec((1, 2 * W, CMP_HIDDEN), lambda s, b, g: (s, 0, 0)),
                  pl.BlockSpec((1, 1, CMP_HIDDEN), lambda s, b, g: (s, 0, 0)),
                  pl.BlockSpec((1, CMP_HIDDEN, LANES), lambda s, b, g: (s, 0, 0))],
        out_specs=pl.BlockSpec((1, 1, R, LANES), lambda s, b, g: (s, b, 0, g)),
        out_shape=jax.ShapeDtypeStruct((2, B, R, G * LANES), BF16),
        compiler_params=_cparams(("parallel", "parallel", "parallel")),
        name="nsa_compress",
    )(tkv, pe, w1, b1, w2dup)


def _cmp_kernel(q_ref, kc_ref, vc_ref, cov_ref, o_ref, mm_ref, *, tq, n_sel, top_n):
    qi = pl.program_id(2)
    kc = kc_ref[0, 0]
    vc = vc_ref[0, 0]
    R = kc.shape[0]
    lane = lax.broadcasted_iota(jnp.int32, (tq, LANES), 1)
    half0 = lane < HEAD_DIM
    t_r = lax.broadcasted_iota(jnp.int32, (tq, R), 0) + qi * tq
    c_end = lax.broadcasted_iota(jnp.int32, (tq, R), 1) * CMP_STRIDE + (L_CMP - 1)
    ok = c_end <= t_r
    psum = jnp.zeros((tq, R), F32)
    outs = []
    for pr in range(2):
        q2 = q_ref[:, pr * LANES:(pr + 1) * LANES].astype(F32)
        zero = jnp.zeros_like(q2)
        pair = []
        for a in range(2):
            qm = (jnp.where(half0, q2, zero) if a == 0 else jnp.where(half0, zero, q2)).astype(BF16)
            s = jnp.where(ok, _dot_nt(qm, kc), NEG)
            m = jnp.max(s, axis=1, keepdims=True)
            e = jnp.where(ok, jnp.exp(s - m), 0.0)
            l = jnp.sum(e, axis=1, keepdims=True)
            p = e / jnp.where(l > 0.0, l, 1.0)
            psum = psum + p
            pair.append(_dot(p.astype(BF16), vc))
        outs.append(jnp.where(half0, pair[0], pair[1]))
    o_ref[...] = jnp.concatenate(outs, axis=1)

    imp = _dot_exact01(psum, cov_ref[...])
    jl = lane & (HEAD_DIM - 1)
    t_q = lax.broadcasted_iota(jnp.int32, (tq, LANES), 0) + qi * tq
    tb = lax.shift_right_logical(t_q, int(math.log2(L_SEL)))
    valid = jl <= tb
    forced = (jl == 0) | (jl == tb) | (jl == tb - 1)
    score = jnp.where(forced, BIG, jnp.where(valid, imp, -BIG))
    score = jnp.where(half0 & (jl < n_sel), score, LOWEST)
    mm = jnp.full((tq, LANES), -1.0, F32)
    for _ in range(top_n):
        idx = jnp.argmax(score, axis=1, keepdims=True).astype(jnp.int32)
        mm = jnp.where(jl == idx, 0.0, mm)
        score = jnp.where(lane == idx, LOWEST, score)
    mm_ref[...] = mm.astype(mm_ref.dtype)


def _cmp_topk(B, S, q, q_blk0, ckv, cover_dup):
    tq = min(TQ, S)
    nq = S // tq
    T = B * S
    R = ckv.shape[2]
    n_sel = S // L_SEL
    gw = 2 * LANES
    return pl.pallas_call(
        functools.partial(_cmp_kernel, tq=tq, n_sel=n_sel, top_n=min(TOP_N, n_sel)),
        grid=(B, G_B, nq),
        in_specs=[pl.BlockSpec((tq, gw), lambda b, g, i: (b * nq + i, q_blk0 // 2 + g)),
                  pl.BlockSpec((1, 1, R, LANES), lambda b, g, i: (0, b, 0, g)),
                  pl.BlockSpec((1, 1, R, LANES), lambda b, g, i: (1, b, 0, g)),
                  pl.BlockSpec((R, LANES), lambda b, g, i: (0, 0))],
        out_specs=[pl.BlockSpec((tq, gw), lambda b, g, i: (b * nq + i, g)),
                   pl.BlockSpec((tq, LANES), lambda b, g, i: (b * nq + i, g))],
        out_shape=[jax.ShapeDtypeStruct((T, G_B * gw), F32),
                   jax.ShapeDtypeStruct((T, G_B * LANES), BF16)],
        compiler_params=_cparams(("parallel", "parallel", "parallel")),
        name="nsa_cmp_topk",
    )(q, ckv, ckv, cover_dup)


def _post_kernel(aa_ref, oc_ref, os_ref, ow_ref, ac_ref, mg_ref, za_ref, zb_ref, zc_ref, fg_ref,
                 x_ref, p_ref, wa_ref, wb_ref, wc_ref, wo_ref, wg_ref, wp_ref, e_ref, gn_ref,
                 *outs, last):
    def silu(z):
        return z * jax.nn.sigmoid(z)

    W = W_B
    ge = _dot_exact01(jax.nn.sigmoid(fg_ref[...]), e_ref[...])
    ya = aa_ref[...] * silu(za_ref[...])
    yb = (ge[:, :W] * oc_ref[...] + ge[:, W:2 * W] * os_ref[...]
          + ge[:, 2 * W:] * ow_ref[...]) * silu(zb_ref[...])
    yc = ac_ref[...] * silu(zc_ref[...])
    pa = _dot(ya.astype(BF16), wa_ref[...])
    pb = _dot(yb.astype(BF16), wb_ref[...])
    pc = _dot(yc.astype(BF16), wc_ref[...])
    D = pa.shape[1]
    merged = (jax.nn.sigmoid(mg_ref[:, :D]) * pa + jax.nn.sigmoid(mg_ref[:, D:2 * D]) * pb
              + jax.nn.sigmoid(mg_ref[:, 2 * D:]) * pc)
    x1 = x_ref[...] + _dot(merged.astype(BF16), wo_ref[...])
    gate = jax.nn.sigmoid(_dot(x1.astype(BF16), wg_ref[...]))
    x2 = x1 + gate * _dot(p_ref[...].astype(BF16), wp_ref[...])
    y = x2 * lax.rsqrt(jnp.mean(x2 * x2, axis=-1, keepdims=True) + EPS) * gn_ref[...]
    if last:
        outs[0][...] = y
    else:
        outs[0][...] = x2
        outs[1][...] = y.astype(BF16)


def _post(aa, oc, osel, ow, ac, o3, x, p, wa, wb, wc, wo, wg, wp, e_mat, g_next, last):
    T, D = x.shape
    tm = min(POST_TM, T)
    W = W_A
    row = lambda i: (i, 0)
    const = lambda i: (0, 0)
    in_specs = [pl.BlockSpec((tm, W), row)] * 5 + [
        pl.BlockSpec((tm, N_BRANCH * D), lambda i: (i, 0)),
        pl.BlockSpec((tm, W), lambda i: (i, 6)),
        pl.BlockSpec((tm, W), lambda i: (i, 7)),
        pl.BlockSpec((tm, W), lambda i: (i, 8)),
        pl.BlockSpec((tm, LANES), lambda i: (i, 36)),
        pl.BlockSpec((tm, D), row),
        pl.BlockSpec((tm, p.shape[1]), row),
        pl.BlockSpec(wa.shape, const), pl.BlockSpec(wb.shape, const), pl.BlockSpec(wc.shape, const),
        pl.BlockSpec(wo.shape, const), pl.BlockSpec(wg.shape, const), pl.BlockSpec(wp.shape, const),
        pl.BlockSpec(e_mat.shape, const), pl.BlockSpec((1, D), const)]
    if last:
        out_specs = [pl.BlockSpec((tm, D), row)]
        out_shape = [jax.ShapeDtypeStruct((T, D), F32)]
    else:
        out_specs = [pl.BlockSpec((tm, D), row), pl.BlockSpec((tm, D), row)]
        out_shape = [jax.ShapeDtypeStruct((T, D), F32), jax.ShapeDtypeStruct((T, D), BF16)]
    return pl.pallas_call(
        functools.partial(_post_kernel, last=last),
        grid=(T // tm,),
        in_specs=in_specs,
        out_specs=out_specs,
        out_shape=out_shape,
        compiler_params=_cparams(("parallel",)),
        name="post",
    )(aa, oc, osel, ow, ac, o3, o3, o3, o3, o3, x, p, wa, wb, wc, wo, wg, wp, e_mat, g_next)


def _gate_expand_matrix():
    e = np.zeros((LANES, N_BRANCH * W_B), np.float32)
    for h in range(H_B):
        for r in range(N_BRANCH):
            e[H_A + h * N_BRANCH + r, r * W_B + h * HEAD_DIM:r * W_B + (h + 1) * HEAD_DIM] = 1.0
    return jnp.asarray(e, BF16)


def _cover_matrix(R, n_sel):
    c_start = np.arange(R)[:, None] * CMP_STRIDE
    j_start = np.arange(HEAD_DIM)[None, :] * L_SEL
    cov = ((c_start < j_start + L_SEL) & (c_start + L_CMP > j_start)
           & (np.arange(HEAD_DIM)[None, :] < n_sel)).astype(np.float32)
    return jnp.asarray(np.concatenate([cov, cov], axis=1), BF16)


def _layer_weights(w_in_i):
    points = np.cumsum(np.array(SPLIT_SIZES))[:-1].tolist()
    (qa, ka, va, fa, za, qb, kcb, vcb, ksb, vsb, kwb, vwb, gb, zb,
     qc, kc, vc, zc, mg) = jnp.split(w_in_i, points, axis=-1)
    D = w_in_i.shape[0]

    def dup(w):
        return jnp.concatenate([w[:, :HEAD_DIM], w[:, :HEAD_DIM], w[:, HEAD_DIM:], w[:, HEAD_DIM:]], axis=1)

    qs = SCALE * LOG2E
    w1 = jnp.concatenate([qa * qs, ka, va, qb * SCALE, vc, dup(vsb), dup(vwb)], axis=1)
    w2 = jnp.concatenate([qb * qs, qc * qs, kc, dup(kwb), dup(ksb)], axis=1)
    pad = jnp.zeros((D, LANES - H_A - 3 * H_B), w_in_i.dtype)
    w3 = jnp.concatenate([mg, za, zb, zc, fa, gb, pad, kcb, vcb], axis=1)
    return w1.astype(BF16), w2.astype(BF16), w3.astype(BF16)


O1_QA, O1_KA, O1_VA, O1_QBU, O1_VC, O1_VS, O1_VW = 0, 4, 8, 12, 16, 20, 22
O2_QBR, O2_QC, O2_KC, O2_KW, O2_KS = 0, 4, 8, 12, 14
O3_FG = 36
O3_KCB = 37 * LANES


def kernel(x, p, positions, norm_g, w_in, b_forget, cmp_pe_k, cmp_w1_k, cmp_b1_k, cmp_w2_k,
           cmp_pe_v, cmp_w1_v, cmp_b1_v, cmp_w2_v, diff_lam, diff_subln_g,
           w_br_a, w_br_b, w_br_c, w_out, w_ple, w_ple_gate, final_g):
    B, S, D = x.shape
    depth = w_in.shape[0]
    T = B * S
    R = S // CMP_STRIDE
    n_sel = S // L_SEL
    assert n_sel <= HEAD_DIM and S % min(TQ, S) == 0

    xf = x.reshape(T, D)
    cos_t, sin_t = _rope_tables(positions.astype(F32).reshape(T))
    e_mat = _gate_expand_matrix()
    cover = _cover_matrix(R, n_sel)
    h = _rmsnorm(xf, norm_g[0], BF16)

    for i in range(depth):
        w1, w2, w3 = _layer_weights(w_in[i])
        o1 = _proj(h, w1, BF16, 1024, name="proj_plain")
        o2 = _proj(h, w2, BF16, 1024, rope_tabs=(cos_t, sin_t), name="proj_rope")
        o3 = _proj(h, w3, F32, 1664, name="proj_f32")

        b_pad = jnp.zeros((1, LANES), F32).at[0, :H_A].set(b_forget[i])
        c = _forget_cumsum(o3, O3_FG, b_pad, S)
        ck = c[:, :H_A].reshape(B, S, H_A // 2, 2).transpose(0, 2, 3, 1)
        att_a = _flash("fox", B, S, H_A // 2, o1, O1_QA, o1, lambda h_: O1_KA + h_,
                       o1, lambda h_: O1_VA + h_, ck=ck)

        lam_init = 0.8 - 0.6 * math.exp(-0.3 * i)
        att_c = _flash("diff", B, S, H_C, o2, O2_QC, o2, lambda h_: O2_KC + h_,
                       o1, lambda h_: O1_VC + h_, lam=diff_lam[i],
                       subln_g=diff_subln_g[i].reshape(1, LANES), lam_init=lam_init)

        kv = o3[:, O3_KCB:O3_KCB + 2 * LANES].reshape(B, S, 2, G_B, HEAD_DIM)
        tkv = kv.transpose(2, 0, 3, 1, 4).reshape(2, B, G_B, R, CMP_STRIDE * HEAD_DIM)
        pe = jnp.stack([cmp_pe_k[i].reshape(1, -1), cmp_pe_v[i].reshape(1, -1)])
        cw1 = jnp.stack([cmp_w1_k[i], cmp_w1_v[i]]).astype(BF16)
        cb1 = jnp.stack([cmp_b1_k[i].reshape(1, -1), cmp_b1_v[i].reshape(1, -1)])
        cw2 = jnp.stack([cmp_w2_k[i], cmp_w2_v[i]])
        cw2 = jnp.concatenate([cw2, cw2], axis=-1).astype(BF16)
        ckv = _compress(tkv, pe, cw1, cb1, cw2)
        o_cmp, mm = _cmp_topk(B, S, o1, O1_QBU, ckv, cover)
        o_sel = _flash("sel", B, S, H_B // 2, o2, O2_QBR, o2, lambda h_: O2_KS + h_ // 2,
                       o1, lambda h_: O1_VS + h_ // 2, mm=mm)
        o_win = _flash("win", B, S, H_B // 2, o2, O2_QBR, o2, lambda h_: O2_KW + h_ // 2,
                       o1, lambda h_: O1_VW + h_ // 2)

        last = i == depth - 1
        g_next = (final_g if last else norm_g[i + 1]).reshape(1, D)
        res = _post(att_a, o_cmp, o_sel, o_win, att_c, o3, xf, p[i].reshape(T, -1),
                    w_br_a[i].astype(BF16), w_br_b[i].astype(BF16), w_br_c[i].astype(BF16),
                    w_out[i].astype(BF16), w_ple_gate[i].astype(BF16), w_ple[i].astype(BF16),
                    e_mat, g_next, last)
        if last:
            return res[0].reshape(B, S, D)
        xf, h = res
```

```python
import functools
import math

import numpy as np
import jax
import jax.numpy as jnp
from jax import lax
from jax.experimental import pallas as pl
from jax.experimental.pallas import tpu as pltpu

F32 = jnp.float32
BF16 = jnp.bfloat16

LANES = 128
HEAD_DIM = 64
NEG = -1e30
BIG = 1e30
LOWEST = -3e38
EPS = 1e-6
ROPE_THETA = 10000.0
H_A = 8
H_B = 8
G_B = 2
H_C = 4
L_CMP = 32
CMP_STRIDE = 16
CMP_HIDDEN = 256
L_SEL = 64
TOP_N = 16
WINDOW = 512
N_BRANCH = 3
D_MODEL = 1024
W_A = H_A * HEAD_DIM
W_B = H_B * HEAD_DIM
W_C = H_C * 2 * HEAD_DIM
KV_B = G_B * HEAD_DIM
SPLIT_SIZES = (W_A, W_A, W_A, H_A, W_A,
               W_B, KV_B, KV_B, KV_B, KV_B, KV_B, KV_B, 3 * H_B, W_B,
               2 * H_C * HEAD_DIM, 2 * H_C * HEAD_DIM, W_C, W_C,
               N_BRANCH * D_MODEL)
SCALE = HEAD_DIM ** -0.5
LOG2E = math.log2(math.e)

VMEM_LIMIT = 48 * 1024 * 1024

TQ = 512
TK = 1024
FLASH_ROWS = 128
PROJ_TM = 1024
POST_TM = 256


def _cparams(sem):
    return pltpu.CompilerParams(dimension_semantics=sem, vmem_limit_bytes=VMEM_LIMIT)


def _dot(a, b):
    return jnp.dot(a, b, preferred_element_type=F32)


def _dot_nt(a, b):
    return lax.dot_general(a, b, (((1,), (1,)), ((), ())), preferred_element_type=F32)


def _split3(x):
    hi = x.astype(BF16)
    r1 = x - hi.astype(F32)
    mid = r1.astype(BF16)
    lo = (r1 - mid.astype(F32)).astype(BF16)
    return hi, mid, lo


def _dot_exact01(x, m01):
    hi, mid, lo = _split3(x)
    return _dot(hi, m01) + _dot(mid, m01) + _dot(lo, m01)


def _rmsnorm_kernel(x_ref, g_ref, o_ref):
    x = x_ref[...]
    y = x * lax.rsqrt(jnp.mean(x * x, axis=-1, keepdims=True) + EPS)
    o_ref[...] = (y * g_ref[...]).astype(o_ref.dtype)


def _rmsnorm(x, g, out_dtype, tm=512):
    T, D = x.shape
    return pl.pallas_call(
        _rmsnorm_kernel,
        grid=(T // tm,),
        in_specs=[pl.BlockSpec((tm, D), lambda i: (i, 0)),
                  pl.BlockSpec((1, D), lambda i: (0, 0))],
        out_specs=pl.BlockSpec((tm, D), lambda i: (i, 0)),
        out_shape=jax.ShapeDtypeStruct((T, D), out_dtype),
        compiler_params=_cparams(("parallel",)),
        name="rmsnorm",
    )(x, g.reshape(1, D))


def _rope_table_kernel(pos_ref, invf_ref, sign_ref, cos_ref, sin_ref):
    ang = pos_ref[...] * invf_ref[...]
    cos_ref[...] = jnp.cos(ang)
    sin_ref[...] = jnp.sin(ang) * sign_ref[...]


def _rope_tables(pos_f32, tm=512):
    T = pos_f32.shape[0]
    half = HEAD_DIM // 2
    inv_freq = ROPE_THETA ** (-jnp.arange(half, dtype=F32) / half)
    invf = jnp.tile(inv_freq, LANES // half).reshape(1, LANES)
    sign = jnp.tile(jnp.concatenate([-jnp.ones((half,), F32), jnp.ones((half,), F32)]),
                    LANES // HEAD_DIM).reshape(1, LANES)
    return pl.pallas_call(
        _rope_table_kernel,
        grid=(T // tm,),
        in_specs=[pl.BlockSpec((tm, 1), lambda i: (i, 0)),
                  pl.BlockSpec((1, LANES), lambda i: (0, 0)),
                  pl.BlockSpec((1, LANES), lambda i: (0, 0))],
        out_specs=[pl.BlockSpec((tm, LANES), lambda i: (i, 0)),
                   pl.BlockSpec((tm, LANES), lambda i: (i, 0))],
        out_shape=[jax.ShapeDtypeStruct((T, LANES), F32),
                   jax.ShapeDtypeStruct((T, LANES), F32)],
        compiler_params=_cparams(("parallel",)),
        name="rope_table",
    )(pos_f32.reshape(T, 1), invf, sign)


def _proj_kernel(h_ref, w_ref, *rest, rope):
    acc = _dot(h_ref[...], w_ref[...])
    if rope:
        cos_ref, sin_ref, o_ref = rest
        cos = cos_ref[...]
        sin = sin_ref[...]
        lane = lax.broadcasted_iota(jnp.int32, cos.shape, 1)
        first = (lane & (HEAD_DIM - 1)) < (HEAD_DIM // 2)
        for c in range(acc.shape[1] // LANES):
            t = acc[:, c * LANES:(c + 1) * LANES]
            sw = jnp.where(first, pltpu.roll(t, LANES - HEAD_DIM // 2, 1),
                           pltpu.roll(t, HEAD_DIM // 2, 1))
            o_ref[:, c * LANES:(c + 1) * LANES] = (t * cos + sw * sin).astype(o_ref.dtype)
    else:
        (o_ref,) = rest
        o_ref[...] = acc.astype(o_ref.dtype)


def _proj(h, w, out_dtype, tn, rope_tabs=None, name="proj"):
    T, D = h.shape
    N = w.shape[1]
    tm = min(PROJ_TM, T)
    assert T % tm == 0 and N % tn == 0
    in_specs = [pl.BlockSpec((tm, D), lambda i, j: (i, 0)),
                pl.BlockSpec((D, tn), lambda i, j: (0, j))]
    args = [h, w]
    if rope_tabs is not None:
        in_specs += [pl.BlockSpec((tm, LANES), lambda i, j: (i, 0)),
                     pl.BlockSpec((tm, LANES), lambda i, j: (i, 0))]
        args += list(rope_tabs)
    return pl.pallas_call(
        functools.partial(_proj_kernel, rope=rope_tabs is not None),
        grid=(T // tm, N // tn),
        in_specs=in_specs,
        out_specs=pl.BlockSpec((tm, tn), lambda i, j: (i, j)),
        out_shape=jax.ShapeDtypeStruct((T, N), out_dtype),
        compiler_params=_cparams(("parallel", "arbitrary")),
        name=name,
    )(*args)


def _cumsum_kernel(fg_ref, b_ref, c_ref, carry_sc, *, tiles_per_seq):
    i = pl.program_id(0)

    @pl.when(i % tiles_per_seq == 0)
    def _():
        carry_sc[...] = jnp.zeros_like(carry_sc)

    z = fg_ref[...] + b_ref[...]
    logf = jnp.minimum(z, 0.0) - jnp.log1p(jnp.exp(-jnp.abs(z)))
    tm = z.shape[0]
    r = lax.broadcasted_iota(jnp.int32, (tm, tm), 0)
    c = lax.broadcasted_iota(jnp.int32, (tm, tm), 1)
    tri = jnp.where(c <= r, 1.0, 0.0).astype(BF16)
    cs = _dot_exact01_left(tri, logf) + carry_sc[...]
    c_ref[...] = cs * LOG2E
    carry_sc[...] = cs[tm - 1:tm, :]


def _dot_exact01_left(m01, x):
    hi, mid, lo = _split3(x)
    return _dot(m01, hi) + _dot(m01, mid) + _dot(m01, lo)


def _forget_cumsum(o3, fg_blk, b_pad, S, tm=256):
    T = o3.shape[0]
    return pl.pallas_call(
        functools.partial(_cumsum_kernel, tiles_per_seq=S // tm),
        grid=(T // tm,),
        in_specs=[pl.BlockSpec((tm, LANES), lambda i: (i, fg_blk)),
                  pl.BlockSpec((1, LANES), lambda i: (0, 0))],
        out_specs=pl.BlockSpec((tm, LANES), lambda i: (i, 0)),
        out_shape=jax.ShapeDtypeStruct((T, LANES), F32),
        scratch_shapes=[pltpu.VMEM((1, LANES), F32)],
        compiler_params=_cparams(("arbitrary",)),
        name="forget_cumsum",
    )(o3, b_pad)


def _flash_kernel(*refs, mode, tq, tk, window, lam_init):
    it = iter(refs)
    q_ref = next(it)
    mm_ref = next(it) if mode == "sel" else None
    k_ref = next(it)
    v_ref = next(it)
    if mode == "fox":
        ck_ref = next(it)
    if mode == "diff":
        lam_ref = next(it)
        g_ref = next(it)
    o_ref = next(it)
    m_sc = next(it)
    acc_sc = next(it)
    v1_sc = next(it)
    s0_sc = next(it)
    s1_sc = next(it)
    if mode == "sel":
        ke_sc = next(it)

    qi = pl.program_id(2)
    S = k_ref.shape[0]
    lane = lax.broadcasted_iota(jnp.int32, (tq, LANES), 1)
    half0 = lane < HEAD_DIM

    @pl.when(qi == 0)
    def _():
        v1_sc[:, :LANES] = v_ref[...]
        v1_sc[:, LANES:] = jnp.ones((S, LANES), BF16)
        if mode == "sel":
            row = lax.broadcasted_iota(jnp.int32, (S, LANES), 0)
            ln = lax.broadcasted_iota(jnp.int32, (S, LANES), 1)
            blk = lax.shift_right_logical(row, int(math.log2(L_SEL)))
            ke_sc[:, :LANES] = k_ref[...]
            ke_sc[:, LANES:] = jnp.where(ln == blk, BIG, 0.0).astype(BF16)

    kk = ke_sc if mode == "sel" else k_ref

    q2 = q_ref[...].astype(F32)
    zero = jnp.zeros_like(q2)
    qa = jnp.where(half0, q2, zero).astype(BF16)
    qb = jnp.where(half0, zero, q2).astype(BF16)
    if mode == "sel":
        mmh = jnp.where(half0, mm_ref[...].astype(F32), zero).astype(BF16)
        qa = jnp.concatenate([qa, mmh], axis=1)
        qb = jnp.concatenate([qb, mmh], axis=1)
    qm = (qa, qb)

    m_sc[...] = jnp.full_like(m_sc, NEG)
    acc_sc[...] = jnp.zeros_like(acc_sc)
    q0 = qi * tq
    rb = LANES
    nrb = tq // rb
    dsq = (lax.broadcasted_iota(jnp.int32, (rb, LANES), 0)
           - lax.broadcasted_iota(jnp.int32, (rb, LANES), 1))

    nch = tq // LANES

    def qk(start, s_sc):
        koff = pl.multiple_of(start, tq)
        kt = kk[pl.ds(koff, tq), :]
        for a in range(2):
            s = _dot_nt(qm[a], kt)
            if mode == "fox":
                s = s - ck_ref[0, 0, a:a + 1, pl.ds(koff, tq)]
            s_sc[a] = s

    def soft_pv(s_sc, start, kind):
        koff = pl.multiple_of(start, tq)
        vt = v1_sc[pl.ds(koff, tq), :]
        for a in range(2):
            for rbi in range(nrb):
                rows = slice(rbi * rb, (rbi + 1) * rb)
                if kind == "full":
                    c_lo, c_hi, c_edge = 0, nch, None
                elif kind == "diag":
                    c_lo, c_hi, c_edge = 0, rbi + 1, rbi
                else:
                    c_lo, c_hi, c_edge = rbi, nch, rbi
                ch = [s_sc[a, rows, c * LANES:(c + 1) * LANES] for c in range(c_lo, c_hi)]
                if c_edge is not None:
                    ok = dsq >= 0 if kind == "diag" else dsq < 0
                    ch[c_edge - c_lo] = jnp.where(ok, ch[c_edge - c_lo], NEG)
                mx = functools.reduce(jnp.maximum, ch)
                m_old = m_sc[a, rows]
                m_new = jnp.maximum(m_old, jnp.max(mx, axis=1, keepdims=True))
                alpha = jnp.exp2(m_old - m_new)
                p = jnp.concatenate([jnp.exp2(c_ - m_new).astype(BF16) for c_ in ch], axis=1)
                m_sc[a, rows] = m_new
                al = jnp.concatenate([alpha, alpha], axis=1)
                acc_sc[a, rows] = (al * acc_sc[a, rows]
                                   + _dot(p, vt[c_lo * LANES:c_hi * LANES]))

    if mode == "win":
        @pl.when(qi > 0)
        def _():
            qk(q0 - window, s0_sc)
            qk(q0, s1_sc)
            soft_pv(s0_sc, q0 - window, "prev")
            soft_pv(s1_sc, q0, "diag")

        @pl.when(qi == 0)
        def _():
            qk(q0, s1_sc)
            soft_pv(s1_sc, q0, "diag")
    else:
        qk(0, s0_sc)

        def pair_body(i, carry):
            t0 = 2 * i * tq
            qk(t0 + tq, s1_sc)
            soft_pv(s0_sc, t0, "full")
            qk(t0 + 2 * tq, s0_sc)
            soft_pv(s1_sc, t0 + tq, "full")
            return carry

        lax.fori_loop(0, qi // 2, pair_body, 0)

        @pl.when(qi % 2 == 1)
        def _():
            qk(q0, s1_sc)
            soft_pv(s0_sc, q0 - tq, "full")
            soft_pv(s1_sc, q0, "diag")

        @pl.when(qi % 2 == 0)
        def _():
            soft_pv(s0_sc, q0, "diag")

    o0 = acc_sc[0, :, :LANES] / acc_sc[0, :, LANES:]
    o1 = acc_sc[1, :, :LANES] / acc_sc[1, :, LANES:]
    if mode == "diff":
        lq = lam_ref[...]
        lam = (jnp.exp(jnp.sum(lq[0:1] * lq[1:2], keepdims=True))
               - jnp.exp(jnp.sum(lq[2:3] * lq[3:4], keepdims=True)) + lam_init)
        d = o0 - lam * o1
        y = d * lax.rsqrt(jnp.mean(d * d, axis=-1, keepdims=True) + EPS)
        o_ref[...] = (y * g_ref[...]) * (1.0 - lam_init)
    else:
        o_ref[...] = jnp.where(half0, o0, o1)


def _flash(mode, B, S, n_blk, q, q_blk0, k, k_blk, v, v_blk, *,
           mm=None, ck=None, lam=None, subln_g=None, lam_init=0.0):
    tq = min(TQ, S)
    tk = min(TK, S)
    nq = S // tq
    T = B * S
    assert S % tk == 0 and tk == 2 * tq and WINDOW == tq and tq % LANES == 0
    in_specs = [pl.BlockSpec((tq, LANES), lambda b, h, i: (b * nq + i, q_blk0 + h))]
    args = [q]
    if mode == "sel":
        in_specs.append(pl.BlockSpec((tq, LANES), lambda b, h, i: (b * nq + i, h // 2)))
        args.append(mm)
    in_specs.append(pl.BlockSpec((S, LANES), lambda b, h, i: (b, k_blk(h))))
    args.append(k)
    in_specs.append(pl.BlockSpec((S, LANES), lambda b, h, i: (b, v_blk(h))))
    args.append(v)
    if mode == "fox":
        in_specs.append(pl.BlockSpec((1, 1, 2, S), lambda b, h, i: (b, h, 0, 0)))
        args += [ck]
    if mode == "diff":
        in_specs.append(pl.BlockSpec(lam.shape, lambda b, h, i: (0, 0)))
        in_specs.append(pl.BlockSpec((1, LANES), lambda b, h, i: (0, 0)))
        args += [lam, subln_g]
    scratch = [pltpu.VMEM((2, tq, LANES), F32), pltpu.VMEM((2, tq, 2 * LANES), F32),
               pltpu.VMEM((S, 2 * LANES), BF16),
               pltpu.VMEM((2, tq, tq), F32), pltpu.VMEM((2, tq, tq), F32)]
    if mode == "sel":
        scratch += [pltpu.VMEM((S, 2 * LANES), BF16)]
    return pl.pallas_call(
        functools.partial(_flash_kernel, mode=mode, tq=tq, tk=tk, window=WINDOW, lam_init=lam_init),
        grid=(B, n_blk, nq),
        in_specs=in_specs,
        out_specs=pl.BlockSpec((tq, LANES), lambda b, h, i: (b * nq + i, h)),
        out_shape=jax.ShapeDtypeStruct((T, n_blk * LANES), F32),
        scratch_shapes=scratch,
        compiler_params=_cparams(("parallel", "parallel", "arbitrary")),
        name="flash_" + mode,
    )(*args)


def _compress_kernel(t_ref, pe_ref, w1_ref, b1_ref, w2_ref, o_ref):
    t = t_ref[0, 0, 0]
    R, half = t.shape
    pe = pe_ref[0]
    xa = (t + pe[:, :half]).astype(BF16)
    xb = (t + pe[:, half:]).astype(BF16)
    a = _dot(xa, w1_ref[0, :half, :])
    b = _dot(xb, w1_ref[0, half:, :])
    hp = a + pltpu.roll(b, R - 1, 0) + b1_ref[0]
    hid = hp * jax.nn.sigmoid(hp)
    o_ref[0, 0] = _dot(hid.astype(BF16), w2_ref[0]).astype(o_ref.dtype)


def _compress(tkv, pe, w1, b1, w2dup):
    _, B, G, R, W = tkv.shape
    return pl.pallas_call(
        _compress_kernel,
        grid=(2, B, G),
        in_specs=[pl.BlockSpec((1, 1, 1, R, W), lambda s, b, g: (s, b, g, 0, 0)),
                  pl.BlockSpec((1, 1, 2 * W), lambda s, b, g: (s, 0, 0)),
                  pl.BlockSpec((1, 2 * W, CMP_HIDDEN), lambda s, b, g: (s, 0, 0)),
                  pl.BlockSpec((1, 1, CMP_HIDDEN), lambda s, b, g: (s, 0, 0)),
                  pl.BlockSpec((1, CMP_HIDDEN, LANES), lambda s, b, g: (s, 0, 0))],
        out_specs=pl.BlockSpec((1, 1, R, LANES), lambda s, b, g: (s, b, 0, g)),
        out_shape=jax.ShapeDtypeStruct((2, B, R, G * LANES), BF16),
        compiler_params=_cparams(("parallel", "parallel", "parallel")),
        name="nsa_compress",
    )(tkv, pe, w1, b1, w2dup)


def _cmp_kernel(q_ref, kc_ref, vc_ref, cov_ref, o_ref, mm_ref, *, tq, n_sel, top_n):
    qi = pl.program_id(2)
    kc = kc_ref[0, 0]
    vc = vc_ref[0, 0]
    R = kc.shape[0]
    lane = lax.broadcasted_iota(jnp.int32, (tq, LANES), 1)
    half0 = lane < HEAD_DIM
    t_r = lax.broadcasted_iota(jnp.int32, (tq, R), 0) + qi * tq
    c_end = lax.broadcasted_iota(jnp.int32, (tq, R), 1) * CMP_STRIDE + (L_CMP - 1)
    ok = c_end <= t_r
    psum = jnp.zeros((tq, R), F32)
    outs = []
    for pr in range(2):
        q2 = q_ref[:, pr * LANES:(pr + 1) * LANES].astype(F32)
        zero = jnp.zeros_like(q2)
        pair = []
        for a in range(2):
            qm = (jnp.where(half0, q2, zero) if a == 0 else jnp.where(half0, zero, q2)).astype(BF16)
            s = jnp.where(ok, _dot_nt(qm, kc), NEG)
            m = jnp.max(s, axis=1, keepdims=True)
            e = jnp.where(ok, jnp.exp(s - m), 0.0)
            l = jnp.sum(e, axis=1, keepdims=True)
            p = e / jnp.where(l > 0.0, l, 1.0)
            psum = psum + p
            pair.append(_dot(p.astype(BF16), vc))
        outs.append(jnp.where(half0, pair[0], pair[1]))
    o_ref[...] = jnp.concatenate(outs, axis=1)

    imp = _dot_exact01(psum, cov_ref[...])
    jl = lane & (HEAD_DIM - 1)
    t_q = lax.broadcasted_iota(jnp.int32, (tq, LANES), 0) + qi * tq
    tb = lax.shift_right_logical(t_q, int(math.log2(L_SEL)))
    valid = jl <= tb
    forced = (jl == 0) | (jl == tb) | (jl == tb - 1)
    score = jnp.where(forced, BIG, jnp.where(valid, imp, -BIG))
    score = jnp.where(half0 & (jl < n_sel), score, LOWEST)
    mm = jnp.full((tq, LANES), -1.0, F32)
    for _ in range(top_n):
        idx = jnp.argmax(score, axis=1, keepdims=True).astype(jnp.int32)
        mm = jnp.where(jl == idx, 0.0, mm)
        score = jnp.where(lane == idx, LOWEST, score)
    mm_ref[...] = mm.astype(mm_ref.dtype)


def _cmp_topk(B, S, q, q_blk0, ckv, cover_dup):
    tq = min(TQ, S)
    nq = S // tq
    T = B * S
    R = ckv.shape[2]
    n_sel = S // L_SEL
    gw = 2 * LANES
    return pl.pallas_call(
        functools.partial(_cmp_kernel, tq=tq, n_sel=n_sel, top_n=min(TOP_N, n_sel)),
        grid=(B, G_B, nq),
        in_specs=[pl.BlockSpec((tq, gw), lambda b, g, i: (b * nq + i, q_blk0 // 2 + g)),
                  pl.BlockSpec((1, 1, R, LANES), lambda b, g, i: (0, b, 0, g)),
                  pl.BlockSpec((1, 1, R, LANES), lambda b, g, i: (1, b, 0, g)),
                  pl.BlockSpec((R, LANES), lambda b, g, i: (0, 0))],
        out_specs=[pl.BlockSpec((tq, gw), lambda b, g, i: (b * nq + i, g)),
                   pl.BlockSpec((tq, LANES), lambda b, g, i: (b * nq + i, g))],
        out_shape=[jax.ShapeDtypeStruct((T, G_B * gw), F32),
                   jax.ShapeDtypeStruct((T, G_B * LANES), BF16)],
        compiler_params=_cparams(("parallel", "parallel", "parallel")),
        name="nsa_cmp_topk",
    )(q, ckv, ckv, cover_dup)


def _post_kernel(aa_ref, oc_ref, os_ref, ow_ref, ac_ref, fg_ref, h_ref, x_ref, p_ref,
                 wz_ref, wa_ref, wb_ref, wc_ref, wo_ref, wg_ref, wp_ref, e_ref, gn_ref,
                 *outs, last):
    def silu(z):
        return z * jax.nn.sigmoid(z)

    W = W_B
    D = x_ref.shape[1]
    mz = _dot(h_ref[...], wz_ref[...])
    za = mz[:, 3 * D:3 * D + W]
    zb = mz[:, 3 * D + W:3 * D + 2 * W]
    zc = mz[:, 3 * D + 2 * W:]
    ge = _dot_exact01(jax.nn.sigmoid(fg_ref[...]), e_ref[...])
    ya = aa_ref[...] * silu(za)
    yb = (ge[:, :W] * oc_ref[...] + ge[:, W:2 * W] * os_ref[...]
          + ge[:, 2 * W:] * ow_ref[...]) * silu(zb)
    yc = ac_ref[...] * silu(zc)
    pa = _dot(ya.astype(BF16), wa_ref[...])
    pb = _dot(yb.astype(BF16), wb_ref[...])
    pc = _dot(yc.astype(BF16), wc_ref[...])
    merged = (jax.nn.sigmoid(mz[:, :D]) * pa + jax.nn.sigmoid(mz[:, D:2 * D]) * pb
              + jax.nn.sigmoid(mz[:, 2 * D:3 * D]) * pc)
    x1 = x_ref[...] + _dot(merged.astype(BF16), wo_ref[...])
    gate = jax.nn.sigmoid(_dot(x1.astype(BF16), wg_ref[...]))
    x2 = x1 + gate * _dot(p_ref[...].astype(BF16), wp_ref[...])
    y = x2 * lax.rsqrt(jnp.mean(x2 * x2, axis=-1, keepdims=True) + EPS) * gn_ref[...]
    if last:
        outs[0][...] = y
    else:
        outs[0][...] = x2
        outs[1][...] = y.astype(BF16)


def _post(aa, oc, osel, ow, ac, o3, h, x, p, wz, wa, wb, wc, wo, wg, wp, e_mat, g_next, last):
    T, D = x.shape
    tm = min(POST_TM, T)
    W = W_A
    row = lambda i: (i, 0)
    const = lambda i: (0, 0)

    def resident(arr):
        return pl.BlockSpec(arr.shape, const, pipeline_mode=pl.Buffered(1))

    in_specs = [pl.BlockSpec((tm, W), row)] * 5 + [
        pl.BlockSpec((tm, LANES), lambda i: (i, O3_FG)),
        pl.BlockSpec((tm, D), row),
        pl.BlockSpec((tm, D), row),
        pl.BlockSpec((tm, p.shape[1]), row),
        resident(wz), resident(wa), resident(wb), resident(wc), resident(wo), resident(wg),
        resident(wp), resident(e_mat), pl.BlockSpec((1, D), const)]
    if last:
        out_specs = [pl.BlockSpec((tm, D), row)]
        out_shape = [jax.ShapeDtypeStruct((T, D), F32)]
    else:
        out_specs = [pl.BlockSpec((tm, D), row), pl.BlockSpec((tm, D), row)]
        out_shape = [jax.ShapeDtypeStruct((T, D), F32), jax.ShapeDtypeStruct((T, D), BF16)]
    return pl.pallas_call(
        functools.partial(_post_kernel, last=last),
        grid=(T // tm,),
        in_specs=in_specs,
        out_specs=out_specs,
        out_shape=out_shape,
        compiler_params=_cparams(("parallel",)),
        name="post",
    )(aa, oc, osel, ow, ac, o3, h, x, p, wz, wa, wb, wc, wo, wg, wp, e_mat, g_next)


def _gate_expand_matrix():
    e = np.zeros((LANES, N_BRANCH * W_B), np.float32)
    for h in range(H_B):
        for r in range(N_BRANCH):
            e[H_A + h * N_BRANCH + r, r * W_B + h * HEAD_DIM:r * W_B + (h + 1) * HEAD_DIM] = 1.0
    return jnp.asarray(e, BF16)


def _cover_matrix(R, n_sel):
    c_start = np.arange(R)[:, None] * CMP_STRIDE
    j_start = np.arange(HEAD_DIM)[None, :] * L_SEL
    cov = ((c_start < j_start + L_SEL) & (c_start + L_CMP > j_start)
           & (np.arange(HEAD_DIM)[None, :] < n_sel)).astype(np.float32)
    return jnp.asarray(np.concatenate([cov, cov], axis=1), BF16)


def _layer_weights(w_in_i):
    points = np.cumsum(np.array(SPLIT_SIZES))[:-1].tolist()
    (qa, ka, va, fa, za, qb, kcb, vcb, ksb, vsb, kwb, vwb, gb, zb,
     qc, kc, vc, zc, mg) = jnp.split(w_in_i, points, axis=-1)
    D = w_in_i.shape[0]

    def dup(w):
        return jnp.concatenate([w[:, :HEAD_DIM], w[:, :HEAD_DIM], w[:, HEAD_DIM:], w[:, HEAD_DIM:]], axis=1)

    qs = SCALE * LOG2E
    w1 = jnp.concatenate([qa * qs, ka, va, qb * SCALE, vc, dup(vsb), dup(vwb)], axis=1)
    w2 = jnp.concatenate([qb * qs, qc * qs, kc, dup(kwb), dup(ksb)], axis=1)
    pad = jnp.zeros((D, LANES - H_A - 3 * H_B), w_in_i.dtype)
    w3 = jnp.concatenate([fa, gb, pad, kcb, vcb], axis=1)
    wz = jnp.concatenate([mg, za, zb, zc], axis=1)
    return w1.astype(BF16), w2.astype(BF16), w3.astype(BF16), wz.astype(BF16)


O1_QA, O1_KA, O1_VA, O1_QBU, O1_VC, O1_VS, O1_VW = 0, 4, 8, 12, 16, 20, 22
O2_QBR, O2_QC, O2_KC, O2_KW, O2_KS = 0, 4, 8, 12, 14
O3_FG = 0
O3_KCB = LANES


def kernel(x, p, positions, norm_g, w_in, b_forget, cmp_pe_k, cmp_w1_k, cmp_b1_k, cmp_w2_k,
           cmp_pe_v, cmp_w1_v, cmp_b1_v, cmp_w2_v, diff_lam, diff_subln_g,
           w_br_a, w_br_b, w_br_c, w_out, w_ple, w_ple_gate, final_g):
    B, S, D = x.shape
    depth = w_in.shape[0]
    T = B * S
    R = S // CMP_STRIDE
    n_sel = S // L_SEL
    assert n_sel <= HEAD_DIM and S % min(TQ, S) == 0

    xf = x.reshape(T, D)
    cos_t, sin_t = _rope_tables(positions.astype(F32).reshape(T))
    e_mat = _gate_expand_matrix()
    cover = _cover_matrix(R, n_sel)
    h = _rmsnorm(xf, norm_g[0], BF16)

    for i in range(depth):
        w1, w2, w3, wz = _layer_weights(w_in[i])
        o1 = _proj(h, w1, BF16, 1024, name="proj_plain")
        o2 = _proj(h, w2, BF16, 1024, rope_tabs=(cos_t, sin_t), name="proj_rope")
        o3 = _proj(h, w3, F32, 3 * LANES, name="proj_f32")

        b_pad = jnp.zeros((1, LANES), F32).at[0, :H_A].set(b_forget[i])
        c = _forget_cumsum(o3, O3_FG, b_pad, S)
        ck = c[:, :H_A].reshape(B, S, H_A // 2, 2).transpose(0, 2, 3, 1)
        att_a = _flash("fox", B, S, H_A // 2, o1, O1_QA, o1, lambda h_: O1_KA + h_,
                       o1, lambda h_: O1_VA + h_, ck=ck)

        lam_init = 0.8 - 0.6 * math.exp(-0.3 * i)
        att_c = _flash("diff", B, S, H_C, o2, O2_QC, o2, lambda h_: O2_KC + h_,
                       o1, lambda h_: O1_VC + h_, lam=diff_lam[i],
                       subln_g=diff_subln_g[i].reshape(1, LANES), lam_init=lam_init)

        kv = o3[:, O3_KCB:O3_KCB + 2 * LANES].reshape(B, S, 2, G_B, HEAD_DIM)
        tkv = kv.transpose(2, 0, 3, 1, 4).reshape(2, B, G_B, R, CMP_STRIDE * HEAD_DIM)
        pe = jnp.stack([cmp_pe_k[i].reshape(1, -1), cmp_pe_v[i].reshape(1, -1)])
        cw1 = jnp.stack([cmp_w1_k[i], cmp_w1_v[i]]).astype(BF16)
        cb1 = jnp.stack([cmp_b1_k[i].reshape(1, -1), cmp_b1_v[i].reshape(1, -1)])
        cw2 = jnp.stack([cmp_w2_k[i], cmp_w2_v[i]])
        cw2 = jnp.concatenate([cw2, cw2], axis=-1).astype(BF16)
        ckv = _compress(tkv, pe, cw1, cb1, cw2)
        o_cmp, mm = _cmp_topk(B, S, o1, O1_QBU, ckv, cover)
        o_sel = _flash("sel", B, S, H_B // 2, o2, O2_QBR, o2, lambda h_: O2_KS + h_ // 2,
                       o1, lambda h_: O1_VS + h_ // 2, mm=mm)
        o_win = _flash("win", B, S, H_B // 2, o2, O2_QBR, o2, lambda h_: O2_KW + h_ // 2,
                       o1, lambda h_: O1_VW + h_ // 2)

        last = i == depth - 1
        g_next = (final_g if last else norm_g[i + 1]).reshape(1, D)
        res = _post(att_a, o_cmp, o_sel, o_win, att_c, o3, h, xf, p[i].reshape(T, -1),
                    wz, w_br_a[i].astype(BF16), w_br_b[i].astype(BF16), w_br_c[i].astype(BF16),
                    w_out[i].astype(BF16), w_ple_gate[i].astype(BF16), w_ple[i].astype(BF16),
                    e_mat, g_next, last)
        if last:
            return res[0].reshape(B, S, D)
        xf, h = res
```

```python
import functools
import math

import numpy as np
import jax
import jax.numpy as jnp
from jax import lax
from jax.experimental import pallas as pl
from jax.experimental.pallas import tpu as pltpu

F32 = jnp.float32
BF16 = jnp.bfloat16

LANES = 128
HEAD_DIM = 64
NEG = -1e30
BIG = 1e30
LOWEST = -3e38
EPS = 1e-6
ROPE_THETA = 10000.0
H_A = 8
H_B = 8
G_B = 2
H_C = 4
L_CMP = 32
CMP_STRIDE = 16
CMP_HIDDEN = 256
L_SEL = 64
TOP_N = 16
WINDOW = 512
N_BRANCH = 3
D_MODEL = 1024
W_A = H_A * HEAD_DIM
W_B = H_B * HEAD_DIM
W_C = H_C * 2 * HEAD_DIM
KV_B = G_B * HEAD_DIM
SPLIT_SIZES = (W_A, W_A, W_A, H_A, W_A,
               W_B, KV_B, KV_B, KV_B, KV_B, KV_B, KV_B, 3 * H_B, W_B,
               2 * H_C * HEAD_DIM, 2 * H_C * HEAD_DIM, W_C, W_C,
               N_BRANCH * D_MODEL)
SCALE = HEAD_DIM ** -0.5
LOG2E = math.log2(math.e)

VMEM_LIMIT = 48 * 1024 * 1024

TQ = 512
TK = 1024
FLASH_ROWS = 256
PROJ_TM = 1024
POST_TM = 256


def _cparams(sem):
    return pltpu.CompilerParams(dimension_semantics=sem, vmem_limit_bytes=VMEM_LIMIT)


def _dot(a, b):
    return jnp.dot(a, b, preferred_element_type=F32)


def _dot_nt(a, b):
    return lax.dot_general(a, b, (((1,), (1,)), ((), ())), preferred_element_type=F32)


def _split3(x):
    hi = x.astype(BF16)
    r1 = x - hi.astype(F32)
    mid = r1.astype(BF16)
    lo = (r1 - mid.astype(F32)).astype(BF16)
    return hi, mid, lo


def _dot_exact01(x, m01):
    hi, mid, lo = _split3(x)
    return _dot(hi, m01) + _dot(mid, m01) + _dot(lo, m01)


def _rmsnorm_kernel(x_ref, g_ref, o_ref):
    x = x_ref[...]
    y = x * lax.rsqrt(jnp.mean(x * x, axis=-1, keepdims=True) + EPS)
    o_ref[...] = (y * g_ref[...]).astype(o_ref.dtype)


def _rmsnorm(x, g, out_dtype, tm=1024):
    T, D = x.shape
    return pl.pallas_call(
        _rmsnorm_kernel,
        grid=(T // tm,),
        in_specs=[pl.BlockSpec((tm, D), lambda i: (i, 0)),
                  pl.BlockSpec((1, D), lambda i: (0, 0))],
        out_specs=pl.BlockSpec((tm, D), lambda i: (i, 0)),
        out_shape=jax.ShapeDtypeStruct((T, D), out_dtype),
        compiler_params=_cparams(("parallel",)),
        name="rmsnorm",
    )(x, g.reshape(1, D))


def _rope_table_kernel(pos_ref, invf_ref, sign_ref, cos_ref, sin_ref):
    ang = pos_ref[...] * invf_ref[...]
    cos_ref[...] = jnp.cos(ang)
    sin_ref[...] = jnp.sin(ang) * sign_ref[...]


def _rope_tables(pos_f32, tm=512):
    T = pos_f32.shape[0]
    half = HEAD_DIM // 2
    inv_freq = ROPE_THETA ** (-jnp.arange(half, dtype=F32) / half)
    invf = jnp.tile(inv_freq, LANES // half).reshape(1, LANES)
    sign = jnp.tile(jnp.concatenate([-jnp.ones((half,), F32), jnp.ones((half,), F32)]),
                    LANES // HEAD_DIM).reshape(1, LANES)
    return pl.pallas_call(
        _rope_table_kernel,
        grid=(T // tm,),
        in_specs=[pl.BlockSpec((tm, 1), lambda i: (i, 0)),
                  pl.BlockSpec((1, LANES), lambda i: (0, 0)),
                  pl.BlockSpec((1, LANES), lambda i: (0, 0))],
        out_specs=[pl.BlockSpec((tm, LANES), lambda i: (i, 0)),
                   pl.BlockSpec((tm, LANES), lambda i: (i, 0))],
        out_shape=[jax.ShapeDtypeStruct((T, LANES), F32),
                   jax.ShapeDtypeStruct((T, LANES), F32)],
        compiler_params=_cparams(("parallel",)),
        name="rope_table",
    )(pos_f32.reshape(T, 1), invf, sign)


def _proj_kernel(h_ref, w_ref, *rest, rope):
    acc = _dot(h_ref[...], w_ref[...])
    if rope:
        cos_ref, sin_ref, o_ref = rest
        cos = cos_ref[...]
        sin = sin_ref[...]
        lane = lax.broadcasted_iota(jnp.int32, cos.shape, 1)
        first = (lane & (HEAD_DIM - 1)) < (HEAD_DIM // 2)
        for c in range(acc.shape[1] // LANES):
            t = acc[:, c * LANES:(c + 1) * LANES]
            sw = jnp.where(first, pltpu.roll(t, LANES - HEAD_DIM // 2, 1),
                           pltpu.roll(t, HEAD_DIM // 2, 1))
            o_ref[:, c * LANES:(c + 1) * LANES] = (t * cos + sw * sin).astype(o_ref.dtype)
    else:
        (o_ref,) = rest
        o_ref[...] = acc.astype(o_ref.dtype)


def _proj_f32_kernel(h_ref, w_ref, fg_ref, kv_ref):
    acc = _dot(h_ref[...], w_ref[...])
    fg_ref[...] = acc[:, :LANES]
    for c in range(2 * G_B):
        kv_ref[c] = acc[:, LANES + c * HEAD_DIM:LANES + (c + 1) * HEAD_DIM]


def _proj_f32(h, w):
    T, D = h.shape
    tm = min(PROJ_TM, T)
    return pl.pallas_call(
        _proj_f32_kernel,
        grid=(T // tm,),
        in_specs=[pl.BlockSpec((tm, D), lambda i: (i, 0)),
                  pl.BlockSpec(w.shape, lambda i: (0, 0))],
        out_specs=[pl.BlockSpec((tm, LANES), lambda i: (i, 0)),
                   pl.BlockSpec((2 * G_B, tm, HEAD_DIM), lambda i: (0, i, 0))],
        out_shape=[jax.ShapeDtypeStruct((T, LANES), F32),
                   jax.ShapeDtypeStruct((2 * G_B, T, HEAD_DIM), F32)],
        compiler_params=_cparams(("parallel",)),
        name="proj_f32",
    )(h, w)


def _proj(h, w, out_dtype, tn, rope_tabs=None, name="proj"):
    T, D = h.shape
    N = w.shape[1]
    tm = min(PROJ_TM, T)
    assert T % tm == 0 and N % tn == 0
    in_specs = [pl.BlockSpec((tm, D), lambda i, j: (i, 0)),
                pl.BlockSpec((D, tn), lambda i, j: (0, j))]
    args = [h, w]
    if rope_tabs is not None:
        in_specs += [pl.BlockSpec((tm, LANES), lambda i, j: (i, 0)),
                     pl.BlockSpec((tm, LANES), lambda i, j: (i, 0))]
        args += list(rope_tabs)
    return pl.pallas_call(
        functools.partial(_proj_kernel, rope=rope_tabs is not None),
        grid=(T // tm, N // tn),
        in_specs=in_specs,
        out_specs=pl.BlockSpec((tm, tn), lambda i, j: (i, j)),
        out_shape=jax.ShapeDtypeStruct((T, N), out_dtype),
        compiler_params=_cparams(("parallel", "arbitrary")),
        name=name,
    )(*args)


def _cumsum_kernel(fg_ref, b_ref, c_ref, carry_sc, *, tiles_per_seq):
    i = pl.program_id(0)

    @pl.when(i % tiles_per_seq == 0)
    def _():
        carry_sc[...] = jnp.zeros_like(carry_sc)

    z = fg_ref[...] + b_ref[...]
    logf = jnp.minimum(z, 0.0) - jnp.log1p(jnp.exp(-jnp.abs(z)))
    tm = z.shape[0]
    r = lax.broadcasted_iota(jnp.int32, (tm, tm), 0)
    c = lax.broadcasted_iota(jnp.int32, (tm, tm), 1)
    tri = jnp.where(c <= r, 1.0, 0.0).astype(BF16)
    cs = _dot_exact01_left(tri, logf) + carry_sc[...]
    carry_sc[...] = cs[tm - 1:tm, :]
    c_ref[0] = jnp.transpose(cs * LOG2E)[:H_A, :]


def _dot_exact01_left(m01, x):
    hi, mid, lo = _split3(x)
    return _dot(m01, hi) + _dot(m01, mid) + _dot(m01, lo)


def _forget_cumsum(fg, b_pad, B, S, tm=256):
    tps = S // tm
    return pl.pallas_call(
        functools.partial(_cumsum_kernel, tiles_per_seq=tps),
        grid=(B * tps,),
        in_specs=[pl.BlockSpec((tm, LANES), lambda i: (i, 0)),
                  pl.BlockSpec((1, LANES), lambda i: (0, 0))],
        out_specs=pl.BlockSpec((1, H_A, tm), lambda i: (i // tps, 0, i % tps)),
        out_shape=jax.ShapeDtypeStruct((B, H_A, S), F32),
        scratch_shapes=[pltpu.VMEM((1, LANES), F32)],
        compiler_params=_cparams(("arbitrary",)),
        name="forget_cumsum",
    )(fg, b_pad)


def _flash_kernel(*refs, mode, tq, tk, window, lam_init):
    it = iter(refs)
    q_ref = next(it)
    mm_ref = next(it) if mode == "sel" else None
    k_ref = next(it)
    v_ref = next(it)
    if mode == "fox":
        ck_ref = next(it)
    if mode == "diff":
        lam_ref = next(it)
        g_ref = next(it)
    o_ref = next(it)
    m_sc = next(it)
    acc_sc = next(it)
    v1_sc = next(it)
    s0_sc = next(it)
    s1_sc = next(it)
    if mode == "sel":
        ke_sc = next(it)

    qi = pl.program_id(2)
    S = k_ref.shape[0]
    lane = lax.broadcasted_iota(jnp.int32, (tq, LANES), 1)
    half0 = lane < HEAD_DIM

    @pl.when(qi == 0)
    def _():
        v1_sc[:, :LANES] = v_ref[...]
        v1_sc[:, LANES:] = jnp.ones((S, LANES), BF16)
        if mode == "sel":
            row = lax.broadcasted_iota(jnp.int32, (S, LANES), 0)
            ln = lax.broadcasted_iota(jnp.int32, (S, LANES), 1)
            blk = lax.shift_right_logical(row, int(math.log2(L_SEL)))
            ke_sc[:, :LANES] = k_ref[...]
            ke_sc[:, LANES:] = jnp.where(ln == blk, BIG, 0.0).astype(BF16)

    kk = ke_sc if mode == "sel" else k_ref

    q2 = q_ref[...].astype(F32)
    zero = jnp.zeros_like(q2)
    qa = jnp.where(half0, q2, zero).astype(BF16)
    qb = jnp.where(half0, zero, q2).astype(BF16)
    if mode == "sel":
        mmh = jnp.where(half0, mm_ref[...].astype(F32), zero).astype(BF16)
        qa = jnp.concatenate([qa, mmh], axis=1)
        qb = jnp.concatenate([qb, mmh], axis=1)
    qm = (qa, qb)

    m_sc[...] = jnp.full_like(m_sc, NEG)
    acc_sc[...] = jnp.zeros_like(acc_sc)
    q0 = qi * tq
    rb = LANES
    nrb = tq // rb
    dsq = (lax.broadcasted_iota(jnp.int32, (rb, LANES), 0)
           - lax.broadcasted_iota(jnp.int32, (rb, LANES), 1))

    nch = tq // LANES

    def qk(start, s_sc):
        koff = pl.multiple_of(start, tq)
        kt = kk[pl.ds(koff, tq), :]
        for a in range(2):
            s = _dot_nt(qm[a], kt)
            if mode == "fox":
                s = s - ck_ref[0, 0, a:a + 1, pl.ds(koff, tq)]
            s_sc[a] = s

    def soft_pv(s_sc, start, kind):
        koff = pl.multiple_of(start, tq)
        vt = v1_sc[pl.ds(koff, tq), :]
        rbk = FLASH_ROWS if kind == "full" else rb
        for a in range(2):
            for rbi in range(tq // rbk):
                rows = slice(rbi * rbk, (rbi + 1) * rbk)
                if kind == "full":
                    c_lo, c_hi, c_edge = 0, nch, None
                elif kind == "diag":
                    c_lo, c_hi, c_edge = 0, rbi + 1, rbi
                else:
                    c_lo, c_hi, c_edge = rbi, nch, rbi
                ch = [s_sc[a, rows, c * LANES:(c + 1) * LANES] for c in range(c_lo, c_hi)]
                if c_edge is not None:
                    ok = dsq >= 0 if kind == "diag" else dsq < 0
                    ch[c_edge - c_lo] = jnp.where(ok, ch[c_edge - c_lo], NEG)
                mx = functools.reduce(jnp.maximum, ch)
                m_old = m_sc[a, rows]
                m_new = jnp.maximum(m_old, jnp.max(mx, axis=1, keepdims=True))
                alpha = jnp.exp2(m_old - m_new)
                p = jnp.concatenate([jnp.exp2(c_ - m_new).astype(BF16) for c_ in ch], axis=1)
                m_sc[a, rows] = m_new
                al = jnp.concatenate([alpha, alpha], axis=1)
                acc_sc[a, rows] = (al * acc_sc[a, rows]
                                   + _dot(p, vt[c_lo * LANES:c_hi * LANES]))

    if mode == "win":
        @pl.when(qi > 0)
        def _():
            qk(q0 - window, s0_sc)
            qk(q0, s1_sc)
            soft_pv(s0_sc, q0 - window, "prev")
            soft_pv(s1_sc, q0, "diag")

        @pl.when(qi == 0)
        def _():
            qk(q0, s1_sc)
            soft_pv(s1_sc, q0, "diag")
    else:
        qk(0, s0_sc)

        def pair_body(i, carry):
            t0 = 2 * i * tq
            qk(t0 + tq, s1_sc)
            soft_pv(s0_sc, t0, "full")
            qk(t0 + 2 * tq, s0_sc)
            soft_pv(s1_sc, t0 + tq, "full")
            return carry

        lax.fori_loop(0, qi // 2, pair_body, 0)

        @pl.when(qi % 2 == 1)
        def _():
            qk(q0, s1_sc)
            soft_pv(s0_sc, q0 - tq, "full")
            soft_pv(s1_sc, q0, "diag")

        @pl.when(qi % 2 == 0)
        def _():
            soft_pv(s0_sc, q0, "diag")

    o0 = acc_sc[0, :, :LANES] / acc_sc[0, :, LANES:]
    o1 = acc_sc[1, :, :LANES] / acc_sc[1, :, LANES:]
    if mode == "diff":
        lq = lam_ref[...]
        lam = (jnp.exp(jnp.sum(lq[0:1] * lq[1:2], keepdims=True))
               - jnp.exp(jnp.sum(lq[2:3] * lq[3:4], keepdims=True)) + lam_init)
        d = o0 - lam * o1
        y = d * lax.rsqrt(jnp.mean(d * d, axis=-1, keepdims=True) + EPS)
        o_ref[...] = (y * g_ref[...]) * (1.0 - lam_init)
    else:
        o_ref[...] = jnp.where(half0, o0, o1)


def _flash(mode, B, S, n_blk, q, q_blk0, k, k_blk, v, v_blk, *,
           mm=None, ck=None, lam=None, subln_g=None, lam_init=0.0):
    tq = min(TQ, S)
    tk = min(TK, S)
    nq = S // tq
    T = B * S
    assert S % tk == 0 and tk == 2 * tq and WINDOW == tq and tq % LANES == 0
    in_specs = [pl.BlockSpec((tq, LANES), lambda b, h, i: (b * nq + i, q_blk0 + h))]
    args = [q]
    if mode == "sel":
        in_specs.append(pl.BlockSpec((tq, LANES), lambda b, h, i: (b * nq + i, h // 2)))
        args.append(mm)
    in_specs.append(pl.BlockSpec((S, LANES), lambda b, h, i: (b, k_blk(h))))
    args.append(k)
    in_specs.append(pl.BlockSpec((S, LANES), lambda b, h, i: (b, v_blk(h))))
    args.append(v)
    if mode == "fox":
        in_specs.append(pl.BlockSpec((1, 1, 2, S), lambda b, h, i: (b, h, 0, 0)))
        args += [ck]
    if mode == "diff":
        in_specs.append(pl.BlockSpec(lam.shape, lambda b, h, i: (0, 0)))
        in_specs.append(pl.BlockSpec((1, LANES), lambda b, h, i: (0, 0)))
        args += [lam, subln_g]
    scratch = [pltpu.VMEM((2, tq, LANES), F32), pltpu.VMEM((2, tq, 2 * LANES), F32),
               pltpu.VMEM((S, 2 * LANES), BF16),
               pltpu.VMEM((2, tq, tq), F32), pltpu.VMEM((2, tq, tq), F32)]
    if mode == "sel":
        scratch += [pltpu.VMEM((S, 2 * LANES), BF16)]
    return pl.pallas_call(
        functools.partial(_flash_kernel, mode=mode, tq=tq, tk=tk, window=WINDOW, lam_init=lam_init),
        grid=(B, n_blk, nq),
        in_specs=in_specs,
        out_specs=pl.BlockSpec((tq, LANES), lambda b, h, i: (b * nq + i, h)),
        out_shape=jax.ShapeDtypeStruct((T, n_blk * LANES), F32),
        scratch_shapes=scratch,
        compiler_params=_cparams(("parallel", "parallel", "arbitrary")),
        name="flash_" + mode,
    )(*args)


def _compress_kernel(t_ref, pe_ref, w1_ref, b1_ref, w2_ref, o_ref):
    t = t_ref[0, 0]
    R, half = t.shape
    pe = pe_ref[0]
    xa = (t + pe[:, :half]).astype(BF16)
    xb = (t + pe[:, half:]).astype(BF16)
    a = _dot(xa, w1_ref[0, :half, :])
    b = _dot(xb, w1_ref[0, half:, :])
    hp = a + pltpu.roll(b, R - 1, 0) + b1_ref[0]
    hid = hp * jax.nn.sigmoid(hp)
    o_ref[0, 0] = _dot(hid.astype(BF16), w2_ref[0]).astype(o_ref.dtype)


def _compress(tkv, pe, w1, b1, w2dup):
    _, B, R, W = tkv.shape
    G = G_B
    return pl.pallas_call(
        _compress_kernel,
        grid=(2, B, G),
        in_specs=[pl.BlockSpec((1, 1, R, W), lambda s, b, g: (s * G_B + g, b, 0, 0)),
                  pl.BlockSpec((1, 1, 2 * W), lambda s, b, g: (s, 0, 0)),
                  pl.BlockSpec((1, 2 * W, CMP_HIDDEN), lambda s, b, g: (s, 0, 0)),
                  pl.BlockSpec((1, 1, CMP_HIDDEN), lambda s, b, g: (s, 0, 0)),
                  pl.BlockSpec((1, CMP_HIDDEN, LANES), lambda s, b, g: (s, 0, 0))],
        out_specs=pl.BlockSpec((1, 1, R, LANES), lambda s, b, g: (s, b, 0, g)),
        out_shape=jax.ShapeDtypeStruct((2, B, R, G * LANES), BF16),
        compiler_params=_cparams(("parallel", "parallel", "parallel")),
        name="nsa_compress",
    )(tkv, pe, w1, b1, w2dup)


def _cmp_kernel(q_ref, kc_ref, vc_ref, cov_ref, o_ref, mm_ref, *, tq, n_sel, top_n):
    qi = pl.program_id(2)
    kc = kc_ref[0, 0]
    vc = vc_ref[0, 0]
    R = kc.shape[0]
    lane = lax.broadcasted_iota(jnp.int32, (tq, LANES), 1)
    half0 = lane < HEAD_DIM
    t_r = lax.broadcasted_iota(jnp.int32, (tq, R), 0) + qi * tq
    c_end = lax.broadcasted_iota(jnp.int32, (tq, R), 1) * CMP_STRIDE + (L_CMP - 1)
    ok = c_end <= t_r
    psum = jnp.zeros((tq, R), F32)
    outs = []
    for pr in range(2):
        q2 = q_ref[:, pr * LANES:(pr + 1) * LANES].astype(F32)
        zero = jnp.zeros_like(q2)
        pair = []
        for a in range(2):
            qm = (jnp.where(half0, q2, zero) if a == 0 else jnp.where(half0, zero, q2)).astype(BF16)
            s = jnp.where(ok, _dot_nt(qm, kc), NEG)
            m = jnp.max(s, axis=1, keepdims=True)
            e = jnp.where(ok, jnp.exp(s - m), 0.0)
            l = jnp.sum(e, axis=1, keepdims=True)
            p = e / jnp.where(l > 0.0, l, 1.0)
            psum = psum + p
            pair.append(_dot(p.astype(BF16), vc))
        outs.append(jnp.where(half0, pair[0], pair[1]))
    o_ref[...] = jnp.concatenate(outs, axis=1)

    imp = _dot_exact01(psum, cov_ref[...])
    jl = lane & (HEAD_DIM - 1)
    t_q = lax.broadcasted_iota(jnp.int32, (tq, LANES), 0) + qi * tq
    tb = lax.shift_right_logical(t_q, int(math.log2(L_SEL)))
    valid = jl <= tb
    forced = (jl == 0) | (jl == tb) | (jl == tb - 1)
    score = jnp.where(forced, LOWEST, jnp.where(valid, imp, -BIG))
    score = jnp.where(half0 & (jl < n_sel), score, LOWEST)
    mm = jnp.where(forced, 0.0, -1.0)
    for _ in range(top_n - 3):
        idx = jnp.argmax(score, axis=1, keepdims=True).astype(jnp.int32)
        mm = jnp.where(jl == idx, 0.0, mm)
        score = jnp.where(lane == idx, LOWEST, score)
    mm_ref[...] = mm.astype(mm_ref.dtype)


def _cmp_topk(B, S, q, q_blk0, ckv, cover_dup):
    tq = min(TQ, S)
    nq = S // tq
    T = B * S
    R = ckv.shape[2]
    n_sel = S // L_SEL
    gw = 2 * LANES
    return pl.pallas_call(
        functools.partial(_cmp_kernel, tq=tq, n_sel=n_sel, top_n=min(TOP_N, n_sel)),
        grid=(B, G_B, nq),
        in_specs=[pl.BlockSpec((tq, gw), lambda b, g, i: (b * nq + i, q_blk0 // 2 + g)),
                  pl.BlockSpec((1, 1, R, LANES), lambda b, g, i: (0, b, 0, g)),
                  pl.BlockSpec((1, 1, R, LANES), lambda b, g, i: (1, b, 0, g)),
                  pl.BlockSpec((R, LANES), lambda b, g, i: (0, 0))],
        out_specs=[pl.BlockSpec((tq, gw), lambda b, g, i: (b * nq + i, g)),
                   pl.BlockSpec((tq, LANES), lambda b, g, i: (b * nq + i, g))],
        out_shape=[jax.ShapeDtypeStruct((T, G_B * gw), F32),
                   jax.ShapeDtypeStruct((T, G_B * LANES), BF16)],
        compiler_params=_cparams(("parallel", "parallel", "parallel")),
        name="nsa_cmp_topk",
    )(q, ckv, ckv, cover_dup)


def _post_kernel(aa_ref, oc_ref, os_ref, ow_ref, ac_ref, fg_ref, h_ref, x_ref, p_ref,
                 wz_ref, wa_ref, wb_ref, wc_ref, wo_ref, wg_ref, wp_ref, e_ref, gn_ref,
                 *outs, last):
    def silu(z):
        return z * jax.nn.sigmoid(z)

    W = W_B
    D = x_ref.shape[1]
    mz = _dot(h_ref[...], wz_ref[...])
    za = mz[:, 3 * D:3 * D + W]
    zb = mz[:, 3 * D + W:3 * D + 2 * W]
    zc = mz[:, 3 * D + 2 * W:]
    ge = _dot_exact01(jax.nn.sigmoid(fg_ref[...]), e_ref[...])
    ya = aa_ref[...] * silu(za)
    yb = (ge[:, :W] * oc_ref[...] + ge[:, W:2 * W] * os_ref[...]
          + ge[:, 2 * W:] * ow_ref[...]) * silu(zb)
    yc = ac_ref[...] * silu(zc)
    pa = _dot(ya.astype(BF16), wa_ref[...])
    pb = _dot(yb.astype(BF16), wb_ref[...])
    pc = _dot(yc.astype(BF16), wc_ref[...])
    merged = (jax.nn.sigmoid(mz[:, :D]) * pa + jax.nn.sigmoid(mz[:, D:2 * D]) * pb
              + jax.nn.sigmoid(mz[:, 2 * D:3 * D]) * pc)
    x1 = x_ref[...] + _dot(merged.astype(BF16), wo_ref[...])
    gate = jax.nn.sigmoid(_dot(x1.astype(BF16), wg_ref[...]))
    x2 = x1 + gate * _dot(p_ref[...].astype(BF16), wp_ref[...])
    y = x2 * lax.rsqrt(jnp.mean(x2 * x2, axis=-1, keepdims=True) + EPS) * gn_ref[...]
    if last:
        outs[0][...] = y
    else:
        outs[0][...] = x2
        outs[1][...] = y.astype(BF16)


def _post(aa, oc, osel, ow, ac, o3, h, x, p, wz, wa, wb, wc, wo, wg, wp, e_mat, g_next, last):
    T, D = x.shape
    tm = min(POST_TM, T)
    W = W_A
    row = lambda i: (i, 0)
    const = lambda i: (0, 0)

    def resident(arr):
        return pl.BlockSpec(arr.shape, const, pipeline_mode=pl.Buffered(1))

    in_specs = [pl.BlockSpec((tm, W), row)] * 5 + [
        pl.BlockSpec((tm, LANES), row),
        pl.BlockSpec((tm, D), row),
        pl.BlockSpec((tm, D), row),
        pl.BlockSpec((tm, p.shape[1]), row),
        resident(wz), resident(wa), resident(wb), resident(wc), resident(wo), resident(wg),
        resident(wp), resident(e_mat), pl.BlockSpec((1, D), const)]
    if last:
        out_specs = [pl.BlockSpec((tm, D), row)]
        out_shape = [jax.ShapeDtypeStruct((T, D), F32)]
    else:
        out_specs = [pl.BlockSpec((tm, D), row), pl.BlockSpec((tm, D), row)]
        out_shape = [jax.ShapeDtypeStruct((T, D), F32), jax.ShapeDtypeStruct((T, D), BF16)]
    return pl.pallas_call(
        functools.partial(_post_kernel, last=last),
        grid=(T // tm,),
        in_specs=in_specs,
        out_specs=out_specs,
        out_shape=out_shape,
        compiler_params=_cparams(("parallel",)),
        name="post",
    )(aa, oc, osel, ow, ac, o3, h, x, p, wz, wa, wb, wc, wo, wg, wp, e_mat, g_next)


def _gate_expand_matrix():
    e = np.zeros((LANES, N_BRANCH * W_B), np.float32)
    for h in range(H_B):
        for r in range(N_BRANCH):
            e[H_A + h * N_BRANCH + r, r * W_B + h * HEAD_DIM:r * W_B + (h + 1) * HEAD_DIM] = 1.0
    return jnp.asarray(e, BF16)


def _cover_matrix(R, n_sel):
    c_start = np.arange(R)[:, None] * CMP_STRIDE
    j_start = np.arange(HEAD_DIM)[None, :] * L_SEL
    cov = ((c_start < j_start + L_SEL) & (c_start + L_CMP > j_start)
           & (np.arange(HEAD_DIM)[None, :] < n_sel)).astype(np.float32)
    return jnp.asarray(np.concatenate([cov, cov], axis=1), BF16)


def _layer_weights(w_in_i):
    offs = np.cumsum((0,) + SPLIT_SIZES)
    (QA, KA, VA, FA, ZA, QB, KCB, VCB, KSB, VSB, KWB, VWB, GB, ZB, QC, KC, VC, ZC, MG) = range(len(SPLIT_SIZES))
    D = w_in_i.shape[0]
    qs = SCALE * LOG2E

    col_scale = np.ones((offs[-1],), np.float32)
    col_scale[offs[QA]:offs[QA + 1]] = qs
    col_scale[offs[QC]:offs[QC + 1]] = qs
    wb = (w_in_i * col_scale).astype(BF16)

    def seg(k):
        return wb[:, offs[k]:offs[k + 1]]

    def dup(w):
        return [w[:, :HEAD_DIM], w[:, :HEAD_DIM], w[:, HEAD_DIM:], w[:, HEAD_DIM:]]

    qb_rot = (w_in_i[:, offs[QB]:offs[QB + 1]] * qs).astype(BF16)
    qb_plain = seg(QB) * SCALE
    w1 = jnp.concatenate([seg(QA), seg(KA), seg(VA), qb_plain, seg(VC)] + dup(seg(VSB)) + dup(seg(VWB)), axis=1)
    w2 = jnp.concatenate([qb_rot, seg(QC), seg(KC)] + dup(seg(KWB)) + dup(seg(KSB)), axis=1)
    pad = jnp.zeros((D, LANES - H_A - 3 * H_B), BF16)
    w3 = jnp.concatenate([seg(FA), seg(GB), pad, seg(KCB), seg(VCB)], axis=1)
    wz = jnp.concatenate([seg(MG), seg(ZA), seg(ZB), seg(ZC)], axis=1)
    return w1, w2, w3, wz


O1_QA, O1_KA, O1_VA, O1_QBU, O1_VC, O1_VS, O1_VW = 0, 4, 8, 12, 16, 20, 22
O2_QBR, O2_QC, O2_KC, O2_KW, O2_KS = 0, 4, 8, 12, 14


def kernel(x, p, positions, norm_g, w_in, b_forget, cmp_pe_k, cmp_w1_k, cmp_b1_k, cmp_w2_k,
           cmp_pe_v, cmp_w1_v, cmp_b1_v, cmp_w2_v, diff_lam, diff_subln_g,
           w_br_a, w_br_b, w_br_c, w_out, w_ple, w_ple_gate, final_g):
    B, S, D = x.shape
    depth = w_in.shape[0]
    T = B * S
    R = S // CMP_STRIDE
    n_sel = S // L_SEL
    assert n_sel <= HEAD_DIM and S % min(TQ, S) == 0

    xf = x.reshape(T, D)
    cos_t, sin_t = _rope_tables(positions.astype(F32).reshape(T))
    e_mat = _gate_expand_matrix()
    cover = _cover_matrix(R, n_sel)
    h = _rmsnorm(xf, norm_g[0], BF16)

    for i in range(depth):
        w1, w2, w3, wz = _layer_weights(w_in[i])
        o1 = _proj(h, w1, BF16, 1024, name="proj_plain")
        o2 = _proj(h, w2, BF16, 1024, rope_tabs=(cos_t, sin_t), name="proj_rope")
        fg, kv4 = _proj_f32(h, w3)

        b_pad = jnp.pad(b_forget[i], (0, LANES - H_A)).reshape(1, LANES)
        ck = _forget_cumsum(fg, b_pad, B, S).reshape(B, H_A // 2, 2, S)
        att_a = _flash("fox", B, S, H_A // 2, o1, O1_QA, o1, lambda h_: O1_KA + h_,
                       o1, lambda h_: O1_VA + h_, ck=ck)

        lam_init = 0.8 - 0.6 * math.exp(-0.3 * i)
        att_c = _flash("diff", B, S, H_C, o2, O2_QC, o2, lambda h_: O2_KC + h_,
                       o1, lambda h_: O1_VC + h_, lam=diff_lam[i],
                       subln_g=diff_subln_g[i].reshape(1, LANES), lam_init=lam_init)

        tkv = kv4.reshape(2 * G_B, B, R, CMP_STRIDE * HEAD_DIM)
        pe =jnp.stack([cmp_pe_k[i].reshape(1, -1), cmp_pe_v[i].reshape(1, -1)])
        cw1 = jnp.stack([cmp_w1_k[i], cmp_w1_v[i]]).astype(BF16)
        cb1 = jnp.stack([cmp_b1_k[i].reshape(1, -1), cmp_b1_v[i].reshape(1, -1)])
        cw2 = jnp.stack([cmp_w2_k[i], cmp_w2_v[i]])
        cw2 = jnp.concatenate([cw2, cw2], axis=-1).astype(BF16)
        ckv = _compress(tkv, pe, cw1, cb1, cw2)
        o_cmp, mm = _cmp_topk(B, S, o1, O1_QBU, ckv, cover)
        o_sel = _flash("sel", B, S, H_B // 2, o2, O2_QBR, o2, lambda h_: O2_KS + h_ // 2,
                       o1, lambda h_: O1_VS + h_ // 2, mm=mm)
        o_win = _flash("win", B, S, H_B // 2, o2, O2_QBR, o2, lambda h_: O2_KW + h_ // 2,
                       o1, lambda h_: O1_VW + h_ // 2)

        last = i == depth - 1
        g_next = (final_g if last else norm_g[i + 1]).reshape(1, D)
        res = _post(att_a, o_cmp, o_sel, o_win, att_c, fg, h, xf, p[i].reshape(T, -1),
                    wz, w_br_a[i].astype(BF16), w_br_b[i].astype(BF16), w_br_c[i].astype(BF16),
                    w_out[i].astype(BF16), w_ple_gate[i].astype(BF16), w_ple[i].astype(BF16),
                    e_mat, g_next, last)
        if last:
            return res[0].reshape(B, S, D)
        xf, h = res
```

```python
import functools
import math

import numpy as np
import jax
import jax.numpy as jnp
from jax import lax
from jax.experimental import pallas as pl
from jax.experimental.pallas import tpu as pltpu

F32 = jnp.float32
BF16 = jnp.bfloat16

LANES = 128
HEAD_DIM = 64
NEG = -1e30
BIG = 1e30
LOWEST = -3e38
EPS = 1e-6
ROPE_THETA = 10000.0
H_A = 8
H_B = 8
G_B = 2
H_C = 4
L_CMP = 32
CMP_STRIDE = 16
CMP_HIDDEN = 256
L_SEL = 64
TOP_N = 16
WINDOW = 512
N_BRANCH = 3
D_MODEL = 1024
W_A = H_A * HEAD_DIM
W_B = H_B * HEAD_DIM
W_C = H_C * 2 * HEAD_DIM
KV_B = G_B * HEAD_DIM
SPLIT_SIZES = (W_A, W_A, W_A, H_A, W_A,
               W_B, KV_B, KV_B, KV_B, KV_B, KV_B, KV_B, 3 * H_B, W_B,
               2 * H_C * HEAD_DIM, 2 * H_C * HEAD_DIM, W_C, W_C,
               N_BRANCH * D_MODEL)
SCALE = HEAD_DIM ** -0.5
LOG2E = math.log2(math.e)

VMEM_LIMIT = 48 * 1024 * 1024

TQ = 1024
CMP_TQ = 512
FLASH_ROWS = 512
PROJ_TM = 1024
POST_TM = 256


def _cparams(sem):
    return pltpu.CompilerParams(dimension_semantics=sem, vmem_limit_bytes=VMEM_LIMIT)


def _dot(a, b):
    return jnp.dot(a, b, preferred_element_type=F32)


def _dot_nt(a, b):
    return lax.dot_general(a, b, (((1,), (1,)), ((), ())), preferred_element_type=F32)


def _split3(x):
    hi = x.astype(BF16)
    r1 = x - hi.astype(F32)
    mid = r1.astype(BF16)
    lo = (r1 - mid.astype(F32)).astype(BF16)
    return hi, mid, lo


def _dot_exact01(x, m01):
    hi, mid, lo = _split3(x)
    return _dot(hi, m01) + _dot(mid, m01) + _dot(lo, m01)


def _rmsnorm_kernel(x_ref, g_ref, o_ref):
    x = x_ref[...]
    y = x * lax.rsqrt(jnp.mean(x * x, axis=-1, keepdims=True) + EPS)
    o_ref[...] = (y * g_ref[...]).astype(o_ref.dtype)


def _rmsnorm(x, g, out_dtype, tm=1024):
    T, D = x.shape
    return pl.pallas_call(
        _rmsnorm_kernel,
        grid=(T // tm,),
        in_specs=[pl.BlockSpec((tm, D), lambda i: (i, 0)),
                  pl.BlockSpec((1, D), lambda i: (0, 0))],
        out_specs=pl.BlockSpec((tm, D), lambda i: (i, 0)),
        out_shape=jax.ShapeDtypeStruct((T, D), out_dtype),
        compiler_params=_cparams(("parallel",)),
        name="rmsnorm",
    )(x, g.reshape(1, D))


def _rope_table_kernel(pos_ref, invf_ref, sign_ref, cos_ref, sin_ref):
    ang = pos_ref[...] * invf_ref[...]
    cos_ref[...] = jnp.cos(ang)
    sin_ref[...] = jnp.sin(ang) * sign_ref[...]


def _rope_tables(pos_f32, tm=512):
    T = pos_f32.shape[0]
    half = HEAD_DIM // 2
    inv_freq = ROPE_THETA ** (-jnp.arange(half, dtype=F32) / half)
    invf = jnp.tile(inv_freq, LANES // half).reshape(1, LANES)
    sign = jnp.tile(jnp.concatenate([-jnp.ones((half,), F32), jnp.ones((half,), F32)]),
                    LANES // HEAD_DIM).reshape(1, LANES)
    return pl.pallas_call(
        _rope_table_kernel,
        grid=(T // tm,),
        in_specs=[pl.BlockSpec((tm, 1), lambda i: (i, 0)),
                  pl.BlockSpec((1, LANES), lambda i: (0, 0)),
                  pl.BlockSpec((1, LANES), lambda i: (0, 0))],
        out_specs=[pl.BlockSpec((tm, LANES), lambda i: (i, 0)),
                   pl.BlockSpec((tm, LANES), lambda i: (i, 0))],
        out_shape=[jax.ShapeDtypeStruct((T, LANES), F32),
                   jax.ShapeDtypeStruct((T, LANES), F32)],
        compiler_params=_cparams(("parallel",)),
        name="rope_table",
    )(pos_f32.reshape(T, 1), invf, sign)


def _proj_kernel(h_ref, w_ref, *rest, rope):
    acc = _dot(h_ref[...], w_ref[...])
    if rope:
        cos_ref, sin_ref, o_ref = rest
        cos = cos_ref[...]
        sin = sin_ref[...]
        lane = lax.broadcasted_iota(jnp.int32, cos.shape, 1)
        first = (lane & (HEAD_DIM - 1)) < (HEAD_DIM // 2)
        for c in range(acc.shape[1] // LANES):
            t = acc[:, c * LANES:(c + 1) * LANES]
            sw = jnp.where(first, pltpu.roll(t, LANES - HEAD_DIM // 2, 1),
                           pltpu.roll(t, HEAD_DIM // 2, 1))
            o_ref[:, c * LANES:(c + 1) * LANES] = (t * cos + sw * sin).astype(o_ref.dtype)
    else:
        (o_ref,) = rest
        o_ref[...] = acc.astype(o_ref.dtype)


def _proj_f32_kernel(h_ref, w_ref, fg_ref, kv_ref):
    acc = _dot(h_ref[...], w_ref[...])
    fg_ref[...] = acc[:, :LANES]
    for c in range(2 * G_B):
        kv_ref[c] = acc[:, LANES + c * HEAD_DIM:LANES + (c + 1) * HEAD_DIM]


def _proj_f32(h, w):
    T, D = h.shape
    tm = min(PROJ_TM, T)
    return pl.pallas_call(
        _proj_f32_kernel,
        grid=(T // tm,),
        in_specs=[pl.BlockSpec((tm, D), lambda i: (i, 0)),
                  pl.BlockSpec(w.shape, lambda i: (0, 0))],
        out_specs=[pl.BlockSpec((tm, LANES), lambda i: (i, 0)),
                   pl.BlockSpec((2 * G_B, tm, HEAD_DIM), lambda i: (0, i, 0))],
        out_shape=[jax.ShapeDtypeStruct((T, LANES), F32),
                   jax.ShapeDtypeStruct((2 * G_B, T, HEAD_DIM), F32)],
        compiler_params=_cparams(("parallel",)),
        name="proj_f32",
    )(h, w)


def _proj(h, w, out_dtype, tn, rope_tabs=None, name="proj"):
    T, D = h.shape
    N = w.shape[1]
    tm = min(PROJ_TM, T)
    assert T % tm == 0 and N % tn == 0
    in_specs = [pl.BlockSpec((tm, D), lambda i, j: (i, 0)),
                pl.BlockSpec((D, tn), lambda i, j: (0, j))]
    args = [h, w]
    if rope_tabs is not None:
        in_specs += [pl.BlockSpec((tm, LANES), lambda i, j: (i, 0)),
                     pl.BlockSpec((tm, LANES), lambda i, j: (i, 0))]
        args += list(rope_tabs)
    return pl.pallas_call(
        functools.partial(_proj_kernel, rope=rope_tabs is not None),
        grid=(T // tm, N // tn),
        in_specs=in_specs,
        out_specs=pl.BlockSpec((tm, tn), lambda i, j: (i, j)),
        out_shape=jax.ShapeDtypeStruct((T, N), out_dtype),
        compiler_params=_cparams(("parallel", "arbitrary")),
        name=name,
    )(*args)


def _cumsum_kernel(fg_ref, b_ref, c_ref, carry_sc, *, tiles_per_seq):
    i = pl.program_id(0)

    @pl.when(i % tiles_per_seq == 0)
    def _():
        carry_sc[...] = jnp.zeros_like(carry_sc)

    z = fg_ref[...] + b_ref[...]
    logf = jnp.minimum(z, 0.0) - jnp.log1p(jnp.exp(-jnp.abs(z)))
    tm = z.shape[0]
    r = lax.broadcasted_iota(jnp.int32, (tm, tm), 0)
    c = lax.broadcasted_iota(jnp.int32, (tm, tm), 1)
    tri = jnp.where(c <= r, 1.0, 0.0).astype(BF16)
    cs = _dot_exact01_left(tri, logf) + carry_sc[...]
    carry_sc[...] = cs[tm - 1:tm, :]
    c_ref[0] = jnp.transpose(cs * LOG2E)[:H_A, :]


def _dot_exact01_left(m01, x):
    hi, mid, lo = _split3(x)
    return _dot(m01, hi) + _dot(m01, mid) + _dot(m01, lo)


def _forget_cumsum(fg, b_pad, B, S, tm=256):
    tps = S // tm
    return pl.pallas_call(
        functools.partial(_cumsum_kernel, tiles_per_seq=tps),
        grid=(B * tps,),
        in_specs=[pl.BlockSpec((tm, LANES), lambda i: (i, 0)),
                  pl.BlockSpec((1, LANES), lambda i: (0, 0))],
        out_specs=pl.BlockSpec((1, H_A, tm), lambda i: (i // tps, 0, i % tps)),
        out_shape=jax.ShapeDtypeStruct((B, H_A, S), F32),
        scratch_shapes=[pltpu.VMEM((1, LANES), F32)],
        compiler_params=_cparams(("arbitrary",)),
        name="forget_cumsum",
    )(fg, b_pad)


def _flash_kernel(*refs, mode, tq, tw, lam_init):
    it = iter(refs)
    q_ref = next(it)
    mm_ref = next(it) if mode == "sel" else None
    k_ref = next(it)
    v_ref = next(it)
    if mode == "fox":
        ck_ref = next(it)
    if mode == "diff":
        lam_ref = next(it)
        g_ref = next(it)
    o_ref = next(it)
    m_sc = next(it)
    acc_sc = next(it)
    v1_sc = next(it)
    s0_sc = next(it)
    s1_sc = next(it)
    s2_sc = next(it)
    if mode == "sel":
        ke_sc = next(it)

    qi = pl.program_id(2)
    S = k_ref.shape[0]
    lane = lax.broadcasted_iota(jnp.int32, (tq, LANES), 1)
    half0 = lane < HEAD_DIM

    @pl.when(qi == 0)
    def _():
        v1_sc[:, :LANES] = v_ref[...]
        v1_sc[:, LANES:] = jnp.ones((S, LANES), BF16)
        if mode == "sel":
            row = lax.broadcasted_iota(jnp.int32, (S, LANES), 0)
            ln = lax.broadcasted_iota(jnp.int32, (S, LANES), 1)
            blk = lax.shift_right_logical(row, int(math.log2(L_SEL)))
            ke_sc[:, :LANES] = k_ref[...]
            ke_sc[:, LANES:] = jnp.where(ln == blk, BIG, 0.0).astype(BF16)

    kk = ke_sc if mode == "sel" else k_ref

    def split_halves(qr, mr):
        q2 = qr[...].astype(F32)
        zero = jnp.zeros_like(q2)
        qa = jnp.where(half0, q2, zero).astype(BF16)
        qb = jnp.where(half0, zero, q2).astype(BF16)
        if mode == "sel":
            mmh = jnp.where(half0, mr[...].astype(F32), zero).astype(BF16)
            qa = jnp.concatenate([qa, mmh], axis=1)
            qb = jnp.concatenate([qb, mmh], axis=1)
        return qa, qb

    qm = split_halves(q_ref, mm_ref)

    m_sc[...] = jnp.full_like(m_sc, NEG)
    acc_sc[...] = jnp.zeros_like(acc_sc)
    q0 = qi * tq
    rb = LANES
    nrb = tq // rb
    nch = tw // LANES
    hrb = nrb // 2
    dsq = (lax.broadcasted_iota(jnp.int32, (rb, LANES), 0)
           - lax.broadcasted_iota(jnp.int32, (rb, LANES), 1))

    def qk(start, s_sc, r_lo=0, r_hi=tq):
        koff = pl.multiple_of(start, tw)
        kt = kk[pl.ds(koff, tw), :]
        for a in range(2):
            s = _dot_nt(qm[a][r_lo:r_hi], kt)
            if mode == "fox":
                s = s - ck_ref[0, 0, a:a + 1, pl.ds(koff, tw)]
            s_sc[a, r_lo:r_hi] = s

    FULL = [("full",)] * nrb
    DIAG_A = [("diag", r) for r in range(hrb)] + [("full",)] * hrb
    DIAG_B = [None] * hrb + [("diag", r) for r in range(hrb)]
    PREV_P = [("prev", r) for r in range(hrb)] + [None] * hrb
    WIN_A = [("diag", r) for r in range(hrb)] + [("prev", r) for r in range(hrb)]

    def soft_pv(s_sc, start, kinds):
        koff = pl.multiple_of(start, tw)
        vt = v1_sc[pl.ds(koff, tw), :]
        all_full = all(k == ("full",) for k in kinds)
        rbk = FLASH_ROWS if all_full else rb
        for a in range(2):
            for rbi in range(tq // rbk):
                kind = ("full",) if all_full else kinds[rbi]
                if kind is None:
                    continue
                rows = slice(rbi * rbk, (rbi + 1) * rbk)
                if kind[0] == "full":
                    c_lo, c_hi, c_edge = 0, nch, None
                elif kind[0] == "diag":
                    c_lo, c_hi, c_edge = 0, kind[1] + 1, kind[1]
                else:
                    c_lo, c_hi, c_edge = kind[1], nch, kind[1]
                ch = [s_sc[a, rows, c * LANES:(c + 1) * LANES] for c in range(c_lo, c_hi)]
                if c_edge is not None:
                    ok = dsq >= 0 if kind[0] == "diag" else dsq < 0
                    ch[c_edge - c_lo] = jnp.where(ok, ch[c_edge - c_lo], NEG)
                mx = functools.reduce(jnp.maximum, ch)
                m_old = m_sc[a, rows]
                m_new = jnp.maximum(m_old, jnp.max(mx, axis=1, keepdims=True))
                alpha = jnp.exp2(m_old - m_new)
                p = jnp.concatenate([jnp.exp2(c_ - m_new).astype(BF16) for c_ in ch], axis=1)
                m_sc[a, rows] = m_new
                al = jnp.concatenate([alpha, alpha], axis=1)
                acc_sc[a, rows] = (al * acc_sc[a, rows]
                                   + _dot(p, vt[c_lo * LANES:c_hi * LANES]))

    if mode == "win":
        @pl.when(qi > 0)
        def _():
            qk(q0 - tw, s0_sc, 0, tq // 2)
            qk(q0, s1_sc)
            qk(q0 + tw, s2_sc, tq // 2, tq)
            soft_pv(s0_sc, q0 - tw, PREV_P)
            soft_pv(s1_sc, q0, WIN_A)
            soft_pv(s2_sc, q0 + tw, DIAG_B)

        @pl.when(qi == 0)
        def _():
            qk(q0, s1_sc)
            qk(q0 + tw, s2_sc, tq // 2, tq)
            soft_pv(s1_sc, q0, WIN_A)
            soft_pv(s2_sc, q0 + tw, DIAG_B)
    else:
        qk(0, s0_sc)

        def pair_body(i, carry):
            t0 = 2 * i * tw
            qk(t0 + tw, s1_sc)
            soft_pv(s0_sc, t0, FULL)
            qk(t0 + 2 * tw, s0_sc)
            soft_pv(s1_sc, t0 + tw, FULL)
            return carry

        lax.fori_loop(0, qi, pair_body, 0)
        qk(q0 + tw, s1_sc, tq // 2, tq)
        soft_pv(s0_sc, q0, DIAG_A)
        soft_pv(s1_sc, q0 + tw, DIAG_B)

    o0 = acc_sc[0, :, :LANES] / acc_sc[0, :, LANES:]
    o1 = acc_sc[1, :, :LANES] / acc_sc[1, :, LANES:]
    if mode == "diff":
        lq = lam_ref[...]
        lam = (jnp.exp(jnp.sum(lq[0:1] * lq[1:2], keepdims=True))
               - jnp.exp(jnp.sum(lq[2:3] * lq[3:4], keepdims=True)) + lam_init)
        d = o0 - lam * o1
        y = d * lax.rsqrt(jnp.mean(d * d, axis=-1, keepdims=True) + EPS)
        o_ref[...] = (y * g_ref[...]) * (1.0 - lam_init)
    else:
        o_ref[...] = jnp.where(half0, o0, o1)


def _flash(mode, B, S, n_blk, q, q_blk0, k, k_blk, v, v_blk, *,
           mm=None, ck=None, lam=None, subln_g=None, lam_init=0.0):
    tq = min(TQ, S)
    tw = WINDOW
    nq = S // tq
    T = B * S
    assert S % tq == 0 and tq == 2 * tw and tw % LANES == 0
    in_specs = [pl.BlockSpec((tq, LANES), lambda b, h, i: (b * nq + i, q_blk0 + h))]
    args = [q]
    if mode == "sel":
        in_specs.append(pl.BlockSpec((tq, LANES), lambda b, h, i: (b * nq + i, h // 2)))
        args.append(mm)
    in_specs.append(pl.BlockSpec((S, LANES), lambda b, h, i: (b, k_blk(h))))
    args.append(k)
    in_specs.append(pl.BlockSpec((S, LANES), lambda b, h, i: (b, v_blk(h))))
    args.append(v)
    if mode == "fox":
        in_specs.append(pl.BlockSpec((1, 1, 2, S), lambda b, h, i: (b, h, 0, 0)))
        args += [ck]
    if mode == "diff":
        in_specs.append(pl.BlockSpec(lam.shape, lambda b, h, i: (0, 0)))
        in_specs.append(pl.BlockSpec((1, LANES), lambda b, h, i: (0, 0)))
        args += [lam, subln_g]
    scratch = [pltpu.VMEM((2, tq, LANES), F32), pltpu.VMEM((2, tq, 2 * LANES), F32),
               pltpu.VMEM((S, 2 * LANES), BF16),
               pltpu.VMEM((2, tq, tw), F32), pltpu.VMEM((2, tq, tw), F32),
               pltpu.VMEM((2, tq, tw), F32)]
    if mode == "sel":
        scratch += [pltpu.VMEM((S, 2 * LANES), BF16)]
    return pl.pallas_call(
        functools.partial(_flash_kernel, mode=mode, tq=tq, tw=tw, lam_init=lam_init),
        grid=(B, n_blk, nq),
        in_specs=in_specs,
        out_specs=pl.BlockSpec((tq, LANES), lambda b, h, i: (b * nq + i, h)),
        out_shape=jax.ShapeDtypeStruct((T, n_blk * LANES), F32),
        scratch_shapes=scratch,
        compiler_params=_cparams(("parallel", "parallel", "arbitrary")),
        name="flash_" + mode,
    )(*args)


def _compress_kernel(t_ref, pe_ref, w1_ref, b1_ref, w2_ref, o_ref):
    t = t_ref[0, 0]
    R, half = t.shape
    pe = pe_ref[0]
    xa = (t + pe[:, :half]).astype(BF16)
    xb = (t + pe[:, half:]).astype(BF16)
    a = _dot(xa, w1_ref[0, :half, :])
    b = _dot(xb, w1_ref[0, half:, :])
    hp = a + pltpu.roll(b, R - 1, 0) + b1_ref[0]
    hid = hp * jax.nn.sigmoid(hp)
    o_ref[0, 0] = _dot(hid.astype(BF16), w2_ref[0]).astype(o_ref.dtype)


def _compress(tkv, pe, w1, b1, w2dup):
    _, B, R, W = tkv.shape
    G = G_B
    return pl.pallas_call(
        _compress_kernel,
        grid=(2, B, G),
        in_specs=[pl.BlockSpec((1, 1, R, W), lambda s, b, g: (s * G_B + g, b, 0, 0)),
                  pl.BlockSpec((1, 1, 2 * W), lambda s, b, g: (s, 0, 0)),
                  pl.BlockSpec((1, 2 * W, CMP_HIDDEN), lambda s, b, g: (s, 0, 0)),
                  pl.BlockSpec((1, 1, CMP_HIDDEN), lambda s, b, g: (s, 0, 0)),
                  pl.BlockSpec((1, CMP_HIDDEN, LANES), lambda s, b, g: (s, 0, 0))],
        out_specs=pl.BlockSpec((1, 1, R, LANES), lambda s, b, g: (s, b, 0, g)),
        out_shape=jax.ShapeDtypeStruct((2, B, R, G * LANES), BF16),
        compiler_params=_cparams(("parallel", "parallel", "parallel")),
        name="nsa_compress",
    )(tkv, pe, w1, b1, w2dup)


def _cmp_kernel(q_ref, kc_ref, vc_ref, cov_ref, o_ref, mm_ref, *, tq, n_sel, top_n):
    qi = pl.program_id(2)
    kc = kc_ref[0, 0]
    vc = vc_ref[0, 0]
    R = kc.shape[0]
    lane = lax.broadcasted_iota(jnp.int32, (tq, LANES), 1)
    half0 = lane < HEAD_DIM
    t_r = lax.broadcasted_iota(jnp.int32, (tq, R), 0) + qi * tq
    c_end = lax.broadcasted_iota(jnp.int32, (tq, R), 1) * CMP_STRIDE + (L_CMP - 1)
    ok = c_end <= t_r
    psum = jnp.zeros((tq, R), F32)
    outs = []
    for pr in range(2):
        q2 = q_ref[:, pr * LANES:(pr + 1) * LANES].astype(F32)
        zero = jnp.zeros_like(q2)
        pair = []
        for a in range(2):
            qm = (jnp.where(half0, q2, zero) if a == 0 else jnp.where(half0, zero, q2)).astype(BF16)
            s = jnp.where(ok, _dot_nt(qm, kc), NEG)
            m = jnp.max(s, axis=1, keepdims=True)
            e = jnp.where(ok, jnp.exp(s - m), 0.0)
            l = jnp.sum(e, axis=1, keepdims=True)
            p = e / jnp.where(l > 0.0, l, 1.0)
            psum = psum + p
            pair.append(_dot(p.astype(BF16), vc))
        outs.append(jnp.where(half0, pair[0], pair[1]))
    o_ref[...] = jnp.concatenate(outs, axis=1)

    imp = _dot_exact01(psum, cov_ref[...])
    jl = lane & (HEAD_DIM - 1)
    t_q = lax.broadcasted_iota(jnp.int32, (tq, LANES), 0) + qi * tq
    tb = lax.shift_right_logical(t_q, int(math.log2(L_SEL)))
    valid = jl <= tb
    forced = (jl == 0) | (jl == tb) | (jl == tb - 1)
    score = jnp.where(forced, LOWEST, jnp.where(valid, imp, -BIG))
    score = jnp.where(half0 & (jl < n_sel), score, LOWEST)
    mm = jnp.where(forced, 0.0, -1.0)
    for _ in range(top_n - 3):
        idx = jnp.argmax(score, axis=1, keepdims=True).astype(jnp.int32)
        mm = jnp.where(jl == idx, 0.0, mm)
        score = jnp.where(lane == idx, LOWEST, score)
    mm_ref[...] = mm.astype(mm_ref.dtype)


def _cmp_topk(B, S, q, q_blk0, ckv, cover_dup):
    tq = min(CMP_TQ, S)
    nq = S // tq
    T = B * S
    R = ckv.shape[2]
    n_sel = S // L_SEL
    gw = 2 * LANES
    return pl.pallas_call(
        functools.partial(_cmp_kernel, tq=tq, n_sel=n_sel, top_n=min(TOP_N, n_sel)),
        grid=(B, G_B, nq),
        in_specs=[pl.BlockSpec((tq, gw), lambda b, g, i: (b * nq + i, q_blk0 // 2 + g)),
                  pl.BlockSpec((1, 1, R, LANES), lambda b, g, i: (0, b, 0, g)),
                  pl.BlockSpec((1, 1, R, LANES), lambda b, g, i: (1, b, 0, g)),
                  pl.BlockSpec((R, LANES), lambda b, g, i: (0, 0))],
        out_specs=[pl.BlockSpec((tq, gw), lambda b, g, i: (b * nq + i, g)),
                   pl.BlockSpec((tq, LANES), lambda b, g, i: (b * nq + i, g))],
        out_shape=[jax.ShapeDtypeStruct((T, G_B * gw), F32),
                   jax.ShapeDtypeStruct((T, G_B * LANES), BF16)],
        compiler_params=_cparams(("parallel", "parallel", "parallel")),
        name="nsa_cmp_topk",
    )(q, ckv, ckv, cover_dup)


def _post_kernel(aa_ref, oc_ref, os_ref, ow_ref, ac_ref, fg_ref, h_ref, x_ref, p_ref,
                 wz_ref, wa_ref, wb_ref, wc_ref, wo_ref, wg_ref, wp_ref, e_ref, gn_ref,
                 *outs, last):
    def silu(z):
        return z * jax.nn.sigmoid(z)

    W = W_B
    D = x_ref.shape[1]
    mz = _dot(h_ref[...], wz_ref[...])
    za = mz[:, 3 * D:3 * D + W]
    zb = mz[:, 3 * D + W:3 * D + 2 * W]
    zc = mz[:, 3 * D + 2 * W:]
    ge = _dot_exact01(jax.nn.sigmoid(fg_ref[...]), e_ref[...])
    ya = aa_ref[...] * silu(za)
    yb = (ge[:, :W] * oc_ref[...] + ge[:, W:2 * W] * os_ref[...]
          + ge[:, 2 * W:] * ow_ref[...]) * silu(zb)
    yc = ac_ref[...] * silu(zc)
    pa = _dot(ya.astype(BF16), wa_ref[...])
    pb = _dot(yb.astype(BF16), wb_ref[...])
    pc = _dot(yc.astype(BF16), wc_ref[...])
    merged = (jax.nn.sigmoid(mz[:, :D]) * pa + jax.nn.sigmoid(mz[:, D:2 * D]) * pb
              + jax.nn.sigmoid(mz[:, 2 * D:3 * D]) * pc)
    x1 = x_ref[...] + _dot(merged.astype(BF16), wo_ref[...])
    gate = jax.nn.sigmoid(_dot(x1.astype(BF16), wg_ref[...]))
    x2 = x1 + gate * _dot(p_ref[...].astype(BF16), wp_ref[...])
    y = x2 * lax.rsqrt(jnp.mean(x2 * x2, axis=-1, keepdims=True) + EPS) * gn_ref[...]
    if last:
        outs[0][...] = y
    else:
        outs[0][...] = x2
        outs[1][...] = y.astype(BF16)


def _post(aa, oc, osel, ow, ac, o3, h, x, p, layer, wz, wa, wb, wc, wo, wg, wp, e_mat, g_next, last):
    T, D = x.shape
    tm = min(POST_TM, T)
    W = W_A
    row = lambda i: (i, 0)
    const = lambda i: (0, 0)

    def resident(arr):
        return pl.BlockSpec(arr.shape, const, pipeline_mode=pl.Buffered(1))

    in_specs = [pl.BlockSpec((tm, W), row)] * 5 + [
        pl.BlockSpec((tm, LANES), row),
        pl.BlockSpec((tm, D), row),
        pl.BlockSpec((tm, D), row),
        pl.BlockSpec((tm, p.shape[1]), lambda i: (i + layer * (T // tm), 0)),
        resident(wz), resident(wa), resident(wb), resident(wc), resident(wo), resident(wg),
        resident(wp), resident(e_mat), pl.BlockSpec((1, D), const)]
    if last:
        out_specs = [pl.BlockSpec((tm, D), row)]
        out_shape = [jax.ShapeDtypeStruct((T, D), F32)]
    else:
        out_specs = [pl.BlockSpec((tm, D), row), pl.BlockSpec((tm, D), row)]
        out_shape = [jax.ShapeDtypeStruct((T, D), F32), jax.ShapeDtypeStruct((T, D), BF16)]
    return pl.pallas_call(
        functools.partial(_post_kernel, last=last),
        grid=(T // tm,),
        in_specs=in_specs,
        out_specs=out_specs,
        out_shape=out_shape,
        compiler_params=_cparams(("parallel",)),
        name="post",
    )(aa, oc, osel, ow, ac, o3, h, x, p, wz, wa, wb, wc, wo, wg, wp, e_mat, g_next)


def _gate_expand_matrix():
    e = np.zeros((LANES, N_BRANCH * W_B), np.float32)
    for h in range(H_B):
        for r in range(N_BRANCH):
            e[H_A + h * N_BRANCH + r, r * W_B + h * HEAD_DIM:r * W_B + (h + 1) * HEAD_DIM] = 1.0
    return jnp.asarray(e, BF16)


def _cover_matrix(R, n_sel):
    c_start = np.arange(R)[:, None] * CMP_STRIDE
    j_start = np.arange(HEAD_DIM)[None, :] * L_SEL
    cov = ((c_start < j_start + L_SEL) & (c_start + L_CMP > j_start)
           & (np.arange(HEAD_DIM)[None, :] < n_sel)).astype(np.float32)
    return jnp.asarray(np.concatenate([cov, cov], axis=1), BF16)


def _layer_weights(w_in_i):
    offs = np.cumsum((0,) + SPLIT_SIZES)
    (QA, KA, VA, FA, ZA, QB, KCB, VCB, KSB, VSB, KWB, VWB, GB, ZB, QC, KC, VC, ZC, MG) = range(len(SPLIT_SIZES))
    D = w_in_i.shape[0]
    qs = SCALE * LOG2E

    col_scale = np.ones((offs[-1],), np.float32)
    col_scale[offs[QA]:offs[QA + 1]] = qs
    col_scale[offs[QC]:offs[QC + 1]] = qs
    wb = (w_in_i * col_scale).astype(BF16)

    def seg(k):
        return wb[:, offs[k]:offs[k + 1]]

    def dup(w):
        return [w[:, :HEAD_DIM], w[:, :HEAD_DIM], w[:, HEAD_DIM:], w[:, HEAD_DIM:]]

    qb_rot = (w_in_i[:, offs[QB]:offs[QB + 1]] * qs).astype(BF16)
    qb_plain = seg(QB) * SCALE
    w1 = jnp.concatenate([seg(QA), seg(KA), seg(VA), qb_plain, seg(VC)] + dup(seg(VSB)) + dup(seg(VWB)), axis=1)
    w2 = jnp.concatenate([qb_rot, seg(QC), seg(KC)] + dup(seg(KWB)) + dup(seg(KSB)), axis=1)
    pad = jnp.zeros((D, LANES - H_A - 3 * H_B), BF16)
    w3 = jnp.concatenate([seg(FA), seg(GB), pad, seg(KCB), seg(VCB)], axis=1)
    wz = jnp.concatenate([seg(MG), seg(ZA), seg(ZB), seg(ZC)], axis=1)
    return w1, w2, w3, wz


O1_QA, O1_KA, O1_VA, O1_QBU, O1_VC, O1_VS, O1_VW = 0, 4, 8, 12, 16, 20, 22
O2_QBR, O2_QC, O2_KC, O2_KW, O2_KS = 0, 4, 8, 12, 14


def kernel(x, p, positions, norm_g, w_in, b_forget, cmp_pe_k, cmp_w1_k, cmp_b1_k, cmp_w2_k,
           cmp_pe_v, cmp_w1_v, cmp_b1_v, cmp_w2_v, diff_lam, diff_subln_g,
           w_br_a, w_br_b, w_br_c, w_out, w_ple, w_ple_gate, final_g):
    B, S, D = x.shape
    depth = w_in.shape[0]
    T = B * S
    R = S // CMP_STRIDE
    n_sel = S // L_SEL
    assert n_sel <= HEAD_DIM and S % min(TQ, S) == 0

    xf = x.reshape(T, D)
    cos_t, sin_t = _rope_tables(positions.astype(F32).reshape(T))
    e_mat = _gate_expand_matrix()
    cover = _cover_matrix(R, n_sel)
    h = _rmsnorm(xf, norm_g[0], BF16)

    for i in range(depth):
        w1, w2, w3, wz = _layer_weights(w_in[i])
        o1 = _proj(h, w1, BF16, 1024, name="proj_plain")
        o2 = _proj(h, w2, BF16, 1024, rope_tabs=(cos_t, sin_t), name="proj_rope")
        fg, kv4 = _proj_f32(h, w3)

        b_pad = jnp.pad(b_forget[i], (0, LANES - H_A)).reshape(1, LANES)
        ck = _forget_cumsum(fg, b_pad, B, S).reshape(B, H_A // 2, 2, S)
        att_a = _flash("fox", B, S, H_A // 2, o1, O1_QA, o1, lambda h_: O1_KA + h_,
                       o1, lambda h_: O1_VA + h_, ck=ck)

        lam_init = 0.8 - 0.6 * math.exp(-0.3 * i)
        att_c = _flash("diff", B, S, H_C, o2, O2_QC, o2, lambda h_: O2_KC + h_,
                       o1, lambda h_: O1_VC + h_, lam=diff_lam[i],
                       subln_g=diff_subln_g[i].reshape(1, LANES), lam_init=lam_init)

        tkv = kv4.reshape(2 * G_B, B, R, CMP_STRIDE * HEAD_DIM)
        pe =jnp.stack([cmp_pe_k[i].reshape(1, -1), cmp_pe_v[i].reshape(1, -1)])
        cw1 = jnp.stack([cmp_w1_k[i], cmp_w1_v[i]]).astype(BF16)
        cb1 = jnp.stack([cmp_b1_k[i].reshape(1, -1), cmp_b1_v[i].reshape(1, -1)])
        cw2 = jnp.stack([cmp_w2_k[i], cmp_w2_v[i]])
        cw2 = jnp.concatenate([cw2, cw2], axis=-1).astype(BF16)
        ckv = _compress(tkv, pe, cw1, cb1, cw2)
        o_cmp, mm = _cmp_topk(B, S, o1, O1_QBU, ckv, cover)
        o_sel = _flash("sel", B, S, H_B // 2, o2, O2_QBR, o2, lambda h_: O2_KS + h_ // 2,
                       o1, lambda h_: O1_VS + h_ // 2, mm=mm)
        o_win = _flash("win", B, S, H_B // 2, o2, O2_QBR, o2, lambda h_: O2_KW + h_ // 2,
                       o1, lambda h_: O1_VW + h_ // 2)

        last = i == depth - 1
        g_next = (final_g if last else norm_g[i + 1]).reshape(1, D)
        res = _post(att_a, o_cmp, o_sel, o_win, att_c, fg, h, xf, p.reshape(depth * T, -1), i,
                    wz, w_br_a[i].astype(BF16), w_br_b[i].astype(BF16), w_br_c[i].astype(BF16),
                    w_out[i].astype(BF16), w_ple_gate[i].astype(BF16), w_ple[i].astype(BF16),
                    e_mat, g_next, last)
        if last:
            return res[0].reshape(B, S, D)
        xf, h = res
```

```python
import functools
import math

import numpy as np
import jax
import jax.numpy as jnp
from jax import lax
from jax.experimental import pallas as pl
from jax.experimental.pallas import tpu as pltpu

F32 = jnp.float32
BF16 = jnp.bfloat16

LANES = 128
HEAD_DIM = 64
NEG = -1e30
BIG = 1e30
LOWEST = -3e38
EPS = 1e-6
ROPE_THETA = 10000.0
H_A = 8
H_B = 8
G_B = 2
H_C = 4
L_CMP = 32
CMP_STRIDE = 16
CMP_HIDDEN = 256
L_SEL = 64
TOP_N = 16
WINDOW = 512
N_BRANCH = 3
D_MODEL = 1024
W_A = H_A * HEAD_DIM
W_B = H_B * HEAD_DIM
W_C = H_C * 2 * HEAD_DIM
KV_B = G_B * HEAD_DIM
SPLIT_SIZES = (W_A, W_A, W_A, H_A, W_A,
               W_B, KV_B, KV_B, KV_B, KV_B, KV_B, KV_B, 3 * H_B, W_B,
               2 * H_C * HEAD_DIM, 2 * H_C * HEAD_DIM, W_C, W_C,
               N_BRANCH * D_MODEL)
SCALE = HEAD_DIM ** -0.5
LOG2E = math.log2(math.e)

VMEM_LIMIT = 48 * 1024 * 1024

TQ = 1024
CMP_TQ = 512
FLASH_ROWS = 512
PROJ_TM = 1024
POST_TM = 256


def _cparams(sem):
    return pltpu.CompilerParams(dimension_semantics=sem, vmem_limit_bytes=VMEM_LIMIT)


def _dot(a, b):
    return jnp.dot(a, b, preferred_element_type=F32)


def _dot_nt(a, b):
    return lax.dot_general(a, b, (((1,), (1,)), ((), ())), preferred_element_type=F32)


def _split3(x):
    hi = x.astype(BF16)
    r1 = x - hi.astype(F32)
    mid = r1.astype(BF16)
    lo = (r1 - mid.astype(F32)).astype(BF16)
    return hi, mid, lo


def _dot_exact01(x, m01):
    hi, mid, lo = _split3(x)
    return _dot(hi, m01) + _dot(mid, m01) + _dot(lo, m01)


def _rmsnorm_kernel(x_ref, g_ref, o_ref):
    x = x_ref[...]
    y = x * lax.rsqrt(jnp.mean(x * x, axis=-1, keepdims=True) + EPS)
    o_ref[...] = (y * g_ref[...]).astype(o_ref.dtype)


def _rmsnorm(x, g, out_dtype, tm=1024):
    T, D = x.shape
    return pl.pallas_call(
        _rmsnorm_kernel,
        grid=(T // tm,),
        in_specs=[pl.BlockSpec((tm, D), lambda i: (i, 0)),
                  pl.BlockSpec((1, D), lambda i: (0, 0))],
        out_specs=pl.BlockSpec((tm, D), lambda i: (i, 0)),
        out_shape=jax.ShapeDtypeStruct((T, D), out_dtype),
        compiler_params=_cparams(("parallel",)),
        name="rmsnorm",
    )(x, g.reshape(1, D))


def _rope_table_kernel(pos_ref, invf_ref, sign_ref, cos_ref, sin_ref):
    ang = pos_ref[...] * invf_ref[...]
    cos_ref[...] = jnp.cos(ang)
    sin_ref[...] = jnp.sin(ang) * sign_ref[...]


def _rope_tables(pos_f32, tm=512):
    T = pos_f32.shape[0]
    half = HEAD_DIM // 2
    inv_freq = ROPE_THETA ** (-jnp.arange(half, dtype=F32) / half)
    invf = jnp.tile(inv_freq, LANES // half).reshape(1, LANES)
    sign = jnp.tile(jnp.concatenate([-jnp.ones((half,), F32), jnp.ones((half,), F32)]),
                    LANES // HEAD_DIM).reshape(1, LANES)
    return pl.pallas_call(
        _rope_table_kernel,
        grid=(T // tm,),
        in_specs=[pl.BlockSpec((tm, 1), lambda i: (i, 0)),
                  pl.BlockSpec((1, LANES), lambda i: (0, 0)),
                  pl.BlockSpec((1, LANES), lambda i: (0, 0))],
        out_specs=[pl.BlockSpec((tm, LANES), lambda i: (i, 0)),
                   pl.BlockSpec((tm, LANES), lambda i: (i, 0))],
        out_shape=[jax.ShapeDtypeStruct((T, LANES), F32),
                   jax.ShapeDtypeStruct((T, LANES), F32)],
        compiler_params=_cparams(("parallel",)),
        name="rope_table",
    )(pos_f32.reshape(T, 1), invf, sign)


def _proj_kernel(h_ref, w_ref, *rest, rope):
    acc = _dot(h_ref[...], w_ref[...])
    if rope:
        cos_ref, sin_ref, o_ref = rest
        cos = cos_ref[...]
        sin = sin_ref[...]
        lane = lax.broadcasted_iota(jnp.int32, cos.shape, 1)
        first = (lane & (HEAD_DIM - 1)) < (HEAD_DIM // 2)
        for c in range(acc.shape[1] // LANES):
            t = acc[:, c * LANES:(c + 1) * LANES]
            sw = jnp.where(first, pltpu.roll(t, LANES - HEAD_DIM // 2, 1),
                           pltpu.roll(t, HEAD_DIM // 2, 1))
            o_ref[:, c * LANES:(c + 1) * LANES] = (t * cos + sw * sin).astype(o_ref.dtype)
    else:
        (o_ref,) = rest
        o_ref[...] = acc.astype(o_ref.dtype)


def _proj_f32_kernel(h_ref, w_ref, fg_ref, kv_ref):
    acc = _dot(h_ref[...], w_ref[...])
    fg_ref[...] = acc[:, :LANES]
    for c in range(2 * G_B):
        kv_ref[c] = acc[:, LANES + c * HEAD_DIM:LANES + (c + 1) * HEAD_DIM]


def _proj_f32(h, w):
    T, D = h.shape
    tm = min(PROJ_TM, T)
    return pl.pallas_call(
        _proj_f32_kernel,
        grid=(T // tm,),
        in_specs=[pl.BlockSpec((tm, D), lambda i: (i, 0)),
                  pl.BlockSpec(w.shape, lambda i: (0, 0))],
        out_specs=[pl.BlockSpec((tm, LANES), lambda i: (i, 0)),
                   pl.BlockSpec((2 * G_B, tm, HEAD_DIM), lambda i: (0, i, 0))],
        out_shape=[jax.ShapeDtypeStruct((T, LANES), F32),
                   jax.ShapeDtypeStruct((2 * G_B, T, HEAD_DIM), F32)],
        compiler_params=_cparams(("parallel",)),
        name="proj_f32",
    )(h, w)


def _proj(h, w, out_dtype, tn, rope_tabs=None, name="proj"):
    T, D = h.shape
    N = w.shape[1]
    tm = min(PROJ_TM, T)
    assert T % tm == 0 and N % tn == 0
    in_specs = [pl.BlockSpec((tm, D), lambda i, j: (i, 0)),
                pl.BlockSpec((D, tn), lambda i, j: (0, j))]
    args = [h, w]
    if rope_tabs is not None:
        in_specs += [pl.BlockSpec((tm, LANES), lambda i, j: (i, 0)),
                     pl.BlockSpec((tm, LANES), lambda i, j: (i, 0))]
        args += list(rope_tabs)
    return pl.pallas_call(
        functools.partial(_proj_kernel, rope=rope_tabs is not None),
        grid=(T // tm, N // tn),
        in_specs=in_specs,
        out_specs=pl.BlockSpec((tm, tn), lambda i, j: (i, j)),
        out_shape=jax.ShapeDtypeStruct((T, N), out_dtype),
        compiler_params=_cparams(("parallel", "arbitrary")),
        name=name,
    )(*args)


def _cumsum_kernel(fg_ref, b_ref, c_ref, carry_sc, *, tiles_per_seq):
    i = pl.program_id(0)

    @pl.when(i % tiles_per_seq == 0)
    def _():
        carry_sc[...] = jnp.zeros_like(carry_sc)

    z = fg_ref[...] + b_ref[...]
    logf = jnp.minimum(z, 0.0) - jnp.log1p(jnp.exp(-jnp.abs(z)))
    tm = z.shape[0]
    r = lax.broadcasted_iota(jnp.int32, (tm, tm), 0)
    c = lax.broadcasted_iota(jnp.int32, (tm, tm), 1)
    tri = jnp.where(c <= r, 1.0, 0.0).astype(BF16)
    cs = _dot_exact01_left(tri, logf) + carry_sc[...]
    carry_sc[...] = cs[tm - 1:tm, :]
    c_ref[0] = jnp.transpose(cs * LOG2E)[:H_A, :]


def _dot_exact01_left(m01, x):
    hi, mid, lo = _split3(x)
    return _dot(m01, hi) + _dot(m01, mid) + _dot(m01, lo)


def _forget_cumsum(fg, b_pad, B, S, tm=256):
    tps = S // tm
    return pl.pallas_call(
        functools.partial(_cumsum_kernel, tiles_per_seq=tps),
        grid=(B * tps,),
        in_specs=[pl.BlockSpec((tm, LANES), lambda i: (i, 0)),
                  pl.BlockSpec((1, LANES), lambda i: (0, 0))],
        out_specs=pl.BlockSpec((1, H_A, tm), lambda i: (i // tps, 0, i % tps)),
        out_shape=jax.ShapeDtypeStruct((B, H_A, S), F32),
        scratch_shapes=[pltpu.VMEM((1, LANES), F32)],
        compiler_params=_cparams(("arbitrary",)),
        name="forget_cumsum",
    )(fg, b_pad)


def _flash_kernel(*refs, mode, tq, tw, lam_init):
    it = iter(refs)
    q_ref = next(it)
    mm_ref = next(it) if mode == "sel" else None
    k_ref = next(it)
    v_ref = next(it)
    if mode == "fox":
        ck_ref = next(it)
    if mode == "diff":
        lam_ref = next(it)
        g_ref = next(it)
    o_ref = next(it)
    m_sc = next(it)
    acc_sc = next(it)
    v1_sc = next(it)
    s0_sc = next(it)
    s1_sc = next(it)
    s2_sc = next(it)
    if mode == "sel":
        ke_sc = next(it)

    qi = pl.program_id(2)
    S = k_ref.shape[0]
    lane = lax.broadcasted_iota(jnp.int32, (tq, LANES), 1)
    half0 = lane < HEAD_DIM

    @pl.when(qi == 0)
    def _():
        v1_sc[:, :LANES] = v_ref[...]
        v1_sc[:, LANES:] = jnp.ones((S, LANES), BF16)
        if mode == "sel":
            row = lax.broadcasted_iota(jnp.int32, (S, LANES), 0)
            ln = lax.broadcasted_iota(jnp.int32, (S, LANES), 1)
            blk = lax.shift_right_logical(row, int(math.log2(L_SEL)))
            ke_sc[:, :LANES] = k_ref[...]
            ke_sc[:, LANES:] = jnp.where(ln == blk, BIG, 0.0).astype(BF16)

    kk = ke_sc if mode == "sel" else k_ref

    def split_halves(qr, mr):
        q2 = qr[...].astype(F32)
        zero = jnp.zeros_like(q2)
        qa = jnp.where(half0, q2, zero).astype(BF16)
        qb = jnp.where(half0, zero, q2).astype(BF16)
        if mode == "sel":
            mmh = jnp.where(half0, mr[...].astype(F32), zero).astype(BF16)
            qa = jnp.concatenate([qa, mmh], axis=1)
            qb = jnp.concatenate([qb, mmh], axis=1)
        return qa, qb

    qm = split_halves(q_ref, mm_ref)

    m_sc[...] = jnp.full_like(m_sc, NEG)
    acc_sc[...] = jnp.zeros_like(acc_sc)
    q0 = qi * tq
    rb = LANES
    nrb = tq // rb
    nch = tw // LANES
    hrb = nrb // 2
    dsq = (lax.broadcasted_iota(jnp.int32, (rb, LANES), 0)
           - lax.broadcasted_iota(jnp.int32, (rb, LANES), 1))

    def qk(start, s_sc, r_lo=0, r_hi=tq):
        koff = pl.multiple_of(start, tw)
        kt = kk[pl.ds(koff, tw), :]
        for a in range(2):
            s = _dot_nt(qm[a][r_lo:r_hi], kt)
            if mode == "fox":
                s = s - ck_ref[0, 0, a:a + 1, pl.ds(koff, tw)]
            s_sc[a, r_lo:r_hi] = s

    FULL = [("full",)] * nrb
    DIAG_A = [("diag", r) for r in range(hrb)] + [("full",)] * hrb
    DIAG_B = [None] * hrb + [("diag", r) for r in range(hrb)]
    PREV_P = [("prev", r) for r in range(hrb)] + [None] * hrb
    WIN_A = [("diag", r) for r in range(hrb)] + [("prev", r) for r in range(hrb)]

    def soft_pv(s_sc, start, kinds):
        koff = pl.multiple_of(start, tw)
        vt = v1_sc[pl.ds(koff, tw), :]
        chains = []
        i = 0
        while i < nrb:
            j = i + 1
            if kinds[i] == ("full",):
                while j < nrb and kinds[j] == ("full",) and (j - i) * rb < FLASH_ROWS:
                    j += 1
            if kinds[i] is not None:
                chains.append((slice(i * rb, j * rb), kinds[i]))
            i = j
        for a in range(2):
            for rows, kind in chains:
                if kind[0] == "full":
                    c_lo, c_hi, c_edge = 0, nch, None
                elif kind[0] == "diag":
                    c_lo, c_hi, c_edge = 0, kind[1] + 1, kind[1]
                else:
                    c_lo, c_hi, c_edge = kind[1], nch, kind[1]
                ch = [s_sc[a, rows, c * LANES:(c + 1) * LANES] for c in range(c_lo, c_hi)]
                if c_edge is not None:
                    ok = dsq >= 0 if kind[0] == "diag" else dsq < 0
                    ch[c_edge - c_lo] = jnp.where(ok, ch[c_edge - c_lo], NEG)
                mx = functools.reduce(jnp.maximum, ch)
                m_old = m_sc[a, rows]
                m_new = jnp.maximum(m_old, jnp.max(mx, axis=1, keepdims=True))
                alpha = jnp.exp2(m_old - m_new)
                p = jnp.concatenate([jnp.exp2(c_ - m_new).astype(BF16) for c_ in ch], axis=1)
                m_sc[a, rows] = m_new
                al = jnp.concatenate([alpha, alpha], axis=1)
                acc_sc[a, rows] = (al * acc_sc[a, rows]
                                   + _dot(p, vt[c_lo * LANES:c_hi * LANES]))

    if mode == "win":
        @pl.when(qi > 0)
        def _():
            qk(q0 - tw, s0_sc, 0, tq // 2)
            qk(q0, s1_sc)
            qk(q0 + tw, s2_sc, tq // 2, tq)
            soft_pv(s0_sc, q0 - tw, PREV_P)
            soft_pv(s1_sc, q0, WIN_A)
            soft_pv(s2_sc, q0 + tw, DIAG_B)

        @pl.when(qi == 0)
        def _():
            qk(q0, s1_sc)
            qk(q0 + tw, s2_sc, tq // 2, tq)
            soft_pv(s1_sc, q0, WIN_A)
            soft_pv(s2_sc, q0 + tw, DIAG_B)
    else:
        qk(0, s0_sc)

        def pair_body(i, carry):
            t0 = 2 * i * tw
            qk(t0 + tw, s1_sc)
            soft_pv(s0_sc, t0, FULL)
            qk(t0 + 2 * tw, s0_sc)
            soft_pv(s1_sc, t0 + tw, FULL)
            return carry

        lax.fori_loop(0, qi, pair_body, 0)
        qk(q0 + tw, s1_sc, tq // 2, tq)
        soft_pv(s0_sc, q0, DIAG_A)
        soft_pv(s1_sc, q0 + tw, DIAG_B)

    o0 = acc_sc[0, :, :LANES] / acc_sc[0, :, LANES:]
    o1 = acc_sc[1, :, :LANES] / acc_sc[1, :, LANES:]
    if mode == "diff":
        lq = lam_ref[...]
        lam = (jnp.exp(jnp.sum(lq[0:1] * lq[1:2], keepdims=True))
               - jnp.exp(jnp.sum(lq[2:3] * lq[3:4], keepdims=True)) + lam_init)
        d = o0 - lam * o1
        y = d * lax.rsqrt(jnp.mean(d * d, axis=-1, keepdims=True) + EPS)
        o_ref[...] = (y * g_ref[...]) * (1.0 - lam_init)
    else:
        o_ref[...] = jnp.where(half0, o0, o1)


def _flash(mode, B, S, n_blk, q, q_blk0, k, k_blk, v, v_blk, *,
           mm=None, ck=None, lam=None, subln_g=None, lam_init=0.0):
    tq = min(TQ, S)
    tw = WINDOW
    nq = S // tq
    T = B * S
    assert S % tq == 0 and tq == 2 * tw and tw % LANES == 0
    in_specs = [pl.BlockSpec((tq, LANES), lambda b, h, i: (b * nq + i, q_blk0 + h))]
    args = [q]
    if mode == "sel":
        in_specs.append(pl.BlockSpec((tq, LANES), lambda b, h, i: (b * nq + i, h // 2)))
        args.append(mm)
    in_specs.append(pl.BlockSpec((S, LANES), lambda b, h, i: (b, k_blk(h))))
    args.append(k)
    in_specs.append(pl.BlockSpec((S, LANES), lambda b, h, i: (b, v_blk(h))))
    args.append(v)
    if mode == "fox":
        in_specs.append(pl.BlockSpec((1, 1, 2, S), lambda b, h, i: (b, h, 0, 0)))
        args += [ck]
    if mode == "diff":
        in_specs.append(pl.BlockSpec(lam.shape, lambda b, h, i: (0, 0)))
        in_specs.append(pl.BlockSpec((1, LANES), lambda b, h, i: (0, 0)))
        args += [lam, subln_g]
    scratch = [pltpu.VMEM((2, tq, LANES), F32), pltpu.VMEM((2, tq, 2 * LANES), F32),
               pltpu.VMEM((S, 2 * LANES), BF16)]
    scratch += [pltpu.VMEM((2, tq, tw), F32)] * 3
    if mode == "sel":
        scratch += [pltpu.VMEM((S, 2 * LANES), BF16)]
    return pl.pallas_call(
        functools.partial(_flash_kernel, mode=mode, tq=tq, tw=tw, lam_init=lam_init),
        grid=(B, n_blk, nq),
        in_specs=in_specs,
        out_specs=pl.BlockSpec((tq, LANES), lambda b, h, i: (b * nq + i, h)),
        out_shape=jax.ShapeDtypeStruct((T, n_blk * LANES), F32),
        scratch_shapes=scratch,
        compiler_params=_cparams(("parallel", "parallel", "arbitrary")),
        name="flash_" + mode,
    )(*args)


def _compress_kernel(t_ref, pe_ref, w1_ref, b1_ref, w2_ref, o_ref):
    t = t_ref[0, 0]
    R, half = t.shape
    pe = pe_ref[0]
    xa = (t + pe[:, :half]).astype(BF16)
    xb = (t + pe[:, half:]).astype(BF16)
    a = _dot(xa, w1_ref[0, :half, :])
    b = _dot(xb, w1_ref[0, half:, :])
    hp = a + pltpu.roll(b, R - 1, 0) + b1_ref[0]
    hid = hp * jax.nn.sigmoid(hp)
    o_ref[0, 0] = _dot(hid.astype(BF16), w2_ref[0]).astype(o_ref.dtype)


def _compress(tkv, pe, w1, b1, w2dup):
    _, B, R, W = tkv.shape
    G = G_B
    return pl.pallas_call(
        _compress_kernel,
        grid=(2, B, G),
        in_specs=[pl.BlockSpec((1, 1, R, W), lambda s, b, g: (s * G_B + g, b, 0, 0)),
                  pl.BlockSpec((1, 1, 2 * W), lambda s, b, g: (s, 0, 0)),
                  pl.BlockSpec((1, 2 * W, CMP_HIDDEN), lambda s, b, g: (s, 0, 0)),
                  pl.BlockSpec((1, 1, CMP_HIDDEN), lambda s, b, g: (s, 0, 0)),
                  pl.BlockSpec((1, CMP_HIDDEN, LANES), lambda s, b, g: (s, 0, 0))],
        out_specs=pl.BlockSpec((1, 1, R, LANES), lambda s, b, g: (s, b, 0, g)),
        out_shape=jax.ShapeDtypeStruct((2, B, R, G * LANES), BF16),
        compiler_params=_cparams(("parallel", "parallel", "parallel")),
        name="nsa_compress",
    )(tkv, pe, w1, b1, w2dup)


def _cmp_kernel(q_ref, kc_ref, vc_ref, cov_ref, o_ref, mm_ref, *, tq, n_sel, top_n):
    qi = pl.program_id(2)
    kc = kc_ref[0, 0]
    vc = vc_ref[0, 0]
    R = kc.shape[0]
    lane = lax.broadcasted_iota(jnp.int32, (tq, LANES), 1)
    half0 = lane < HEAD_DIM
    t_r = lax.broadcasted_iota(jnp.int32, (tq, R), 0) + qi * tq
    c_end = lax.broadcasted_iota(jnp.int32, (tq, R), 1) * CMP_STRIDE + (L_CMP - 1)
    ok = c_end <= t_r
    psum = jnp.zeros((tq, R), F32)
    outs = []
    for pr in range(2):
        q2 = q_ref[:, pr * LANES:(pr + 1) * LANES].astype(F32)
        zero = jnp.zeros_like(q2)
        pair = []
        for a in range(2):
            qm = (jnp.where(half0, q2, zero) if a == 0 else jnp.where(half0, zero, q2)).astype(BF16)
            s = jnp.where(ok, _dot_nt(qm, kc), NEG)
            m = jnp.max(s, axis=1, keepdims=True)
            e = jnp.where(ok, jnp.exp(s - m), 0.0)
            l = jnp.sum(e, axis=1, keepdims=True)
            p = e / jnp.where(l > 0.0, l, 1.0)
            psum = psum + p
            pair.append(_dot(p.astype(BF16), vc))
        outs.append(jnp.where(half0, pair[0], pair[1]))
    o_ref[...] = jnp.concatenate(outs, axis=1)

    imp = _dot_exact01(psum, cov_ref[...])
    jl = lane & (HEAD_DIM - 1)
    t_q = lax.broadcasted_iota(jnp.int32, (tq, LANES), 0) + qi * tq
    tb = lax.shift_right_logical(t_q, int(math.log2(L_SEL)))
    valid = jl <= tb
    forced = (jl == 0) | (jl == tb) | (jl == tb - 1)
    score = jnp.where(forced, LOWEST, jnp.where(valid, imp, -BIG))
    score = jnp.where(half0 & (jl < n_sel), score, LOWEST)
    mm = jnp.where(forced, 0.0, -1.0)
    for _ in range(top_n - 3):
        idx = jnp.argmax(score, axis=1, keepdims=True).astype(jnp.int32)
        mm = jnp.where(jl == idx, 0.0, mm)
        score = jnp.where(lane == idx, LOWEST, score)
    mm_ref[...] = mm.astype(mm_ref.dtype)


def _cmp_topk(B, S, q, q_blk0, ckv, cover_dup):
    tq = min(CMP_TQ, S)
    nq = S // tq
    T = B * S
    R = ckv.shape[2]
    n_sel = S // L_SEL
    gw = 2 * LANES
    return pl.pallas_call(
        functools.partial(_cmp_kernel, tq=tq, n_sel=n_sel, top_n=min(TOP_N, n_sel)),
        grid=(B, G_B, nq),
        in_specs=[pl.BlockSpec((tq, gw), lambda b, g, i: (b * nq + i, q_blk0 // 2 + g)),
                  pl.BlockSpec((1, 1, R, LANES), lambda b, g, i: (0, b, 0, g)),
                  pl.BlockSpec((1, 1, R, LANES), lambda b, g, i: (1, b, 0, g)),
                  pl.BlockSpec((R, LANES), lambda b, g, i: (0, 0))],
        out_specs=[pl.BlockSpec((tq, gw), lambda b, g, i: (b * nq + i, g)),
                   pl.BlockSpec((tq, LANES), lambda b, g, i: (b * nq + i, g))],
        out_shape=[jax.ShapeDtypeStruct((T, G_B * gw), F32),
                   jax.ShapeDtypeStruct((T, G_B * LANES), BF16)],
        compiler_params=_cparams(("parallel", "parallel", "parallel")),
        name="nsa_cmp_topk",
    )(q, ckv, ckv, cover_dup)


def _post_kernel(aa_ref, oc_ref, os_ref, ow_ref, ac_ref, fg_ref, h_ref, x_ref, p_ref,
                 wz_ref, wa_ref, wb_ref, wc_ref, wo_ref, wg_ref, wp_ref, gn_ref,
                 *outs, last):
    def silu(z):
        return z * jax.nn.sigmoid(z)

    W = W_B
    D = x_ref.shape[1]
    mz = _dot(h_ref[...], wz_ref[...])
    za = mz[:, 3 * D:3 * D + W]
    zb = mz[:, 3 * D + W:3 * D + 2 * W]
    zc = mz[:, 3 * D + 2 * W:]
    sg = jax.nn.sigmoid(fg_ref[...])
    head = lax.shift_right_logical(lax.broadcasted_iota(jnp.int32, sg.shape, 1), int(math.log2(HEAD_DIM)))
    mix = []
    for j in range(W // LANES):
        cols = slice(j * LANES, (j + 1) * LANES)
        src = H_A + N_BRANCH * (2 * j + head)
        g = [jnp.take_along_axis(sg, src + r, axis=1) for r in range(N_BRANCH)]
        mix.append(g[0] * oc_ref[:, cols] + g[1] * os_ref[:, cols] + g[2] * ow_ref[:, cols])
    ya = aa_ref[...] * silu(za)
    yb = jnp.concatenate(mix, axis=1) * silu(zb)
    yc = ac_ref[...] * silu(zc)
    pa = _dot(ya.astype(BF16), wa_ref[...])
    pb = _dot(yb.astype(BF16), wb_ref[...])
    pc = _dot(yc.astype(BF16), wc_ref[...])
    merged = (jax.nn.sigmoid(mz[:, :D]) * pa + jax.nn.sigmoid(mz[:, D:2 * D]) * pb
              + jax.nn.sigmoid(mz[:, 2 * D:3 * D]) * pc)
    x1 = x_ref[...] + _dot(merged.astype(BF16), wo_ref[...])
    gate = jax.nn.sigmoid(_dot(x1.astype(BF16), wg_ref[...]))
    x2 = x1 + gate * _dot(p_ref[...].astype(BF16), wp_ref[...])
    y = x2 * lax.rsqrt(jnp.mean(x2 * x2, axis=-1, keepdims=True) + EPS) * gn_ref[...]
    if last:
        outs[0][...] = y
    else:
        outs[0][...] = x2
        outs[1][...] = y.astype(BF16)


def _post(aa, oc, osel, ow, ac, fg, h, x, p, layer, wz, wa, wb, wc, wo, wg, wp, g_next, last):
    T, D = x.shape
    tm = min(POST_TM, T)
    W = W_A
    row = lambda i: (i, 0)
    const = lambda i: (0, 0)

    def resident(arr):
        return pl.BlockSpec(arr.shape, const, pipeline_mode=pl.Buffered(1))

    in_specs = [pl.BlockSpec((tm, W), row)] * 5 + [
        pl.BlockSpec((tm, LANES), row),
        pl.BlockSpec((tm, D), row),
        pl.BlockSpec((tm, D), row),
        pl.BlockSpec((tm, p.shape[1]), lambda i: (i + layer * (T // tm), 0)),
        resident(wz), resident(wa), resident(wb), resident(wc), resident(wo), resident(wg),
        resident(wp), pl.BlockSpec((1, D), const)]
    if last:
        out_specs = [pl.BlockSpec((tm, D), row)]
        out_shape = [jax.ShapeDtypeStruct((T, D), F32)]
    else:
        out_specs = [pl.BlockSpec((tm, D), row), pl.BlockSpec((tm, D), row)]
        out_shape = [jax.ShapeDtypeStruct((T, D), F32), jax.ShapeDtypeStruct((T, D), BF16)]
    return pl.pallas_call(
        functools.partial(_post_kernel, last=last),
        grid=(T // tm,),
        in_specs=in_specs,
        out_specs=out_specs,
        out_shape=out_shape,
        compiler_params=_cparams(("parallel",)),
        name="post",
    )(aa, oc, osel, ow, ac, fg, h, x, p, wz, wa, wb, wc, wo, wg, wp, g_next)


def _cover_matrix(R, n_sel):
    c_start = np.arange(R)[:, None] * CMP_STRIDE
    j_start = np.arange(HEAD_DIM)[None, :] * L_SEL
    cov = ((c_start < j_start + L_SEL) & (c_start + L_CMP > j_start)
           & (np.arange(HEAD_DIM)[None, :] < n_sel)).astype(np.float32)
    return jnp.asarray(np.concatenate([cov, cov], axis=1), BF16)


def _layer_weights(w_in_i):
    offs = np.cumsum((0,) + SPLIT_SIZES)
    (QA, KA, VA, FA, ZA, QB, KCB, VCB, KSB, VSB, KWB, VWB, GB, ZB, QC, KC, VC, ZC, MG) = range(len(SPLIT_SIZES))
    D = w_in_i.shape[0]
    qs = SCALE * LOG2E

    col_scale = np.ones((offs[-1],), np.float32)
    col_scale[offs[QA]:offs[QA + 1]] = qs
    col_scale[offs[QC]:offs[QC + 1]] = qs
    wb = (w_in_i * col_scale).astype(BF16)

    def seg(k):
        return wb[:, offs[k]:offs[k + 1]]

    def dup(w):
        return [w[:, :HEAD_DIM], w[:, :HEAD_DIM], w[:, HEAD_DIM:], w[:, HEAD_DIM:]]

    qb_rot = (w_in_i[:, offs[QB]:offs[QB + 1]] * qs).astype(BF16)
    qb_plain = seg(QB) * SCALE
    w1 = jnp.concatenate([seg(QA), seg(KA), seg(VA), qb_plain, seg(VC)] + dup(seg(VSB)) + dup(seg(VWB)), axis=1)
    w2 = jnp.concatenate([qb_rot, seg(QC), seg(KC)] + dup(seg(KWB)) + dup(seg(KSB)), axis=1)
    pad = jnp.zeros((D, LANES - H_A - 3 * H_B), BF16)
    w3 = jnp.concatenate([seg(FA), seg(GB), pad, seg(KCB), seg(VCB)], axis=1)
    wz = jnp.concatenate([seg(MG), seg(ZA), seg(ZB), seg(ZC)], axis=1)
    return w1, w2, w3, wz


O1_QA, O1_KA, O1_VA, O1_QBU, O1_VC, O1_VS, O1_VW = 0, 4, 8, 12, 16, 20, 22
O2_QBR, O2_QC, O2_KC, O2_KW, O2_KS = 0, 4, 8, 12, 14


def kernel(x, p, positions, norm_g, w_in, b_forget, cmp_pe_k, cmp_w1_k, cmp_b1_k, cmp_w2_k,
           cmp_pe_v, cmp_w1_v, cmp_b1_v, cmp_w2_v, diff_lam, diff_subln_g,
           w_br_a, w_br_b, w_br_c, w_out, w_ple, w_ple_gate, final_g):
    B, S, D = x.shape
    depth = w_in.shape[0]
    T = B * S
    R = S // CMP_STRIDE
    n_sel = S // L_SEL
    assert n_sel <= HEAD_DIM and S % min(TQ, S) == 0

    xf = x.reshape(T, D)
    cos_t, sin_t = _rope_tables(positions.astype(F32).reshape(T))
    cover = _cover_matrix(R, n_sel)
    h = _rmsnorm(xf, norm_g[0], BF16)

    for i in range(depth):
        w1, w2, w3, wz = _layer_weights(w_in[i])
        o1 = _proj(h, w1, BF16, 1024, name="proj_plain")
        o2 = _proj(h, w2, BF16, 1024, rope_tabs=(cos_t, sin_t), name="proj_rope")
        fg, kv4 = _proj_f32(h, w3)

        b_pad = jnp.pad(b_forget[i], (0, LANES - H_A)).reshape(1, LANES)
        ck = _forget_cumsum(fg, b_pad, B, S).reshape(B, H_A // 2, 2, S)
        att_a = _flash("fox", B, S, H_A // 2, o1, O1_QA, o1, lambda h_: O1_KA + h_,
                       o1, lambda h_: O1_VA + h_, ck=ck)

        lam_init = 0.8 - 0.6 * math.exp(-0.3 * i)
        att_c = _flash("diff", B, S, H_C, o2, O2_QC, o2, lambda h_: O2_KC + h_,
                       o1, lambda h_: O1_VC + h_, lam=diff_lam[i],
                       subln_g=diff_subln_g[i].reshape(1, LANES), lam_init=lam_init)

        tkv = kv4.reshape(2 * G_B, B, R, CMP_STRIDE * HEAD_DIM)
        pe =jnp.stack([cmp_pe_k[i].reshape(1, -1), cmp_pe_v[i].reshape(1, -1)])
        cw1 = jnp.stack([cmp_w1_k[i], cmp_w1_v[i]]).astype(BF16)
        cb1 = jnp.stack([cmp_b1_k[i].reshape(1, -1), cmp_b1_v[i].reshape(1, -1)])
        cw2 = jnp.stack([cmp_w2_k[i], cmp_w2_v[i]])
        cw2 = jnp.concatenate([cw2, cw2], axis=-1).astype(BF16)
        ckv = _compress(tkv, pe, cw1, cb1, cw2)
        o_cmp, mm = _cmp_topk(B, S, o1, O1_QBU, ckv, cover)
        o_sel = _flash("sel", B, S, H_B // 2, o2, O2_QBR, o2, lambda h_: O2_KS + h_ // 2,
                       o1, lambda h_: O1_VS + h_ // 2, mm=mm)
        o_win = _flash("win", B, S, H_B // 2, o2, O2_QBR, o2, lambda h_: O2_KW + h_ // 2,
                       o1, lambda h_: O1_VW + h_ // 2)

        last = i == depth - 1
        g_next = (final_g if last else norm_g[i + 1]).reshape(1, D)
        res = _post(att_a, o_cmp, o_sel, o_win, att_c, fg, h, xf, p.reshape(depth * T, -1), i,
                    wz, w_br_a[i].astype(BF16), w_br_b[i].astype(BF16), w_br_c[i].astype(BF16),
                    w_out[i].astype(BF16), w_ple_gate[i].astype(BF16), w_ple[i].astype(BF16),
                    g_next, last)
        if last:
            return res[0].reshape(B, S, D)
        xf, h = res
```

```python
import functools
import math

import numpy as np
import jax
import jax.numpy as jnp
from jax import lax
from jax.experimental import pallas as pl
from jax.experimental.pallas import tpu as pltpu

F32 = jnp.float32
BF16 = jnp.bfloat16

LANES = 128
HEAD_DIM = 64
NEG = -1e30
BIG = 1e30
LOWEST = -2.0 ** 127
MARKED = -2.0 ** 120
EPS = 1e-6
ROPE_THETA = 10000.0
H_A = 8
H_B = 8
G_B = 2
H_C = 4
L_CMP = 32
CMP_STRIDE = 16
CMP_HIDDEN = 256
L_SEL = 64
TOP_N = 16
WINDOW = 512
N_BRANCH = 3
D_MODEL = 1024
W_A = H_A * HEAD_DIM
W_B = H_B * HEAD_DIM
W_C = H_C * 2 * HEAD_DIM
KV_B = G_B * HEAD_DIM
SPLIT_SIZES = (W_A, W_A, W_A, H_A, W_A,
               W_B, KV_B, KV_B, KV_B, KV_B, KV_B, KV_B, 3 * H_B, W_B,
               2 * H_C * HEAD_DIM, 2 * H_C * HEAD_DIM, W_C, W_C,
               N_BRANCH * D_MODEL)
SCALE = HEAD_DIM ** -0.5
LOG2E = math.log2(math.e)

VMEM_LIMIT = 48 * 1024 * 1024

TQ = 1024
CMP_TQ = 512
FLASH_ROWS = 512
PROJ_TM = 1024
POST_TM = 256


def _cparams(sem):
    return pltpu.CompilerParams(dimension_semantics=sem, vmem_limit_bytes=VMEM_LIMIT)


def _dot(a, b):
    return jnp.dot(a, b, preferred_element_type=F32)


def _dot_nt(a, b):
    return lax.dot_general(a, b, (((1,), (1,)), ((), ())), preferred_element_type=F32)


def _split3(x):
    hi = x.astype(BF16)
    r1 = x - hi.astype(F32)
    mid = r1.astype(BF16)
    lo = (r1 - mid.astype(F32)).astype(BF16)
    return hi, mid, lo


def _dot_exact01(x, m01):
    hi, mid, lo = _split3(x)
    return _dot(hi, m01) + _dot(mid, m01) + _dot(lo, m01)


def _rmsnorm_kernel(x_ref, g_ref, o_ref):
    x = x_ref[...]
    y = x * lax.rsqrt(jnp.mean(x * x, axis=-1, keepdims=True) + EPS)
    o_ref[...] = (y * g_ref[...]).astype(o_ref.dtype)


def _rmsnorm(x, g, out_dtype, tm=1024):
    T, D = x.shape
    return pl.pallas_call(
        _rmsnorm_kernel,
        grid=(T // tm,),
        in_specs=[pl.BlockSpec((tm, D), lambda i: (i, 0)),
                  pl.BlockSpec((1, D), lambda i: (0, 0))],
        out_specs=pl.BlockSpec((tm, D), lambda i: (i, 0)),
        out_shape=jax.ShapeDtypeStruct((T, D), out_dtype),
        compiler_params=_cparams(("parallel",)),
        name="rmsnorm",
    )(x, g.reshape(1, D))


def _rope_table_kernel(pos_ref, invf_ref, sign_ref, cos_ref, sin_ref):
    ang = pos_ref[...] * invf_ref[...]
    cos_ref[...] = jnp.cos(ang)
    sin_ref[...] = jnp.sin(ang) * sign_ref[...]


def _rope_tables(pos_f32, tm=512):
    T = pos_f32.shape[0]
    half = HEAD_DIM // 2
    inv_freq = ROPE_THETA ** (-jnp.arange(half, dtype=F32) / half)
    invf = jnp.tile(inv_freq, LANES // half).reshape(1, LANES)
    sign = jnp.tile(jnp.concatenate([-jnp.ones((half,), F32), jnp.ones((half,), F32)]),
                    LANES // HEAD_DIM).reshape(1, LANES)
    return pl.pallas_call(
        _rope_table_kernel,
        grid=(T // tm,),
        in_specs=[pl.BlockSpec((tm, 1), lambda i: (i, 0)),
                  pl.BlockSpec((1, LANES), lambda i: (0, 0)),
                  pl.BlockSpec((1, LANES), lambda i: (0, 0))],
        out_specs=[pl.BlockSpec((tm, LANES), lambda i: (i, 0)),
                   pl.BlockSpec((tm, LANES), lambda i: (i, 0))],
        out_shape=[jax.ShapeDtypeStruct((T, LANES), F32),
                   jax.ShapeDtypeStruct((T, LANES), F32)],
        compiler_params=_cparams(("parallel",)),
        name="rope_table",
    )(pos_f32.reshape(T, 1), invf, sign)


def _proj_kernel(h_ref, w_ref, *rest, rope):
    acc = _dot(h_ref[...], w_ref[...])
    if rope:
        cos_ref, sin_ref, o_ref = rest
        cos = cos_ref[...]
        sin = sin_ref[...]
        lane = lax.broadcasted_iota(jnp.int32, cos.shape, 1)
        first = (lane & (HEAD_DIM - 1)) < (HEAD_DIM // 2)
        for c in range(acc.shape[1] // LANES):
            t = acc[:, c * LANES:(c + 1) * LANES]
            sw = jnp.where(first, pltpu.roll(t, LANES - HEAD_DIM // 2, 1),
                           pltpu.roll(t, HEAD_DIM // 2, 1))
            o_ref[:, c * LANES:(c + 1) * LANES] = (t * cos + sw * sin).astype(o_ref.dtype)
    else:
        (o_ref,) = rest
        o_ref[...] = acc.astype(o_ref.dtype)


def _proj_f32_kernel(h_ref, w_ref, fg_ref, kv_ref):
    acc = _dot(h_ref[...], w_ref[...])
    fg_ref[...] = acc[:, :LANES]
    for c in range(2 * G_B):
        kv_ref[c] = acc[:, LANES + c * HEAD_DIM:LANES + (c + 1) * HEAD_DIM]


def _proj_f32(h, w, layer):
    T, D = h.shape
    tm = min(PROJ_TM, T)
    return pl.pallas_call(
        _proj_f32_kernel,
        grid=(T // tm,),
        in_specs=[pl.BlockSpec((tm, D), lambda i: (i, 0)),
                  pl.BlockSpec((None,) + w.shape[1:], lambda i: (layer, 0, 0))],
        out_specs=[pl.BlockSpec((tm, LANES), lambda i: (i, 0)),
                   pl.BlockSpec((2 * G_B, tm, HEAD_DIM), lambda i: (0, i, 0))],
        out_shape=[jax.ShapeDtypeStruct((T, LANES), F32),
                   jax.ShapeDtypeStruct((2 * G_B, T, HEAD_DIM), F32)],
        compiler_params=_cparams(("parallel",)),
        name="proj_f32",
    )(h, w)


def _proj(h, w, layer, out_dtype, tn, rope_tabs=None, name="proj"):
    T, D = h.shape
    N = w.shape[2]
    tm = min(PROJ_TM, T)
    assert T % tm == 0 and N % tn == 0
    in_specs = [pl.BlockSpec((tm, D), lambda i, j: (i, 0)),
                pl.BlockSpec((None, D, tn), lambda i, j: (layer, 0, j))]
    args = [h, w]
    if rope_tabs is not None:
        in_specs += [pl.BlockSpec((tm, LANES), lambda i, j: (i, 0)),
                     pl.BlockSpec((tm, LANES), lambda i, j: (i, 0))]
        args += list(rope_tabs)
    return pl.pallas_call(
        functools.partial(_proj_kernel, rope=rope_tabs is not None),
        grid=(T // tm, N // tn),
        in_specs=in_specs,
        out_specs=pl.BlockSpec((tm, tn), lambda i, j: (i, j)),
        out_shape=jax.ShapeDtypeStruct((T, N), out_dtype),
        compiler_params=_cparams(("parallel", "arbitrary")),
        name=name,
    )(*args)


def _cumsum_kernel(fg_ref, b_ref, c_ref, carry_sc, *, tiles_per_seq):
    i = pl.program_id(0)

    @pl.when(i % tiles_per_seq == 0)
    def _():
        carry_sc[...] = jnp.zeros_like(carry_sc)

    z = fg_ref[...] + b_ref[...]
    logf = jnp.minimum(z, 0.0) - jnp.log1p(jnp.exp(-jnp.abs(z)))
    tm = z.shape[0]
    r = lax.broadcasted_iota(jnp.int32, (tm, tm), 0)
    c = lax.broadcasted_iota(jnp.int32, (tm, tm), 1)
    tri = jnp.where(c <= r, 1.0, 0.0).astype(BF16)
    cs = _dot_exact01_left(tri, logf) + carry_sc[...]
    carry_sc[...] = cs[tm - 1:tm, :]
    c_ref[0] = jnp.transpose(cs * LOG2E)[:H_A, :]


def _dot_exact01_left(m01, x):
    hi, mid, lo = _split3(x)
    return _dot(m01, hi) + _dot(m01, mid) + _dot(m01, lo)


def _forget_cumsum(fg, b_pad, B, S, tm=256):
    tps = S // tm
    return pl.pallas_call(
        functools.partial(_cumsum_kernel, tiles_per_seq=tps),
        grid=(B * tps,),
        in_specs=[pl.BlockSpec((tm, LANES), lambda i: (i, 0)),
                  pl.BlockSpec((1, LANES), lambda i: (0, 0))],
        out_specs=pl.BlockSpec((1, H_A, tm), lambda i: (i // tps, 0, i % tps)),
        out_shape=jax.ShapeDtypeStruct((B, H_A, S), F32),
        scratch_shapes=[pltpu.VMEM((1, LANES), F32)],
        compiler_params=_cparams(("arbitrary",)),
        name="forget_cumsum",
    )(fg, b_pad)


def _flash_kernel(*refs, mode, tq, tw, lam_init):
    it = iter(refs)
    q_ref = next(it)
    mm_ref = next(it) if mode == "sel" else None
    k_ref = next(it)
    v_ref = next(it)
    if mode == "fox":
        ck_ref = next(it)
    if mode == "diff":
        lam_ref = next(it)
        g_ref = next(it)
    o_ref = next(it)
    m_sc = next(it)
    acc_sc = next(it)
    v1_sc = next(it)
    s0_sc = next(it)
    s1_sc = next(it)
    s2_sc = next(it)
    if mode == "sel":
        ke_sc = next(it)

    qi = pl.program_id(2)
    S = k_ref.shape[0]
    lane = lax.broadcasted_iota(jnp.int32, (tq, LANES), 1)
    half0 = lane < HEAD_DIM

    @pl.when(qi == 0)
    def _():
        v1_sc[:, :LANES] = v_ref[...]
        v1_sc[:, LANES:] = jnp.ones((S, LANES), BF16)
        if mode == "sel":
            row = lax.broadcasted_iota(jnp.int32, (S, LANES), 0)
            ln = lax.broadcasted_iota(jnp.int32, (S, LANES), 1)
            blk = lax.shift_right_logical(row, int(math.log2(L_SEL)))
            ke_sc[:, :LANES] = k_ref[...]
            ke_sc[:, LANES:] = jnp.where(ln == blk, BIG, 0.0).astype(BF16)

    kk = ke_sc if mode == "sel" else k_ref

    def split_halves(qr, mr):
        q2 = qr[...].astype(F32)
        zero = jnp.zeros_like(q2)
        qa = jnp.where(half0, q2, zero).astype(BF16)
        qb = jnp.where(half0, zero, q2).astype(BF16)
        if mode == "sel":
            mmh = jnp.where(half0, mr[...].astype(F32), zero).astype(BF16)
            qa = jnp.concatenate([qa, mmh], axis=1)
            qb = jnp.concatenate([qb, mmh], axis=1)
        return qa, qb

    qm = split_halves(q_ref, mm_ref)

    m_sc[...] = jnp.full_like(m_sc, NEG)
    acc_sc[...] = jnp.zeros_like(acc_sc)
    q0 = qi * tq
    rb = LANES
    nrb = tq // rb
    nch = tw // LANES
    hrb = nrb // 2
    dsq = (lax.broadcasted_iota(jnp.int32, (rb, LANES), 0)
           - lax.broadcasted_iota(jnp.int32, (rb, LANES), 1))

    def qk(start, s_sc, r_lo=0, r_hi=tq):
        koff = pl.multiple_of(start, tw)
        kt = kk[pl.ds(koff, tw), :]
        for a in range(2):
            s = _dot_nt(qm[a][r_lo:r_hi], kt)
            if mode == "fox":
                s = s - ck_ref[0, 0, a:a + 1, pl.ds(koff, tw)]
            s_sc[a, r_lo:r_hi] = s

    FULL = [("full",)] * nrb
    DIAG_A = [("diag", r) for r in range(hrb)] + [("full",)] * hrb
    DIAG_B = [None] * hrb + [("diag", r) for r in range(hrb)]
    PREV_P = [("prev", r) for r in range(hrb)] + [None] * hrb
    WIN_A = [("diag", r) for r in range(hrb)] + [("prev", r) for r in range(hrb)]

    def soft_pv(s_sc, start, kinds):
        koff = pl.multiple_of(start, tw)
        vt = v1_sc[pl.ds(koff, tw), :]
        chains = []
        i = 0
        while i < nrb:
            j = i + 1
            if kinds[i] == ("full",):
                while j < nrb and kinds[j] == ("full",) and (j - i) * rb < FLASH_ROWS:
                    j += 1
            if kinds[i] is not None:
                chains.append((slice(i * rb, j * rb), kinds[i]))
            i = j
        for a in range(2):
            for rows, kind in chains:
                if kind[0] == "full":
                    c_lo, c_hi, c_edge = 0, nch, None
                elif kind[0] == "diag":
                    c_lo, c_hi, c_edge = 0, kind[1] + 1, kind[1]
                else:
                    c_lo, c_hi, c_edge = kind[1], nch, kind[1]
                ch = [s_sc[a, rows, c * LANES:(c + 1) * LANES] for c in range(c_lo, c_hi)]
                if c_edge is not None:
                    ok = dsq >= 0 if kind[0] == "diag" else dsq < 0
                    ch[c_edge - c_lo] = jnp.where(ok, ch[c_edge - c_lo], NEG)
                mx = functools.reduce(jnp.maximum, ch)
                m_old = m_sc[a, rows]
                m_new = jnp.maximum(m_old, jnp.max(mx, axis=1, keepdims=True))
                alpha = jnp.exp2(m_old - m_new)
                p = jnp.concatenate([jnp.exp2(c_ - m_new).astype(BF16) for c_ in ch], axis=1)
                m_sc[a, rows] = m_new
                al = jnp.concatenate([alpha, alpha], axis=1)
                acc_sc[a, rows] = (al * acc_sc[a, rows]
                                   + _dot(p, vt[c_lo * LANES:c_hi * LANES]))

    if mode == "win":
        @pl.when(qi > 0)
        def _():
            qk(q0 - tw, s0_sc, 0, tq // 2)
            qk(q0, s1_sc)
            qk(q0 + tw, s2_sc, tq // 2, tq)
            soft_pv(s0_sc, q0 - tw, PREV_P)
            soft_pv(s1_sc, q0, WIN_A)
            soft_pv(s2_sc, q0 + tw, DIAG_B)

        @pl.when(qi == 0)
        def _():
            qk(q0, s1_sc)
            qk(q0 + tw, s2_sc, tq // 2, tq)
            soft_pv(s1_sc, q0, WIN_A)
            soft_pv(s2_sc, q0 + tw, DIAG_B)
    else:
        qk(0, s0_sc)

        def pair_body(i, carry):
            t0 = 2 * i * tw
            qk(t0 + tw, s1_sc)
            soft_pv(s0_sc, t0, FULL)
            qk(t0 + 2 * tw, s0_sc)
            soft_pv(s1_sc, t0 + tw, FULL)
            return carry

        lax.fori_loop(0, qi, pair_body, 0)
        qk(q0 + tw, s1_sc, tq // 2, tq)
        soft_pv(s0_sc, q0, DIAG_A)
        soft_pv(s1_sc, q0 + tw, DIAG_B)

    o0 = acc_sc[0, :, :LANES] / acc_sc[0, :, LANES:]
    o1 = acc_sc[1, :, :LANES] / acc_sc[1, :, LANES:]
    if mode == "diff":
        lq = lam_ref[...]
        lam = (jnp.exp(jnp.sum(lq[0:1] * lq[1:2], keepdims=True))
               - jnp.exp(jnp.sum(lq[2:3] * lq[3:4], keepdims=True)) + lam_init)
        d = o0 - lam * o1
        y = d * lax.rsqrt(jnp.mean(d * d, axis=-1, keepdims=True) + EPS)
        o_ref[...] = (y * g_ref[...]) * (1.0 - lam_init)
    else:
        o_ref[...] = jnp.where(half0, o0, o1)


def _flash(mode, B, S, n_blk, q, q_blk0, k, k_blk, v, v_blk, *,
           mm=None, ck=None, lam=None, subln_g=None, lam_init=0.0):
    tq = min(TQ, S)
    tw = WINDOW
    nq = S // tq
    T = B * S
    assert S % tq == 0 and tq == 2 * tw and tw % LANES == 0
    in_specs = [pl.BlockSpec((tq, LANES), lambda b, h, i: (b * nq + i, q_blk0 + h))]
    args = [q]
    if mode == "sel":
        in_specs.append(pl.BlockSpec((tq, LANES), lambda b, h, i: (b * nq + i, h // 2)))
        args.append(mm)
    in_specs.append(pl.BlockSpec((S, LANES), lambda b, h, i: (b, k_blk(h))))
    args.append(k)
    in_specs.append(pl.BlockSpec((S, LANES), lambda b, h, i: (b, v_blk(h))))
    args.append(v)
    if mode == "fox":
        in_specs.append(pl.BlockSpec((1, 1, 2, S), lambda b, h, i: (b, h, 0, 0)))
        args += [ck]
    if mode == "diff":
        in_specs.append(pl.BlockSpec(lam.shape, lambda b, h, i: (0, 0)))
        in_specs.append(pl.BlockSpec((1, LANES), lambda b, h, i: (0, 0)))
        args += [lam, subln_g]
    scratch = [pltpu.VMEM((2, tq, LANES), F32), pltpu.VMEM((2, tq, 2 * LANES), F32),
               pltpu.VMEM((S, 2 * LANES), BF16)]
    scratch += [pltpu.VMEM((2, tq, tw), F32)] * 3
    if mode == "sel":
        scratch += [pltpu.VMEM((S, 2 * LANES), BF16)]
    return pl.pallas_call(
        functools.partial(_flash_kernel, mode=mode, tq=tq, tw=tw, lam_init=lam_init),
        grid=(B, n_blk, nq),
        in_specs=in_specs,
        out_specs=pl.BlockSpec((tq, LANES), lambda b, h, i: (b * nq + i, h)),
        out_shape=jax.ShapeDtypeStruct((T, n_blk * LANES), F32),
        scratch_shapes=scratch,
        compiler_params=_cparams(("parallel", "parallel", "arbitrary")),
        name="flash_" + mode,
    )(*args)


def _compress_kernel(t_ref, pe_ref, w1_ref, b1_ref, w2_ref, o_ref):
    t = t_ref[0, 0]
    R, half = t.shape
    pe = pe_ref[0]
    xa = (t + pe[:, :half]).astype(BF16)
    xb = (t + pe[:, half:]).astype(BF16)
    a = _dot(xa, w1_ref[0, :half, :])
    b = _dot(xb, w1_ref[0, half:, :])
    hp = a + pltpu.roll(b, R - 1, 0) + b1_ref[0]
    hid = hp * jax.nn.sigmoid(hp)
    o_ref[0, 0] = _dot(hid.astype(BF16), w2_ref[0]).astype(o_ref.dtype)


def _compress(tkv, pe, w1, b1, w2dup):
    _, B, R, W = tkv.shape
    G = G_B
    return pl.pallas_call(
        _compress_kernel,
        grid=(2, B, G),
        in_specs=[pl.BlockSpec((1, 1, R, W), lambda s, b, g: (s * G_B + g, b, 0, 0)),
                  pl.BlockSpec((1, 1, 2 * W), lambda s, b, g: (s, 0, 0)),
                  pl.BlockSpec((1, 2 * W, CMP_HIDDEN), lambda s, b, g: (s, 0, 0)),
                  pl.BlockSpec((1, 1, CMP_HIDDEN), lambda s, b, g: (s, 0, 0)),
                  pl.BlockSpec((1, CMP_HIDDEN, LANES), lambda s, b, g: (s, 0, 0))],
        out_specs=pl.BlockSpec((1, 1, R, LANES), lambda s, b, g: (s, b, 0, g)),
        out_shape=jax.ShapeDtypeStruct((2, B, R, G * LANES), BF16),
        compiler_params=_cparams(("parallel", "parallel", "parallel")),
        name="nsa_compress",
    )(tkv, pe, w1, b1, w2dup)


def _cmp_kernel(q_ref, kc_ref, vc_ref, cov_ref, o_ref, mm_ref, *, tq, n_sel, top_n):
    qi = pl.program_id(2)
    kc = kc_ref[0, 0]
    vc = vc_ref[0, 0]
    R = kc.shape[0]
    lane = lax.broadcasted_iota(jnp.int32, (tq, LANES), 1)
    half0 = lane < HEAD_DIM
    t_r = lax.broadcasted_iota(jnp.int32, (tq, R), 0) + qi * tq
    c_end = lax.broadcasted_iota(jnp.int32, (tq, R), 1) * CMP_STRIDE + (L_CMP - 1)
    ok = c_end <= t_r
    psum = jnp.zeros((tq, R), F32)
    outs = []
    for pr in range(2):
        q2 = q_ref[:, pr * LANES:(pr + 1) * LANES].astype(F32)
        zero = jnp.zeros_like(q2)
        pair = []
        for a in range(2):
            qm = (jnp.where(half0, q2, zero) if a == 0 else jnp.where(half0, zero, q2)).astype(BF16)
            s = jnp.where(ok, _dot_nt(qm, kc), NEG)
            m = jnp.max(s, axis=1, keepdims=True)
            e = jnp.where(ok, jnp.exp(s - m), 0.0)
            l = jnp.sum(e, axis=1, keepdims=True)
            p = e / jnp.where(l > 0.0, l, 1.0)
            psum = psum + p
            pair.append(_dot(p.astype(BF16), vc))
        outs.append(jnp.where(half0, pair[0], pair[1]))
    o_ref[...] = jnp.concatenate(outs, axis=1)

    imp = _dot_exact01(psum, cov_ref[...])
    jl = lane & (HEAD_DIM - 1)
    t_q = lax.broadcasted_iota(jnp.int32, (tq, LANES), 0) + qi * tq
    tb = lax.shift_right_logical(t_q, int(math.log2(L_SEL)))
    valid = jl <= tb
    forced = (jl == 0) | (jl == tb) | (jl == tb - 1)
    in_range = jl < n_sel
    score = jnp.where(forced, LOWEST, jnp.where(valid, imp, -BIG))
    score = jnp.where(in_range, score, LOWEST)
    for _ in range(top_n - 3):
        idx = jnp.argmax(score, axis=1, keepdims=True).astype(jnp.int32) & (HEAD_DIM - 1)
        score = jnp.where(jl == idx, LOWEST, score)
    mm_ref[...] = jnp.where((score < MARKED) & in_range, 0.0, -1.0).astype(mm_ref.dtype)


def _cmp_topk(B, S, q, q_blk0, ckv, cover_dup):
    tq = min(CMP_TQ, S)
    nq = S // tq
    T = B * S
    R = ckv.shape[2]
    n_sel = S // L_SEL
    gw = 2 * LANES
    return pl.pallas_call(
        functools.partial(_cmp_kernel, tq=tq, n_sel=n_sel, top_n=min(TOP_N, n_sel)),
        grid=(B, G_B, nq),
        in_specs=[pl.BlockSpec((tq, gw), lambda b, g, i: (b * nq + i, q_blk0 // 2 + g)),
                  pl.BlockSpec((1, 1, R, LANES), lambda b, g, i: (0, b, 0, g)),
                  pl.BlockSpec((1, 1, R, LANES), lambda b, g, i: (1, b, 0, g)),
                  pl.BlockSpec((R, LANES), lambda b, g, i: (0, 0))],
        out_specs=[pl.BlockSpec((tq, gw), lambda b, g, i: (b * nq + i, g)),
                   pl.BlockSpec((tq, LANES), lambda b, g, i: (b * nq + i, g))],
        out_shape=[jax.ShapeDtypeStruct((T, G_B * gw), F32),
                   jax.ShapeDtypeStruct((T, G_B * LANES), BF16)],
        compiler_params=_cparams(("parallel", "parallel", "parallel")),
        name="nsa_cmp_topk",
    )(q, ckv, ckv, cover_dup)


def _post_kernel(aa_ref, oc_ref, os_ref, ow_ref, ac_ref, fg_ref, h_ref, x_ref, p_ref,
                 wz_ref, wa_ref, wb_ref, wc_ref, wo_ref, wg_ref, wp_ref, gn_ref,
                 *outs, last):
    def silu(z):
        return z * jax.nn.sigmoid(z)

    W = W_B
    D = x_ref.shape[1]
    mz = _dot(h_ref[...], wz_ref[...])
    za = mz[:, 3 * D:3 * D + W]
    zb = mz[:, 3 * D + W:3 * D + 2 * W]
    zc = mz[:, 3 * D + 2 * W:]
    sg = jax.nn.sigmoid(fg_ref[...])
    head = lax.shift_right_logical(lax.broadcasted_iota(jnp.int32, sg.shape, 1), int(math.log2(HEAD_DIM)))
    mix = []
    for j in range(W // LANES):
        cols = slice(j * LANES, (j + 1) * LANES)
        src = H_A + N_BRANCH * (2 * j + head)
        g = [jnp.take_along_axis(sg, src + r, axis=1) for r in range(N_BRANCH)]
        mix.append(g[0] * oc_ref[:, cols] + g[1] * os_ref[:, cols] + g[2] * ow_ref[:, cols])
    ya = aa_ref[...] * silu(za)
    yb = jnp.concatenate(mix, axis=1) * silu(zb)
    yc = ac_ref[...] * silu(zc)
    pa = _dot(ya.astype(BF16), wa_ref[...])
    pb = _dot(yb.astype(BF16), wb_ref[...])
    pc = _dot(yc.astype(BF16), wc_ref[...])
    merged = (jax.nn.sigmoid(mz[:, :D]) * pa + jax.nn.sigmoid(mz[:, D:2 * D]) * pb
              + jax.nn.sigmoid(mz[:, 2 * D:3 * D]) * pc)
    x1 = x_ref[...] + _dot(merged.astype(BF16), wo_ref[...])
    gate = jax.nn.sigmoid(_dot(x1.astype(BF16), wg_ref[...]))
    x2 = x1 + gate * _dot(p_ref[...].astype(BF16), wp_ref[...])
    y = x2 * lax.rsqrt(jnp.mean(x2 * x2, axis=-1, keepdims=True) + EPS) * gn_ref[...]
    if last:
        outs[0][...] = y
    else:
        outs[0][...] = x2
        outs[1][...] = y.astype(BF16)


def _post(aa, oc, osel, ow, ac, fg, h, x, p, layer, wz, wa, wb, wc, wo, wg, wp, g_next, last):
    T, D = x.shape
    tm = min(POST_TM, T)
    W = W_A
    row = lambda i: (i, 0)
    const = lambda i: (0, 0)

    def resident(arr):
        return pl.BlockSpec(arr.shape, const, pipeline_mode=pl.Buffered(1))

    in_specs = [pl.BlockSpec((tm, W), row)] * 5 + [
        pl.BlockSpec((tm, LANES), row),
        pl.BlockSpec((tm, D), row),
        pl.BlockSpec((tm, D), row),
        pl.BlockSpec((tm, p.shape[1]), lambda i: (i + layer * (T // tm), 0)),
        pl.BlockSpec((None,) + wz.shape[1:], lambda i: (layer, 0, 0), pipeline_mode=pl.Buffered(1)),
        resident(wa), resident(wb), resident(wc), resident(wo), resident(wg),
        resident(wp), pl.BlockSpec((1, D), const)]
    if last:
        out_specs = [pl.BlockSpec((tm, D), row)]
        out_shape = [jax.ShapeDtypeStruct((T, D), F32)]
    else:
        out_specs = [pl.BlockSpec((tm, D), row), pl.BlockSpec((tm, D), row)]
        out_shape = [jax.ShapeDtypeStruct((T, D), F32), jax.ShapeDtypeStruct((T, D), BF16)]
    return pl.pallas_call(
        functools.partial(_post_kernel, last=last),
        grid=(T // tm,),
        in_specs=in_specs,
        out_specs=out_specs,
        out_shape=out_shape,
        compiler_params=_cparams(("parallel",)),
        name="post",
    )(aa, oc, osel, ow, ac, fg, h, x, p, wz, wa, wb, wc, wo, wg, wp, g_next)


def _cover_matrix(R, n_sel):
    c_start = np.arange(R)[:, None] * CMP_STRIDE
    j_start = np.arange(HEAD_DIM)[None, :] * L_SEL
    cov = ((c_start < j_start + L_SEL) & (c_start + L_CMP > j_start)
           & (np.arange(HEAD_DIM)[None, :] < n_sel)).astype(np.float32)
    return jnp.asarray(np.concatenate([cov, cov], axis=1), BF16)


_SEG_OFFS = np.cumsum((0,) + SPLIT_SIZES)
(_QA, _KA, _VA, _FA, _ZA, _QB, _KCB, _VCB, _KSB, _VSB, _KWB, _VWB, _GB, _ZB, _QC, _KC, _VC, _ZC,
 _MG) = range(len(SPLIT_SIZES))
W1_COLS = 5 * W_A + 2 * 2 * KV_B
W2_COLS = 3 * W_A + 2 * 2 * KV_B
W3_COLS = LANES + 2 * KV_B
WZ_COLS = N_BRANCH * D_MODEL + 3 * W_A


def _pack_kernel(w_ref, w1_ref, w2_ref, w3_ref, wz_ref):
    qs = SCALE * LOG2E

    def seg(k, scale=None, lo=0, hi=None):
        a = int(_SEG_OFFS[k]) + lo
        b = int(_SEG_OFFS[k + 1]) if hi is None else int(_SEG_OFFS[k]) + hi
        x = w_ref[0, :, a:b]
        return (x if scale is None else x * scale).astype(BF16)

    def dup(k):
        g0, g1 = seg(k, hi=HEAD_DIM), seg(k, lo=HEAD_DIM)
        return [g0, g0, g1, g1]

    def put(o_ref, pieces):
        c = 0
        for x in pieces:
            o_ref[0, :, c:c + x.shape[1]] = x
            c += x.shape[1]
        assert c == o_ref.shape[2]

    rows = w_ref.shape[1]
    put(w1_ref, [seg(_QA, qs), seg(_KA), seg(_VA), seg(_QB, SCALE), seg(_VC)] + dup(_VSB) + dup(_VWB))
    put(w2_ref, [seg(_QB, qs), seg(_QC, qs), seg(_KC)] + dup(_KWB) + dup(_KSB))
    put(w3_ref, [seg(_FA), seg(_GB), jnp.zeros((rows, LANES - H_A - 3 * H_B), BF16), seg(_KCB), seg(_VCB)])
    put(wz_ref, [seg(_MG), seg(_ZA), seg(_ZB), seg(_ZC)])


def _pack_weights(w_in, tr=256):
    depth, D, N = w_in.shape
    widths = (W1_COLS, W2_COLS, W3_COLS, WZ_COLS)
    return pl.pallas_call(
        _pack_kernel,
        grid=(depth, D // tr),
        in_specs=[pl.BlockSpec((1, tr, N), lambda l, i: (l, i, 0))],
        out_specs=[pl.BlockSpec((1, tr, n), lambda l, i: (l, i, 0)) for n in widths],
        out_shape=[jax.ShapeDtypeStruct((depth, D, n), BF16) for n in widths],
        compiler_params=_cparams(("parallel", "parallel")),
        name="pack_weights",
    )(w_in)


O1_QA, O1_KA, O1_VA, O1_QBU, O1_VC, O1_VS, O1_VW = 0, 4, 8, 12, 16, 20, 22
O2_QBR, O2_QC, O2_KC, O2_KW, O2_KS = 0, 4, 8, 12, 14


def kernel(x, p, positions, norm_g, w_in, b_forget, cmp_pe_k, cmp_w1_k, cmp_b1_k, cmp_w2_k,
           cmp_pe_v, cmp_w1_v, cmp_b1_v, cmp_w2_v, diff_lam, diff_subln_g,
           w_br_a, w_br_b, w_br_c, w_out, w_ple, w_ple_gate, final_g):
    B, S, D = x.shape
    depth = w_in.shape[0]
    T = B * S
    R = S // CMP_STRIDE
    n_sel = S // L_SEL
    assert n_sel <= HEAD_DIM and S % min(TQ, S) == 0

    xf = x.reshape(T, D)
    cos_t, sin_t = _rope_tables(positions.astype(F32).reshape(T))
    cover = _cover_matrix(R, n_sel)
    h = _rmsnorm(xf, norm_g[0], BF16)
    w1, w2, w3, wz = _pack_weights(w_in)

    for i in range(depth):
        o1 = _proj(h, w1, i, BF16, 1024, name="proj_plain")
        o2 = _proj(h, w2, i, BF16, 1024, rope_tabs=(cos_t, sin_t), name="proj_rope")
        fg, kv4 = _proj_f32(h, w3, i)

        b_pad = jnp.pad(b_forget[i], (0, LANES - H_A)).reshape(1, LANES)
        ck = _forget_cumsum(fg, b_pad, B, S).reshape(B, H_A // 2, 2, S)
        att_a = _flash("fox", B, S, H_A // 2, o1, O1_QA, o1, lambda h_: O1_KA + h_,
                       o1, lambda h_: O1_VA + h_, ck=ck)

        lam_init = 0.8 - 0.6 * math.exp(-0.3 * i)
        att_c = _flash("diff", B, S, H_C, o2, O2_QC, o2, lambda h_: O2_KC + h_,
                       o1, lambda h_: O1_VC + h_, lam=diff_lam[i],
                       subln_g=diff_subln_g[i].reshape(1, LANES), lam_init=lam_init)

        tkv = kv4.reshape(2 * G_B, B, R, CMP_STRIDE * HEAD_DIM)
        pe =jnp.stack([cmp_pe_k[i].reshape(1, -1), cmp_pe_v[i].reshape(1, -1)])
        cw1 = jnp.stack([cmp_w1_k[i], cmp_w1_v[i]]).astype(BF16)
        cb1 = jnp.stack([cmp_b1_k[i].reshape(1, -1), cmp_b1_v[i].reshape(1, -1)])
        cw2 = jnp.stack([cmp_w2_k[i], cmp_w2_v[i]])
        cw2 = jnp.concatenate([cw2, cw2], axis=-1).astype(BF16)
        ckv = _compress(tkv, pe, cw1, cb1, cw2)
        o_cmp, mm = _cmp_topk(B, S, o1, O1_QBU, ckv, cover)
        o_sel = _flash("sel", B, S, H_B // 2, o2, O2_QBR, o2, lambda h_: O2_KS + h_ // 2,
                       o1, lambda h_: O1_VS + h_ // 2, mm=mm)
        o_win = _flash("win", B, S, H_B // 2, o2, O2_QBR, o2, lambda h_: O2_KW + h_ // 2,
                       o1, lambda h_: O1_VW + h_ // 2)

        last = i == depth - 1
        g_next = (final_g if last else norm_g[i + 1]).reshape(1, D)
        res = _post(att_a, o_cmp, o_sel, o_win, att_c, fg, h, xf, p.reshape(depth * T, -1), i,
                    wz, w_br_a[i].astype(BF16), w_br_b[i].astype(BF16), w_br_c[i].astype(BF16),
                    w_out[i].astype(BF16), w_ple_gate[i].astype(BF16), w_ple[i].astype(BF16),
                    g_next, last)
        if last:
            return res[0].reshape(B, S, D)
        xf, h = res
```

```python
import functools
import math

import numpy as np
import jax
import jax.numpy as jnp
from jax import lax
from jax.experimental import pallas as pl
from jax.experimental.pallas import tpu as pltpu

F32 = jnp.float32
BF16 = jnp.bfloat16

LANES = 128
HEAD_DIM = 64
NEG = -1e30
BIG = 1e30
LOWEST = -2.0 ** 127
MARKED = -2.0 ** 120
EPS = 1e-6
ROPE_THETA = 10000.0
H_A = 8
H_B = 8
G_B = 2
H_C = 4
L_CMP = 32
CMP_STRIDE = 16
CMP_HIDDEN = 256
L_SEL = 64
TOP_N = 16
WINDOW = 512
N_BRANCH = 3
D_MODEL = 1024
W_A = H_A * HEAD_DIM
W_B = H_B * HEAD_DIM
W_C = H_C * 2 * HEAD_DIM
KV_B = G_B * HEAD_DIM
SPLIT_SIZES = (W_A, W_A, W_A, H_A, W_A,
               W_B, KV_B, KV_B, KV_B, KV_B, KV_B, KV_B, 3 * H_B, W_B,
               2 * H_C * HEAD_DIM, 2 * H_C * HEAD_DIM, W_C, W_C,
               N_BRANCH * D_MODEL)
SCALE = HEAD_DIM ** -0.5
LOG2E = math.log2(math.e)

VMEM_LIMIT = 48 * 1024 * 1024

TQ = 1024
CMP_TQ = 512
FLASH_ROWS = 512
PROJ_TM = 1024
POST_TM = 256


def _cparams(sem):
    return pltpu.CompilerParams(dimension_semantics=sem, vmem_limit_bytes=VMEM_LIMIT)


def _dot(a, b):
    return jnp.dot(a, b, preferred_element_type=F32)


def _dot_nt(a, b):
    return lax.dot_general(a, b, (((1,), (1,)), ((), ())), preferred_element_type=F32)


def _split3(x):
    hi = x.astype(BF16)
    r1 = x - hi.astype(F32)
    mid = r1.astype(BF16)
    lo = (r1 - mid.astype(F32)).astype(BF16)
    return hi, mid, lo


def _dot_exact01(x, m01):
    hi, mid, lo = _split3(x)
    return _dot(hi, m01) + _dot(mid, m01) + _dot(lo, m01)


def _rmsnorm_kernel(x_ref, g_ref, o_ref):
    x = x_ref[...]
    y = x * lax.rsqrt(jnp.mean(x * x, axis=-1, keepdims=True) + EPS)
    o_ref[...] = (y * g_ref[...]).astype(o_ref.dtype)


def _rmsnorm(x, g, out_dtype, tm=1024):
    T, D = x.shape
    return pl.pallas_call(
        _rmsnorm_kernel,
        grid=(T // tm,),
        in_specs=[pl.BlockSpec((tm, D), lambda i: (i, 0)),
                  pl.BlockSpec((1, D), lambda i: (0, 0))],
        out_specs=pl.BlockSpec((tm, D), lambda i: (i, 0)),
        out_shape=jax.ShapeDtypeStruct((T, D), out_dtype),
        compiler_params=_cparams(("parallel",)),
        name="rmsnorm",
    )(x, g.reshape(1, D))


def _rope_table_kernel(pos_ref, invf_ref, sign_ref, cos_ref, sin_ref):
    ang = pos_ref[...] * invf_ref[...]
    cos_ref[...] = jnp.cos(ang)
    sin_ref[...] = jnp.sin(ang) * sign_ref[...]


def _rope_tables(pos_f32, tm=512):
    T = pos_f32.shape[0]
    half = HEAD_DIM // 2
    inv_freq = ROPE_THETA ** (-jnp.arange(half, dtype=F32) / half)
    invf = jnp.tile(inv_freq, LANES // half).reshape(1, LANES)
    sign = jnp.tile(jnp.concatenate([-jnp.ones((half,), F32), jnp.ones((half,), F32)]),
                    LANES // HEAD_DIM).reshape(1, LANES)
    return pl.pallas_call(
        _rope_table_kernel,
        grid=(T // tm,),
        in_specs=[pl.BlockSpec((tm, 1), lambda i: (i, 0)),
                  pl.BlockSpec((1, LANES), lambda i: (0, 0)),
                  pl.BlockSpec((1, LANES), lambda i: (0, 0))],
        out_specs=[pl.BlockSpec((tm, LANES), lambda i: (i, 0)),
                   pl.BlockSpec((tm, LANES), lambda i: (i, 0))],
        out_shape=[jax.ShapeDtypeStruct((T, LANES), F32),
                   jax.ShapeDtypeStruct((T, LANES), F32)],
        compiler_params=_cparams(("parallel",)),
        name="rope_table",
    )(pos_f32.reshape(T, 1), invf, sign)


def _proj_kernel(h_ref, w_ref, *rest, rope):
    acc = _dot_nt(h_ref[...], w_ref[...])
    if rope:
        cos_ref, sin_ref, o_ref = rest
        cos = cos_ref[...]
        sin = sin_ref[...]
        lane = lax.broadcasted_iota(jnp.int32, cos.shape, 1)
        first = (lane & (HEAD_DIM - 1)) < (HEAD_DIM // 2)
        for c in range(acc.shape[1] // LANES):
            t = acc[:, c * LANES:(c + 1) * LANES]
            sw = jnp.where(first, pltpu.roll(t, LANES - HEAD_DIM // 2, 1),
                           pltpu.roll(t, HEAD_DIM // 2, 1))
            o_ref[:, c * LANES:(c + 1) * LANES] = (t * cos + sw * sin).astype(o_ref.dtype)
    else:
        (o_ref,) = rest
        o_ref[...] = acc.astype(o_ref.dtype)


def _proj_f32_kernel(h_ref, w_ref, fg_ref, kv_ref):
    acc = _dot_nt(h_ref[...], w_ref[...])
    fg_ref[...] = acc[:, :LANES]
    for c in range(2 * G_B):
        kv_ref[c] = acc[:, LANES + c * HEAD_DIM:LANES + (c + 1) * HEAD_DIM]


def _proj_f32(h, w, layer):
    T, D = h.shape
    tm = min(PROJ_TM, T)
    return pl.pallas_call(
        _proj_f32_kernel,
        grid=(T // tm,),
        in_specs=[pl.BlockSpec((tm, D), lambda i: (i, 0)),
                  pl.BlockSpec((None,) + w.shape[1:], lambda i: (layer, 0, 0))],
        out_specs=[pl.BlockSpec((tm, LANES), lambda i: (i, 0)),
                   pl.BlockSpec((2 * G_B, tm, HEAD_DIM), lambda i: (0, i, 0))],
        out_shape=[jax.ShapeDtypeStruct((T, LANES), F32),
                   jax.ShapeDtypeStruct((2 * G_B, T, HEAD_DIM), F32)],
        compiler_params=_cparams(("parallel",)),
        name="proj_f32",
    )(h, w)


def _proj(h, w, layer, out_dtype, tn, rope_tabs=None, name="proj"):
    T, D = h.shape
    N = w.shape[1]
    tm = min(PROJ_TM, T)
    assert T % tm == 0 and N % tn == 0
    in_specs = [pl.BlockSpec((tm, D), lambda i, j: (i, 0)),
                pl.BlockSpec((None, tn, D), lambda i, j: (layer, j, 0))]
    args = [h, w]
    if rope_tabs is not None:
        in_specs += [pl.BlockSpec((tm, LANES), lambda i, j: (i, 0)),
                     pl.BlockSpec((tm, LANES), lambda i, j: (i, 0))]
        args += list(rope_tabs)
    return pl.pallas_call(
        functools.partial(_proj_kernel, rope=rope_tabs is not None),
        grid=(T // tm, N // tn),
        in_specs=in_specs,
        out_specs=pl.BlockSpec((tm, tn), lambda i, j: (i, j)),
        out_shape=jax.ShapeDtypeStruct((T, N), out_dtype),
        compiler_params=_cparams(("parallel", "arbitrary")),
        name=name,
    )(*args)


def _cumsum_kernel(fg_ref, b_ref, c_ref, carry_sc, *, tiles_per_seq):
    i = pl.program_id(0)

    @pl.when(i % tiles_per_seq == 0)
    def _():
        carry_sc[...] = jnp.zeros_like(carry_sc)

    z = fg_ref[...] + b_ref[...]
    logf = jnp.minimum(z, 0.0) - jnp.log1p(jnp.exp(-jnp.abs(z)))
    tm = z.shape[0]
    r = lax.broadcasted_iota(jnp.int32, (tm, tm), 0)
    c = lax.broadcasted_iota(jnp.int32, (tm, tm), 1)
    tri = jnp.where(c <= r, 1.0, 0.0).astype(BF16)
    cs = _dot_exact01_left(tri, logf) + carry_sc[...]
    carry_sc[...] = cs[tm - 1:tm, :]
    c_ref[0] = jnp.transpose(cs * LOG2E)[:H_A, :]


def _dot_exact01_left(m01, x):
    hi, mid, lo = _split3(x)
    return _dot(m01, hi) + _dot(m01, mid) + _dot(m01, lo)


def _forget_cumsum(fg, b_pad, B, S, tm=256):
    tps = S // tm
    return pl.pallas_call(
        functools.partial(_cumsum_kernel, tiles_per_seq=tps),
        grid=(B * tps,),
        in_specs=[pl.BlockSpec((tm, LANES), lambda i: (i, 0)),
                  pl.BlockSpec((1, LANES), lambda i: (0, 0))],
        out_specs=pl.BlockSpec((1, H_A, tm), lambda i: (i // tps, 0, i % tps)),
        out_shape=jax.ShapeDtypeStruct((B, H_A, S), F32),
        scratch_shapes=[pltpu.VMEM((1, LANES), F32)],
        compiler_params=_cparams(("arbitrary",)),
        name="forget_cumsum",
    )(fg, b_pad)


def _flash_kernel(*refs, mode, tq, tw, lam_init):
    it = iter(refs)
    q_ref = next(it)
    mm_ref = next(it) if mode == "sel" else None
    k_ref = next(it)
    v_ref = next(it)
    if mode == "fox":
        ck_ref = next(it)
    if mode == "diff":
        lam_ref = next(it)
        g_ref = next(it)
    o_ref = next(it)
    m_sc = next(it)
    acc_sc = next(it)
    v1_sc = next(it)
    s0_sc = next(it)
    s1_sc = next(it)
    s2_sc = next(it)
    if mode == "sel":
        ke_sc = next(it)

    qi = pl.program_id(2)
    S = k_ref.shape[0]
    lane = lax.broadcasted_iota(jnp.int32, (tq, LANES), 1)
    half0 = lane < HEAD_DIM

    @pl.when(qi == 0)
    def _():
        v1_sc[:, :LANES] = v_ref[...]
        v1_sc[:, LANES:] = jnp.ones((S, LANES), BF16)
        if mode == "sel":
            row = lax.broadcasted_iota(jnp.int32, (S, LANES), 0)
            ln = lax.broadcasted_iota(jnp.int32, (S, LANES), 1)
            blk = lax.shift_right_logical(row, int(math.log2(L_SEL)))
            ke_sc[:, :LANES] = k_ref[...]
            ke_sc[:, LANES:] = jnp.where(ln == blk, BIG, 0.0).astype(BF16)

    kk = ke_sc if mode == "sel" else k_ref

    def split_halves(qr, mr):
        q2 = qr[...].astype(F32)
        zero = jnp.zeros_like(q2)
        qa = jnp.where(half0, q2, zero).astype(BF16)
        qb = jnp.where(half0, zero, q2).astype(BF16)
        if mode == "sel":
            mmh = jnp.where(half0, mr[...].astype(F32), zero).astype(BF16)
            qa = jnp.concatenate([qa, mmh], axis=1)
            qb = jnp.concatenate([qb, mmh], axis=1)
        return qa, qb

    qm = split_halves(q_ref, mm_ref)

    m_sc[...] = jnp.full_like(m_sc, NEG)
    acc_sc[...] = jnp.zeros_like(acc_sc)
    q0 = qi * tq
    rb = LANES
    nrb = tq // rb
    nch = tw // LANES
    hrb = nrb // 2
    dsq = (lax.broadcasted_iota(jnp.int32, (rb, LANES), 0)
           - lax.broadcasted_iota(jnp.int32, (rb, LANES), 1))

    def qk(start, s_sc, r_lo=0, r_hi=tq):
        koff = pl.multiple_of(start, tw)
        kt = kk[pl.ds(koff, tw), :]
        for a in range(2):
            s = _dot_nt(qm[a][r_lo:r_hi], kt)
            if mode == "fox":
                s = s - ck_ref[0, 0, a:a + 1, pl.ds(koff, tw)]
            s_sc[a, r_lo:r_hi] = s

    FULL = [("full",)] * nrb
    DIAG_A = [("diag", r) for r in range(hrb)] + [("full",)] * hrb
    DIAG_B = [None] * hrb + [("diag", r) for r in range(hrb)]
    PREV_P = [("prev", r) for r in range(hrb)] + [None] * hrb
    WIN_A = [("diag", r) for r in range(hrb)] + [("prev", r) for r in range(hrb)]

    def soft_pv(s_sc, start, kinds):
        koff = pl.multiple_of(start, tw)
        vt = v1_sc[pl.ds(koff, tw), :]
        chains = []
        i = 0
        while i < nrb:
            j = i + 1
            if kinds[i] == ("full",):
                while j < nrb and kinds[j] == ("full",) and (j - i) * rb < FLASH_ROWS:
                    j += 1
            if kinds[i] is not None:
                chains.append((slice(i * rb, j * rb), kinds[i]))
            i = j
        for a in range(2):
            for rows, kind in chains:
                if kind[0] == "full":
                    c_lo, c_hi, c_edge = 0, nch, None
                elif kind[0] == "diag":
                    c_lo, c_hi, c_edge = 0, kind[1] + 1, kind[1]
                else:
                    c_lo, c_hi, c_edge = kind[1], nch, kind[1]
                ch = [s_sc[a, rows, c * LANES:(c + 1) * LANES] for c in range(c_lo, c_hi)]
                if c_edge is not None:
                    ok = dsq >= 0 if kind[0] == "diag" else dsq < 0
                    ch[c_edge - c_lo] = jnp.where(ok, ch[c_edge - c_lo], NEG)
                mx = functools.reduce(jnp.maximum, ch)
                m_old = m_sc[a, rows]
                m_new = jnp.maximum(m_old, jnp.max(mx, axis=1, keepdims=True))
                alpha = jnp.exp2(m_old - m_new)
                p = jnp.concatenate([jnp.exp2(c_ - m_new).astype(BF16) for c_ in ch], axis=1)
                m_sc[a, rows] = m_new
                al = jnp.concatenate([alpha, alpha], axis=1)
                acc_sc[a, rows] = (al * acc_sc[a, rows]
                                   + _dot(p, vt[c_lo * LANES:c_hi * LANES]))

    if mode == "win":
        @pl.when(qi > 0)
        def _():
            qk(q0 - tw, s0_sc, 0, tq // 2)
            qk(q0, s1_sc)
            qk(q0 + tw, s2_sc, tq // 2, tq)
            soft_pv(s0_sc, q0 - tw, PREV_P)
            soft_pv(s1_sc, q0, WIN_A)
            soft_pv(s2_sc, q0 + tw, DIAG_B)

        @pl.when(qi == 0)
        def _():
            qk(q0, s1_sc)
            qk(q0 + tw, s2_sc, tq // 2, tq)
            soft_pv(s1_sc, q0, WIN_A)
            soft_pv(s2_sc, q0 + tw, DIAG_B)
    else:
        qk(0, s0_sc)

        def pair_body(i, carry):
            t0 = 2 * i * tw
            qk(t0 + tw, s1_sc)
            soft_pv(s0_sc, t0, FULL)
            qk(t0 + 2 * tw, s0_sc)
            soft_pv(s1_sc, t0 + tw, FULL)
            return carry

        lax.fori_loop(0, qi, pair_body, 0)
        qk(q0 + tw, s1_sc, tq // 2, tq)
        soft_pv(s0_sc, q0, DIAG_A)
        soft_pv(s1_sc, q0 + tw, DIAG_B)

    o0 = acc_sc[0, :, :LANES] / acc_sc[0, :, LANES:]
    o1 = acc_sc[1, :, :LANES] / acc_sc[1, :, LANES:]
    if mode == "diff":
        lq = lam_ref[...]
        lam = (jnp.exp(jnp.sum(lq[0:1] * lq[1:2], keepdims=True))
               - jnp.exp(jnp.sum(lq[2:3] * lq[3:4], keepdims=True)) + lam_init)
        d = o0 - lam * o1
        y = d * lax.rsqrt(jnp.mean(d * d, axis=-1, keepdims=True) + EPS)
        o_ref[...] = (y * g_ref[...]) * (1.0 - lam_init)
    else:
        o_ref[...] = jnp.where(half0, o0, o1)


def _flash(mode, B, S, n_blk, q, q_blk0, k, k_blk, v, v_blk, *,
           mm=None, ck=None, lam=None, subln_g=None, lam_init=0.0):
    tq = min(TQ, S)
    tw = WINDOW
    nq = S // tq
    T = B * S
    assert S % tq == 0 and tq == 2 * tw and tw % LANES == 0
    in_specs = [pl.BlockSpec((tq, LANES), lambda b, h, i: (b * nq + i, q_blk0 + h))]
    args = [q]
    if mode == "sel":
        in_specs.append(pl.BlockSpec((tq, LANES), lambda b, h, i: (b * nq + i, h // 2)))
        args.append(mm)
    in_specs.append(pl.BlockSpec((S, LANES), lambda b, h, i: (b, k_blk(h))))
    args.append(k)
    in_specs.append(pl.BlockSpec((S, LANES), lambda b, h, i: (b, v_blk(h))))
    args.append(v)
    if mode == "fox":
        in_specs.append(pl.BlockSpec((1, 1, 2, S), lambda b, h, i: (b, h, 0, 0)))
        args += [ck]
    if mode == "diff":
        in_specs.append(pl.BlockSpec(lam.shape, lambda b, h, i: (0, 0)))
        in_specs.append(pl.BlockSpec((1, LANES), lambda b, h, i: (0, 0)))
        args += [lam, subln_g]
    scratch = [pltpu.VMEM((2, tq, LANES), F32), pltpu.VMEM((2, tq, 2 * LANES), F32),
               pltpu.VMEM((S, 2 * LANES), BF16)]
    scratch += [pltpu.VMEM((2, tq, tw), F32)] * 3
    if mode == "sel":
        scratch += [pltpu.VMEM((S, 2 * LANES), BF16)]
    return pl.pallas_call(
        functools.partial(_flash_kernel, mode=mode, tq=tq, tw=tw, lam_init=lam_init),
        grid=(B, n_blk, nq),
        in_specs=in_specs,
        out_specs=pl.BlockSpec((tq, LANES), lambda b, h, i: (b * nq + i, h)),
        out_shape=jax.ShapeDtypeStruct((T, n_blk * LANES), F32),
        scratch_shapes=scratch,
        compiler_params=_cparams(("parallel", "parallel", "arbitrary")),
        name="flash_" + mode,
    )(*args)


def _compress_kernel(t_ref, pe_ref, w1_ref, b1_ref, w2_ref, o_ref):
    t = t_ref[0, 0]
    R, half = t.shape
    pe = pe_ref[0]
    xa = (t + pe[:, :half]).astype(BF16)
    xb = (t + pe[:, half:]).astype(BF16)
    a = _dot(xa, w1_ref[0, :half, :])
    b = _dot(xb, w1_ref[0, half:, :])
    hp = a + pltpu.roll(b, R - 1, 0) + b1_ref[0]
    hid = hp * jax.nn.sigmoid(hp)
    o_ref[0, 0] = _dot(hid.astype(BF16), w2_ref[0]).astype(o_ref.dtype)


def _compress(tkv, pe, w1, b1, w2dup):
    _, B, R, W = tkv.shape
    G = G_B
    return pl.pallas_call(
        _compress_kernel,
        grid=(2, B, G),
        in_specs=[pl.BlockSpec((1, 1, R, W), lambda s, b, g: (s * G_B + g, b, 0, 0)),
                  pl.BlockSpec((1, 1, 2 * W), lambda s, b, g: (s, 0, 0)),
                  pl.BlockSpec((1, 2 * W, CMP_HIDDEN), lambda s, b, g: (s, 0, 0)),
                  pl.BlockSpec((1, 1, CMP_HIDDEN), lambda s, b, g: (s, 0, 0)),
                  pl.BlockSpec((1, CMP_HIDDEN, LANES), lambda s, b, g: (s, 0, 0))],
        out_specs=pl.BlockSpec((1, 1, R, LANES), lambda s, b, g: (s, b, 0, g)),
        out_shape=jax.ShapeDtypeStruct((2, B, R, G * LANES), BF16),
        compiler_params=_cparams(("parallel", "parallel", "parallel")),
        name="nsa_compress",
    )(tkv, pe, w1, b1, w2dup)


def _cmp_kernel(q_ref, kc_ref, vc_ref, cov_ref, o_ref, mm_ref, *, tq, n_sel, top_n):
    qi = pl.program_id(2)
    kc = kc_ref[0, 0]
    vc = vc_ref[0, 0]
    R = kc.shape[0]
    lane = lax.broadcasted_iota(jnp.int32, (tq, LANES), 1)
    half0 = lane < HEAD_DIM
    t_r = lax.broadcasted_iota(jnp.int32, (tq, R), 0) + qi * tq
    c_end = lax.broadcasted_iota(jnp.int32, (tq, R), 1) * CMP_STRIDE + (L_CMP - 1)
    ok = c_end <= t_r
    psum = jnp.zeros((tq, R), F32)
    outs = []
    for pr in range(2):
        q2 = q_ref[:, pr * LANES:(pr + 1) * LANES].astype(F32)
        zero = jnp.zeros_like(q2)
        pair = []
        for a in range(2):
            qm = (jnp.where(half0, q2, zero) if a == 0 else jnp.where(half0, zero, q2)).astype(BF16)
            s = jnp.where(ok, _dot_nt(qm, kc), NEG)
            m = jnp.max(s, axis=1, keepdims=True)
            e = jnp.where(ok, jnp.exp(s - m), 0.0)
            l = jnp.sum(e, axis=1, keepdims=True)
            p = e / jnp.where(l > 0.0, l, 1.0)
            psum = psum + p
            pair.append(_dot(p.astype(BF16), vc))
        outs.append(jnp.where(half0, pair[0], pair[1]))
    o_ref[...] = jnp.concatenate(outs, axis=1)

    imp = _dot_exact01(psum, cov_ref[...])
    jl = lane & (HEAD_DIM - 1)
    t_q = lax.broadcasted_iota(jnp.int32, (tq, LANES), 0) + qi * tq
    tb = lax.shift_right_logical(t_q, int(math.log2(L_SEL)))
    valid = jl <= tb
    forced = (jl == 0) | (jl == tb) | (jl == tb - 1)
    in_range = jl < n_sel
    score = jnp.where(forced, LOWEST, jnp.where(valid, imp, -BIG))
    score = jnp.where(in_range, score, LOWEST)
    for _ in range(top_n - 3):
        idx = jnp.argmax(score, axis=1, keepdims=True).astype(jnp.int32) & (HEAD_DIM - 1)
        score = jnp.where(jl == idx, LOWEST, score)
    mm_ref[...] = jnp.where((score < MARKED) & in_range, 0.0, -1.0).astype(mm_ref.dtype)


def _cmp_topk(B, S, q, q_blk0, ckv, cover_dup):
    tq = min(CMP_TQ, S)
    nq = S // tq
    T = B * S
    R = ckv.shape[2]
    n_sel = S // L_SEL
    gw = 2 * LANES
    return pl.pallas_call(
        functools.partial(_cmp_kernel, tq=tq, n_sel=n_sel, top_n=min(TOP_N, n_sel)),
        grid=(B, G_B, nq),
        in_specs=[pl.BlockSpec((tq, gw), lambda b, g, i: (b * nq + i, q_blk0 // 2 + g)),
                  pl.BlockSpec((1, 1, R, LANES), lambda b, g, i: (0, b, 0, g)),
                  pl.BlockSpec((1, 1, R, LANES), lambda b, g, i: (1, b, 0, g)),
                  pl.BlockSpec((R, LANES), lambda b, g, i: (0, 0))],
        out_specs=[pl.BlockSpec((tq, gw), lambda b, g, i: (b * nq + i, g)),
                   pl.BlockSpec((tq, LANES), lambda b, g, i: (b * nq + i, g))],
        out_shape=[jax.ShapeDtypeStruct((T, G_B * gw), F32),
                   jax.ShapeDtypeStruct((T, G_B * LANES), BF16)],
        compiler_params=_cparams(("parallel", "parallel", "parallel")),
        name="nsa_cmp_topk",
    )(q, ckv, ckv, cover_dup)


def _post_kernel(aa_ref, oc_ref, os_ref, ow_ref, ac_ref, fg_ref, h_ref, x_ref, p_ref,
                 wz_ref, wa_ref, wb_ref, wc_ref, wo_ref, wg_ref, wp_ref, gn_ref,
                 *outs, last):
    def silu(z):
        return z * jax.nn.sigmoid(z)

    W = W_B
    D = x_ref.shape[1]
    mz = _dot_nt(h_ref[...], wz_ref[...])
    za = mz[:, 3 * D:3 * D + W]
    zb = mz[:, 3 * D + W:3 * D + 2 * W]
    zc = mz[:, 3 * D + 2 * W:]
    sg = jax.nn.sigmoid(fg_ref[...])
    head = lax.shift_right_logical(lax.broadcasted_iota(jnp.int32, sg.shape, 1), int(math.log2(HEAD_DIM)))
    mix = []
    for j in range(W // LANES):
        cols = slice(j * LANES, (j + 1) * LANES)
        src = H_A + N_BRANCH * (2 * j + head)
        g = [jnp.take_along_axis(sg, src + r, axis=1) for r in range(N_BRANCH)]
        mix.append(g[0] * oc_ref[:, cols] + g[1] * os_ref[:, cols] + g[2] * ow_ref[:, cols])
    ya = aa_ref[...] * silu(za)
    yb = jnp.concatenate(mix, axis=1) * silu(zb)
    yc = ac_ref[...] * silu(zc)
    pa = _dot(ya.astype(BF16), wa_ref[...])
    pb = _dot(yb.astype(BF16), wb_ref[...])
    pc = _dot(yc.astype(BF16), wc_ref[...])
    merged = (jax.nn.sigmoid(mz[:, :D]) * pa + jax.nn.sigmoid(mz[:, D:2 * D]) * pb
              + jax.nn.sigmoid(mz[:, 2 * D:3 * D]) * pc)
    x1 = x_ref[...] + _dot(merged.astype(BF16), wo_ref[...])
    gate = jax.nn.sigmoid(_dot(x1.astype(BF16), wg_ref[...]))
    x2 = x1 + gate * _dot(p_ref[...].astype(BF16), wp_ref[...])
    y = x2 * lax.rsqrt(jnp.mean(x2 * x2, axis=-1, keepdims=True) + EPS) * gn_ref[...]
    if last:
        outs[0][...] = y
    else:
        outs[0][...] = x2
        outs[1][...] = y.astype(BF16)


def _post(aa, oc, osel, ow, ac, fg, h, x, p, layer, wz, wa, wb, wc, wo, wg, wp, g_next, last):
    T, D = x.shape
    tm = min(POST_TM, T)
    W = W_A
    row = lambda i: (i, 0)
    const = lambda i: (0, 0)

    def resident(arr):
        return pl.BlockSpec(arr.shape, const, pipeline_mode=pl.Buffered(1))

    in_specs = [pl.BlockSpec((tm, W), row)] * 5 + [
        pl.BlockSpec((tm, LANES), row),
        pl.BlockSpec((tm, D), row),
        pl.BlockSpec((tm, D), row),
        pl.BlockSpec((tm, p.shape[1]), lambda i: (i + layer * (T // tm), 0)),
        pl.BlockSpec((None,) + wz.shape[1:], lambda i: (layer, 0, 0), pipeline_mode=pl.Buffered(1)),
        resident(wa), resident(wb), resident(wc), resident(wo), resident(wg),
        resident(wp), pl.BlockSpec((1, D), const)]
    if last:
        out_specs = [pl.BlockSpec((tm, D), row)]
        out_shape = [jax.ShapeDtypeStruct((T, D), F32)]
    else:
        out_specs = [pl.BlockSpec((tm, D), row), pl.BlockSpec((tm, D), row)]
        out_shape = [jax.ShapeDtypeStruct((T, D), F32), jax.ShapeDtypeStruct((T, D), BF16)]
    return pl.pallas_call(
        functools.partial(_post_kernel, last=last),
        grid=(T // tm,),
        in_specs=in_specs,
        out_specs=out_specs,
        out_shape=out_shape,
        compiler_params=_cparams(("parallel",)),
        name="post",
    )(aa, oc, osel, ow, ac, fg, h, x, p, wz, wa, wb, wc, wo, wg, wp, g_next)


def _cover_matrix(R, n_sel):
    c_start = np.arange(R)[:, None] * CMP_STRIDE
    j_start = np.arange(HEAD_DIM)[None, :] * L_SEL
    cov = ((c_start < j_start + L_SEL) & (c_start + L_CMP > j_start)
           & (np.arange(HEAD_DIM)[None, :] < n_sel)).astype(np.float32)
    return jnp.asarray(np.concatenate([cov, cov], axis=1), BF16)


_SEG_OFFS = np.cumsum((0,) + SPLIT_SIZES)
(_QA, _KA, _VA, _FA, _ZA, _QB, _KCB, _VCB, _KSB, _VSB, _KWB, _VWB, _GB, _ZB, _QC, _KC, _VC, _ZC,
 _MG) = range(len(SPLIT_SIZES))
W1_COLS = 5 * W_A + 2 * 2 * KV_B
W2_COLS = 3 * W_A + 2 * 2 * KV_B
W3_COLS = LANES + 2 * KV_B
WZ_COLS = N_BRANCH * D_MODEL + 3 * W_A


def _pack_kernel(w_ref, w1_ref, w2_ref, w3_ref, wz_ref):
    qs = SCALE * LOG2E

    def seg32(k, scale=None, lo=0, hi=None):
        a = int(_SEG_OFFS[k]) + lo
        b = int(_SEG_OFFS[k + 1]) if hi is None else int(_SEG_OFFS[k]) + hi
        x = w_ref[0, a:b, :]
        return x if scale is None else x * scale

    def seg(k, scale=None, lo=0, hi=None):
        return seg32(k, scale, lo, hi).astype(BF16)

    def dup(k):
        g0, g1 = seg(k, hi=HEAD_DIM), seg(k, lo=HEAD_DIM)
        return [g0, g0, g1, g1]

    def put(o_ref, pieces):
        c = 0
        for x in pieces:
            o_ref[0, c:c + x.shape[0], :] = x
            c += x.shape[0]
        assert c == o_ref.shape[1]

    cols = w_ref.shape[2]
    put(w1_ref, [seg(_QA, qs), seg(_KA), seg(_VA), seg(_QB, SCALE), seg(_VC)] + dup(_VSB) + dup(_VWB))
    put(w2_ref, [seg(_QB, qs), seg(_QC, qs), seg(_KC)] + dup(_KWB) + dup(_KSB))
    gates = jnp.concatenate([seg32(_FA), seg32(_GB), jnp.zeros((LANES - H_A - 3 * H_B, cols), F32)], axis=0)
    put(w3_ref, [gates.astype(BF16), seg(_KCB), seg(_VCB)])
    put(wz_ref, [seg(_MG), seg(_ZA), seg(_ZB), seg(_ZC)])


def _pack_weights(w_in, tc=256):
    depth, D, N = w_in.shape
    wt = jnp.swapaxes(w_in, 1, 2)
    widths = (W1_COLS, W2_COLS, W3_COLS, WZ_COLS)
    return pl.pallas_call(
        _pack_kernel,
        grid=(depth, D // tc),
        in_specs=[pl.BlockSpec((1, N, tc), lambda l, i: (l, 0, i))],
        out_specs=[pl.BlockSpec((1, n, tc), lambda l, i: (l, 0, i)) for n in widths],
        out_shape=[jax.ShapeDtypeStruct((depth, n, D), BF16) for n in widths],
        compiler_params=_cparams(("parallel", "parallel")),
        name="pack_weights",
    )(wt)


O1_QA, O1_KA, O1_VA, O1_QBU, O1_VC, O1_VS, O1_VW = 0, 4, 8, 12, 16, 20, 22
O2_QBR, O2_QC, O2_KC, O2_KW, O2_KS = 0, 4, 8, 12, 14


def kernel(x, p, positions, norm_g, w_in, b_forget, cmp_pe_k, cmp_w1_k, cmp_b1_k, cmp_w2_k,
           cmp_pe_v, cmp_w1_v, cmp_b1_v, cmp_w2_v, diff_lam, diff_subln_g,
           w_br_a, w_br_b, w_br_c, w_out, w_ple, w_ple_gate, final_g):
    B, S, D = x.shape
    depth = w_in.shape[0]
    T = B * S
    R = S // CMP_STRIDE
    n_sel = S // L_SEL
    assert n_sel <= HEAD_DIM and S % min(TQ, S) == 0

    xf = x.reshape(T, D)
    cos_t, sin_t = _rope_tables(positions.astype(F32).reshape(T))
    cover = _cover_matrix(R, n_sel)
    h = _rmsnorm(xf, norm_g[0], BF16)
    w1, w2, w3, wz = _pack_weights(w_in)

    for i in range(depth):
        o1 = _proj(h, w1, i, BF16, 1024, name="proj_plain")
        o2 = _proj(h, w2, i, BF16, 1024, rope_tabs=(cos_t, sin_t), name="proj_rope")
        fg, kv4 = _proj_f32(h, w3, i)

        b_pad = jnp.pad(b_forget[i], (0, LANES - H_A)).reshape(1, LANES)
        ck = _forget_cumsum(fg, b_pad, B, S).reshape(B, H_A // 2, 2, S)
        att_a = _flash("fox", B, S, H_A // 2, o1, O1_QA, o1, lambda h_: O1_KA + h_,
                       o1, lambda h_: O1_VA + h_, ck=ck)

        lam_init = 0.8 - 0.6 * math.exp(-0.3 * i)
        att_c = _flash("diff", B, S, H_C, o2, O2_QC, o2, lambda h_: O2_KC + h_,
                       o1, lambda h_: O1_VC + h_, lam=diff_lam[i],
                       subln_g=diff_subln_g[i].reshape(1, LANES), lam_init=lam_init)

        tkv = kv4.reshape(2 * G_B, B, R, CMP_STRIDE * HEAD_DIM)
        pe =jnp.stack([cmp_pe_k[i].reshape(1, -1), cmp_pe_v[i].reshape(1, -1)])
        cw1 = jnp.stack([cmp_w1_k[i], cmp_w1_v[i]]).astype(BF16)
        cb1 = jnp.stack([cmp_b1_k[i].reshape(1, -1), cmp_b1_v[i].reshape(1, -1)])
        cw2 = jnp.stack([cmp_w2_k[i], cmp_w2_v[i]])
        cw2 = jnp.concatenate([cw2, cw2], axis=-1).astype(BF16)
        ckv = _compress(tkv, pe, cw1, cb1, cw2)
        o_cmp, mm = _cmp_topk(B, S, o1, O1_QBU, ckv, cover)
        o_sel = _flash("sel", B, S, H_B // 2, o2, O2_QBR, o2, lambda h_: O2_KS + h_ // 2,
                       o1, lambda h_: O1_VS + h_ // 2, mm=mm)
        o_win = _flash("win", B, S, H_B // 2, o2, O2_QBR, o2, lambda h_: O2_KW + h_ // 2,
                       o1, lambda h_: O1_VW + h_ // 2)

        last = i == depth - 1
        g_next = (final_g if last else norm_g[i + 1]).reshape(1, D)
        res = _post(att_a, o_cmp, o_sel, o_win, att_c, fg, h, xf, p.reshape(depth * T, -1), i,
                    wz, w_br_a[i].astype(BF16), w_br_b[i].astype(BF16), w_br_c[i].astype(BF16),
                    w_out[i].astype(BF16), w_ple_gate[i].astype(BF16), w_ple[i].astype(BF16),
                    g_next, last)
        if last:
            return res[0].reshape(B, S, D)
        xf, h = res
```

```python
import functools
import math

import numpy as np
import jax
import jax.numpy as jnp
from jax import lax
from jax.experimental import pallas as pl
from jax.experimental.pallas import tpu as pltpu

F32 = jnp.float32
BF16 = jnp.bfloat16

LANES = 128
HEAD_DIM = 64
NEG = -1e30
BIG = 1e30
LOWEST = -2.0 ** 127
MARKED = -2.0 ** 120
EPS = 1e-6
ROPE_THETA = 10000.0
H_A = 8
H_B = 8
G_B = 2
H_C = 4
L_CMP = 32
CMP_STRIDE = 16
CMP_HIDDEN = 256
L_SEL = 64
TOP_N = 16
WINDOW = 512
N_BRANCH = 3
D_MODEL = 1024
W_A = H_A * HEAD_DIM
W_B = H_B * HEAD_DIM
W_C = H_C * 2 * HEAD_DIM
KV_B = G_B * HEAD_DIM
SPLIT_SIZES = (W_A, W_A, W_A, H_A, W_A,
               W_B, KV_B, KV_B, KV_B, KV_B, KV_B, KV_B, 3 * H_B, W_B,
               2 * H_C * HEAD_DIM, 2 * H_C * HEAD_DIM, W_C, W_C,
               N_BRANCH * D_MODEL)
SCALE = HEAD_DIM ** -0.5
LOG2E = math.log2(math.e)

VMEM_LIMIT = 48 * 1024 * 1024

TQ = 1024
CMP_TQ = 512
FLASH_ROWS = 1024
PROJ_TM = 1024
POST_TM = 256


def _cparams(sem):
    return pltpu.CompilerParams(dimension_semantics=sem, vmem_limit_bytes=VMEM_LIMIT)


def _dot(a, b):
    return jnp.dot(a, b, preferred_element_type=F32)


def _dot_nt(a, b):
    return lax.dot_general(a, b, (((1,), (1,)), ((), ())), preferred_element_type=F32)


def _split3(x):
    hi = x.astype(BF16)
    r1 = x - hi.astype(F32)
    mid = r1.astype(BF16)
    lo = (r1 - mid.astype(F32)).astype(BF16)
    return hi, mid, lo


def _dot_exact01(x, m01):
    hi, mid, lo = _split3(x)
    return _dot(hi, m01) + _dot(mid, m01) + _dot(lo, m01)


def _rmsnorm_kernel(x_ref, g_ref, o_ref):
    x = x_ref[...]
    y = x * lax.rsqrt(jnp.mean(x * x, axis=-1, keepdims=True) + EPS)
    o_ref[...] = (y * g_ref[...]).astype(o_ref.dtype)


def _rmsnorm(x, g, out_dtype, tm=1024):
    T, D = x.shape
    return pl.pallas_call(
        _rmsnorm_kernel,
        grid=(T // tm,),
        in_specs=[pl.BlockSpec((tm, D), lambda i: (i, 0)),
                  pl.BlockSpec((1, D), lambda i: (0, 0))],
        out_specs=pl.BlockSpec((tm, D), lambda i: (i, 0)),
        out_shape=jax.ShapeDtypeStruct((T, D), out_dtype),
        compiler_params=_cparams(("parallel",)),
        name="rmsnorm",
    )(x, g.reshape(1, D))


def _rope_table_kernel(pos_ref, invf_ref, sign_ref, cos_ref, sin_ref):
    ang = pos_ref[...] * invf_ref[...]
    cos_ref[...] = jnp.cos(ang)
    sin_ref[...] = jnp.sin(ang) * sign_ref[...]


def _rope_tables(pos_f32, tm=512):
    T = pos_f32.shape[0]
    half = HEAD_DIM // 2
    inv_freq = ROPE_THETA ** (-jnp.arange(half, dtype=F32) / half)
    invf = jnp.tile(inv_freq, LANES // half).reshape(1, LANES)
    sign = jnp.tile(jnp.concatenate([-jnp.ones((half,), F32), jnp.ones((half,), F32)]),
                    LANES // HEAD_DIM).reshape(1, LANES)
    return pl.pallas_call(
        _rope_table_kernel,
        grid=(T // tm,),
        in_specs=[pl.BlockSpec((tm, 1), lambda i: (i, 0)),
                  pl.BlockSpec((1, LANES), lambda i: (0, 0)),
                  pl.BlockSpec((1, LANES), lambda i: (0, 0))],
        out_specs=[pl.BlockSpec((tm, LANES), lambda i: (i, 0)),
                   pl.BlockSpec((tm, LANES), lambda i: (i, 0))],
        out_shape=[jax.ShapeDtypeStruct((T, LANES), F32),
                   jax.ShapeDtypeStruct((T, LANES), F32)],
        compiler_params=_cparams(("parallel",)),
        name="rope_table",
    )(pos_f32.reshape(T, 1), invf, sign)


def _proj_kernel(h_ref, w_ref, *rest, rope):
    acc = _dot_nt(h_ref[...], w_ref[...])
    if rope:
        cos_ref, sin_ref, o_ref = rest
        cos = cos_ref[...]
        sin = sin_ref[...]
        lane = lax.broadcasted_iota(jnp.int32, cos.shape, 1)
        first = (lane & (HEAD_DIM - 1)) < (HEAD_DIM // 2)
        for c in range(acc.shape[1] // LANES):
            t = acc[:, c * LANES:(c + 1) * LANES]
            sw = jnp.where(first, pltpu.roll(t, LANES - HEAD_DIM // 2, 1),
                           pltpu.roll(t, HEAD_DIM // 2, 1))
            o_ref[:, c * LANES:(c + 1) * LANES] = (t * cos + sw * sin).astype(o_ref.dtype)
    else:
        (o_ref,) = rest
        o_ref[...] = acc.astype(o_ref.dtype)


def _proj_f32_kernel(h_ref, w_ref, fg_ref, kv_ref, x_sc):
    acc = _dot_nt(h_ref[...], w_ref[...])
    fg_ref[...] = acc[:, :LANES]
    rows = kv_ref.shape[1]
    for c in range(G_B):
        x_sc[c] = acc[:, LANES * (1 + c):LANES * (2 + c)]
    for c in range(2 * G_B):
        for l in range(CMP_STRIDE):
            x = x_sc[c // 2, pl.ds(l, rows, stride=CMP_STRIDE), :]
            kv_ref[c, :, l * HEAD_DIM:(l + 1) * HEAD_DIM] = x[:, (c % 2) * HEAD_DIM:(c % 2 + 1) * HEAD_DIM]


def _proj_f32(h, w, layer):
    T, D = h.shape
    tm = min(PROJ_TM, T)
    return pl.pallas_call(
        _proj_f32_kernel,
        grid=(T // tm,),
        in_specs=[pl.BlockSpec((tm, D), lambda i: (i, 0)),
                  pl.BlockSpec((None,) + w.shape[1:], lambda i: (layer, 0, 0))],
        out_specs=[pl.BlockSpec((tm, LANES), lambda i: (i, 0)),
                   pl.BlockSpec((2 * G_B, tm // CMP_STRIDE, CMP_STRIDE * HEAD_DIM), lambda i: (0, i, 0))],
        out_shape=[jax.ShapeDtypeStruct((T, LANES), F32),
                   jax.ShapeDtypeStruct((2 * G_B, T // CMP_STRIDE, CMP_STRIDE * HEAD_DIM), F32)],
        scratch_shapes=[pltpu.VMEM((G_B, tm, LANES), F32)],
        compiler_params=_cparams(("parallel",)),
        name="proj_f32",
    )(h, w)


def _proj(h, w, layer, out_dtype, tn, rope_tabs=None, name="proj"):
    T, D = h.shape
    N = w.shape[1]
    tm = min(PROJ_TM, T)
    assert T % tm == 0 and N % tn == 0
    in_specs = [pl.BlockSpec((tm, D), lambda i, j: (i, 0)),
                pl.BlockSpec((None, tn, D), lambda i, j: (layer, j, 0))]
    args = [h, w]
    if rope_tabs is not None:
        in_specs += [pl.BlockSpec((tm, LANES), lambda i, j: (i, 0)),
                     pl.BlockSpec((tm, LANES), lambda i, j: (i, 0))]
        args += list(rope_tabs)
    return pl.pallas_call(
        functools.partial(_proj_kernel, rope=rope_tabs is not None),
        grid=(T // tm, N // tn),
        in_specs=in_specs,
        out_specs=pl.BlockSpec((tm, tn), lambda i, j: (i, j)),
        out_shape=jax.ShapeDtypeStruct((T, N), out_dtype),
        compiler_params=_cparams(("parallel", "arbitrary")),
        name=name,
    )(*args)


def _cumsum_kernel(fg_ref, b_ref, c_ref, carry_sc, *, tiles_per_seq):
    i = pl.program_id(0)

    @pl.when(i % tiles_per_seq == 0)
    def _():
        carry_sc[...] = jnp.zeros_like(carry_sc)

    z = fg_ref[...] + b_ref[...]
    logf = jnp.minimum(z, 0.0) - jnp.log1p(jnp.exp(-jnp.abs(z)))
    tm = z.shape[0]
    r = lax.broadcasted_iota(jnp.int32, (tm, tm), 0)
    c = lax.broadcasted_iota(jnp.int32, (tm, tm), 1)
    tri = jnp.where(c <= r, 1.0, 0.0).astype(BF16)
    cs = _dot_exact01_left(tri, logf) + carry_sc[...]
    carry_sc[...] = cs[tm - 1:tm, :]
    c_ref[0] = jnp.transpose(cs * LOG2E)[:H_A, :]


def _dot_exact01_left(m01, x):
    hi, mid, lo = _split3(x)
    return _dot(m01, hi) + _dot(m01, mid) + _dot(m01, lo)


def _forget_cumsum(fg, b_pad, B, S, tm=512):
    tps = S // tm
    return pl.pallas_call(
        functools.partial(_cumsum_kernel, tiles_per_seq=tps),
        grid=(B * tps,),
        in_specs=[pl.BlockSpec((tm, LANES), lambda i: (i, 0)),
                  pl.BlockSpec((1, LANES), lambda i: (0, 0))],
        out_specs=pl.BlockSpec((1, H_A, tm), lambda i: (i // tps, 0, i % tps)),
        out_shape=jax.ShapeDtypeStruct((B, H_A, S), F32),
        scratch_shapes=[pltpu.VMEM((1, LANES), F32)],
        compiler_params=_cparams(("arbitrary",)),
        name="forget_cumsum",
    )(fg, b_pad)


def _flash_kernel(*refs, mode, tq, tw, lam_init):
    it = iter(refs)
    q_ref = next(it)
    mm_ref = next(it) if mode == "sel" else None
    k_ref = next(it)
    v_ref = next(it)
    if mode == "fox":
        ck_ref = next(it)
    if mode == "diff":
        lam_ref = next(it)
        g_ref = next(it)
    o_ref = next(it)
    m_sc = next(it)
    acc_sc = next(it)
    v1_sc = next(it)
    s0_sc = next(it)
    s1_sc = next(it)
    s2_sc = next(it)
    if mode == "sel":
        ke_sc = next(it)

    qi = pl.program_id(2)
    S = k_ref.shape[0]
    lane = lax.broadcasted_iota(jnp.int32, (tq, LANES), 1)
    half0 = lane < HEAD_DIM

    @pl.when(qi == 0)
    def _():
        v1_sc[:, :LANES] = v_ref[...]
        v1_sc[:, LANES:] = jnp.ones((S, LANES), BF16)
        if mode == "sel":
            row = lax.broadcasted_iota(jnp.int32, (S, LANES), 0)
            ln = lax.broadcasted_iota(jnp.int32, (S, LANES), 1)
            blk = lax.shift_right_logical(row, int(math.log2(L_SEL)))
            ke_sc[:, :LANES] = k_ref[...]
            ke_sc[:, LANES:] = jnp.where(ln == blk, BIG, 0.0).astype(BF16)

    kk = ke_sc if mode == "sel" else k_ref

    def split_halves(qr, mr):
        q2 = qr[...].astype(F32)
        zero = jnp.zeros_like(q2)
        qa = jnp.where(half0, q2, zero).astype(BF16)
        qb = jnp.where(half0, zero, q2).astype(BF16)
        if mode == "sel":
            mmh = jnp.where(half0, mr[...].astype(F32), zero).astype(BF16)
            qa = jnp.concatenate([qa, mmh], axis=1)
            qb = jnp.concatenate([qb, mmh], axis=1)
        return qa, qb

    qm = split_halves(q_ref, mm_ref)

    m_sc[...] = jnp.full_like(m_sc, NEG)
    acc_sc[...] = jnp.zeros_like(acc_sc)
    q0 = qi * tq
    rb = LANES
    nrb = tq // rb
    nch = tw // LANES
    hrb = nrb // 2
    dsq = (lax.broadcasted_iota(jnp.int32, (rb, LANES), 0)
           - lax.broadcasted_iota(jnp.int32, (rb, LANES), 1))

    def qk(start, s_sc, r_lo=0, r_hi=tq):
        koff = pl.multiple_of(start, tw)
        kt = kk[pl.ds(koff, tw), :]
        for a in range(2):
            s = _dot_nt(qm[a][r_lo:r_hi], kt)
            if mode == "fox":
                s = s - ck_ref[0, 0, a:a + 1, pl.ds(koff, tw)]
            s_sc[a, r_lo:r_hi] = s

    FULL = [("full",)] * nrb
    DIAG_A = [("diag", r) for r in range(hrb)] + [("full",)] * hrb
    DIAG_B = [None] * hrb + [("diag", r) for r in range(hrb)]
    PREV_P = [("prev", r) for r in range(hrb)] + [None] * hrb
    WIN_A = [("diag", r) for r in range(hrb)] + [("prev", r) for r in range(hrb)]

    def soft_pv(s_sc, start, kinds):
        koff = pl.multiple_of(start, tw)
        vt = v1_sc[pl.ds(koff, tw), :]
        chains = []
        i = 0
        while i < nrb:
            j = i + 1
            if kinds[i] == ("full",):
                while j < nrb and kinds[j] == ("full",) and (j - i) * rb < FLASH_ROWS:
                    j += 1
            if kinds[i] is not None:
                chains.append((slice(i * rb, j * rb), kinds[i]))
            i = j
        for a in range(2):
            for rows, kind in chains:
                if kind[0] == "full":
                    c_lo, c_hi, c_edge = 0, nch, None
                elif kind[0] == "diag":
                    c_lo, c_hi, c_edge = 0, kind[1] + 1, kind[1]
                else:
                    c_lo, c_hi, c_edge = kind[1], nch, kind[1]
                ch = [s_sc[a, rows, c * LANES:(c + 1) * LANES] for c in range(c_lo, c_hi)]
                if c_edge is not None:
                    ok = dsq >= 0 if kind[0] == "diag" else dsq < 0
                    ch[c_edge - c_lo] = jnp.where(ok, ch[c_edge - c_lo], NEG)
                mx = functools.reduce(jnp.maximum, ch)
                m_old = m_sc[a, rows]
                m_new = jnp.maximum(m_old, jnp.max(mx, axis=1, keepdims=True))
                alpha = jnp.exp2(m_old - m_new)
                p = jnp.concatenate([jnp.exp2(c_ - m_new).astype(BF16) for c_ in ch], axis=1)
                m_sc[a, rows] = m_new
                al = jnp.concatenate([alpha, alpha], axis=1)
                acc_sc[a, rows] = (al * acc_sc[a, rows]
                                   + _dot(p, vt[c_lo * LANES:c_hi * LANES]))

    if mode == "win":
        @pl.when(qi > 0)
        def _():
            qk(q0 - tw, s0_sc, 0, tq // 2)
            qk(q0, s1_sc)
            qk(q0 + tw, s2_sc, tq // 2, tq)
            soft_pv(s0_sc, q0 - tw, PREV_P)
            soft_pv(s1_sc, q0, WIN_A)
            soft_pv(s2_sc, q0 + tw, DIAG_B)

        @pl.when(qi == 0)
        def _():
            qk(q0, s1_sc)
            qk(q0 + tw, s2_sc, tq // 2, tq)
            soft_pv(s1_sc, q0, WIN_A)
            soft_pv(s2_sc, q0 + tw, DIAG_B)
    else:
        qk(0, s0_sc)

        def pair_body(i, carry):
            t0 = 2 * i * tw
            qk(t0 + tw, s1_sc)
            soft_pv(s0_sc, t0, FULL)
            qk(t0 + 2 * tw, s0_sc)
            soft_pv(s1_sc, t0 + tw, FULL)
            return carry

        lax.fori_loop(0, qi, pair_body, 0)
        qk(q0 + tw, s1_sc, tq // 2, tq)
        soft_pv(s0_sc, q0, DIAG_A)
        soft_pv(s1_sc, q0 + tw, DIAG_B)

    o0 = acc_sc[0, :, :LANES] / acc_sc[0, :, LANES:]
    o1 = acc_sc[1, :, :LANES] / acc_sc[1, :, LANES:]
    if mode == "diff":
        lq = lam_ref[...]
        lam = (jnp.exp(jnp.sum(lq[0:1] * lq[1:2], keepdims=True))
               - jnp.exp(jnp.sum(lq[2:3] * lq[3:4], keepdims=True)) + lam_init)
        d = o0 - lam * o1
        y = d * lax.rsqrt(jnp.mean(d * d, axis=-1, keepdims=True) + EPS)
        o_ref[...] = (y * g_ref[...]) * (1.0 - lam_init)
    else:
        o_ref[...] = jnp.where(half0, o0, o1)


def _flash(mode, B, S, n_blk, q, q_blk0, k, k_blk, v, v_blk, *,
           mm=None, ck=None, lam=None, subln_g=None, lam_init=0.0):
    tq = min(TQ, S)
    tw = WINDOW
    nq = S // tq
    T = B * S
    assert S % tq == 0 and tq == 2 * tw and tw % LANES == 0
    in_specs = [pl.BlockSpec((tq, LANES), lambda b, h, i: (b * nq + i, q_blk0 + h))]
    args = [q]
    if mode == "sel":
        in_specs.append(pl.BlockSpec((tq, LANES), lambda b, h, i: (b * nq + i, h // 2)))
        args.append(mm)
    in_specs.append(pl.BlockSpec((S, LANES), lambda b, h, i: (b, k_blk(h))))
    args.append(k)
    in_specs.append(pl.BlockSpec((S, LANES), lambda b, h, i: (b, v_blk(h))))
    args.append(v)
    if mode == "fox":
        in_specs.append(pl.BlockSpec((1, 1, 2, S), lambda b, h, i: (b, h, 0, 0)))
        args += [ck]
    if mode == "diff":
        in_specs.append(pl.BlockSpec(lam.shape, lambda b, h, i: (0, 0)))
        in_specs.append(pl.BlockSpec((1, LANES), lambda b, h, i: (0, 0)))
        args += [lam, subln_g]
    scratch = [pltpu.VMEM((2, tq, LANES), F32), pltpu.VMEM((2, tq, 2 * LANES), F32),
               pltpu.VMEM((S, 2 * LANES), BF16)]
    scratch += [pltpu.VMEM((2, tq, tw), F32)] * 3
    if mode == "sel":
        scratch += [pltpu.VMEM((S, 2 * LANES), BF16)]
    return pl.pallas_call(
        functools.partial(_flash_kernel, mode=mode, tq=tq, tw=tw, lam_init=lam_init),
        grid=(B, n_blk, nq),
        in_specs=in_specs,
        out_specs=pl.BlockSpec((tq, LANES), lambda b, h, i: (b * nq + i, h)),
        out_shape=jax.ShapeDtypeStruct((T, n_blk * LANES), F32),
        scratch_shapes=scratch,
        compiler_params=_cparams(("parallel", "parallel", "arbitrary")),
        name="flash_" + mode,
    )(*args)


def _compress_kernel(t_ref, pe_ref, w1_ref, b1_ref, w2_ref, o_ref):
    t = t_ref[0, 0]
    R, half = t.shape
    pe = pe_ref[0]
    xa = (t + pe[:, :half]).astype(BF16)
    xb = (t + pe[:, half:]).astype(BF16)
    a = _dot(xa, w1_ref[0, :half, :])
    b = _dot(xb, w1_ref[0, half:, :])
    hp = a + pltpu.roll(b, R - 1, 0) + b1_ref[0]
    hid = hp * jax.nn.sigmoid(hp)
    o_ref[0, 0] = _dot(hid.astype(BF16), w2_ref[0]).astype(o_ref.dtype)


def _compress(tkv, pe, w1, b1, w2dup):
    _, B, R, W = tkv.shape
    G = G_B
    return pl.pallas_call(
        _compress_kernel,
        grid=(2, B, G),
        in_specs=[pl.BlockSpec((1, 1, R, W), lambda s, b, g: (s * G_B + g, b, 0, 0)),
                  pl.BlockSpec((1, 1, 2 * W), lambda s, b, g: (s, 0, 0)),
                  pl.BlockSpec((1, 2 * W, CMP_HIDDEN), lambda s, b, g: (s, 0, 0)),
                  pl.BlockSpec((1, 1, CMP_HIDDEN), lambda s, b, g: (s, 0, 0)),
                  pl.BlockSpec((1, CMP_HIDDEN, LANES), lambda s, b, g: (s, 0, 0))],
        out_specs=pl.BlockSpec((1, 1, R, LANES), lambda s, b, g: (s, b, 0, g)),
        out_shape=jax.ShapeDtypeStruct((2, B, R, G * LANES), BF16),
        compiler_params=_cparams(("parallel", "parallel", "parallel")),
        name="nsa_compress",
    )(tkv, pe, w1, b1, w2dup)


def _cmp_kernel(q_ref, kc_ref, vc_ref, cov_ref, o_ref, mm_ref, *, tq, n_sel, top_n):
    qi = pl.program_id(2)
    kc = kc_ref[0, 0]
    vc = vc_ref[0, 0]
    R = kc.shape[0]
    lane = lax.broadcasted_iota(jnp.int32, (tq, LANES), 1)
    half0 = lane < HEAD_DIM
    t_r = lax.broadcasted_iota(jnp.int32, (tq, R), 0) + qi * tq
    c_end = lax.broadcasted_iota(jnp.int32, (tq, R), 1) * CMP_STRIDE + (L_CMP - 1)
    ok = c_end <= t_r
    psum = jnp.zeros((tq, R), F32)
    outs = []
    for pr in range(2):
        q2 = q_ref[:, pr * LANES:(pr + 1) * LANES].astype(F32)
        zero = jnp.zeros_like(q2)
        pair = []
        for a in range(2):
            qm = (jnp.where(half0, q2, zero) if a == 0 else jnp.where(half0, zero, q2)).astype(BF16)
            s = jnp.where(ok, _dot_nt(qm, kc), NEG)
            m = jnp.max(s, axis=1, keepdims=True)
            e = jnp.where(ok, jnp.exp(s - m), 0.0)
            l = jnp.sum(e, axis=1, keepdims=True)
            p = e / jnp.where(l > 0.0, l, 1.0)
            psum = psum + p
            pair.append(_dot(p.astype(BF16), vc))
        outs.append(jnp.where(half0, pair[0], pair[1]))
    o_ref[...] = jnp.concatenate(outs, axis=1)

    imp = _dot_exact01(psum, cov_ref[...])
    jl = lane & (HEAD_DIM - 1)
    t_q = lax.broadcasted_iota(jnp.int32, (tq, LANES), 0) + qi * tq
    tb = lax.shift_right_logical(t_q, int(math.log2(L_SEL)))
    valid = jl <= tb
    forced = (jl == 0) | (jl == tb) | (jl == tb - 1)
    in_range = jl < n_sel
    score = jnp.where(forced, LOWEST, jnp.where(valid, imp, -BIG))
    score = jnp.where(in_range, score, LOWEST)
    for _ in range(top_n - 3):
        idx = jnp.argmax(score, axis=1, keepdims=True).astype(jnp.int32) & (HEAD_DIM - 1)
        score = jnp.where(jl == idx, LOWEST, score)
    mm_ref[...] = jnp.where((score < MARKED) & in_range, 0.0, -1.0).astype(mm_ref.dtype)


def _cmp_topk(B, S, q, q_blk0, ckv, cover_dup):
    tq = min(CMP_TQ, S)
    nq = S // tq
    T = B * S
    R = ckv.shape[2]
    n_sel = S // L_SEL
    gw = 2 * LANES
    return pl.pallas_call(
        functools.partial(_cmp_kernel, tq=tq, n_sel=n_sel, top_n=min(TOP_N, n_sel)),
        grid=(B, G_B, nq),
        in_specs=[pl.BlockSpec((tq, gw), lambda b, g, i: (b * nq + i, q_blk0 // 2 + g)),
                  pl.BlockSpec((1, 1, R, LANES), lambda b, g, i: (0, b, 0, g)),
                  pl.BlockSpec((1, 1, R, LANES), lambda b, g, i: (1, b, 0, g)),
                  pl.BlockSpec((R, LANES), lambda b, g, i: (0, 0))],
        out_specs=[pl.BlockSpec((tq, gw), lambda b, g, i: (b * nq + i, g)),
                   pl.BlockSpec((tq, LANES), lambda b, g, i: (b * nq + i, g))],
        out_shape=[jax.ShapeDtypeStruct((T, G_B * gw), F32),
                   jax.ShapeDtypeStruct((T, G_B * LANES), BF16)],
        compiler_params=_cparams(("parallel", "parallel", "parallel")),
        name="nsa_cmp_topk",
    )(q, ckv, ckv, cover_dup)


def _post_kernel(aa_ref, oc_ref, os_ref, ow_ref, ac_ref, fg_ref, h_ref, x_ref, p_ref,
                 wz_ref, wa_ref, wb_ref, wc_ref, wo_ref, wg_ref, wp_ref, gn_ref,
                 *outs, last):
    def silu(z):
        return z * jax.nn.sigmoid(z)

    W = W_B
    D = x_ref.shape[1]
    mz = _dot_nt(h_ref[...], wz_ref[...])
    za = mz[:, 3 * D:3 * D + W]
    zb = mz[:, 3 * D + W:3 * D + 2 * W]
    zc = mz[:, 3 * D + 2 * W:]
    sg = jax.nn.sigmoid(fg_ref[...])
    head = lax.shift_right_logical(lax.broadcasted_iota(jnp.int32, sg.shape, 1), int(math.log2(HEAD_DIM)))
    mix = []
    for j in range(W // LANES):
        cols = slice(j * LANES, (j + 1) * LANES)
        src = H_A + N_BRANCH * (2 * j + head)
        g = [jnp.take_along_axis(sg, src + r, axis=1) for r in range(N_BRANCH)]
        mix.append(g[0] * oc_ref[:, cols] + g[1] * os_ref[:, cols] + g[2] * ow_ref[:, cols])
    ya = aa_ref[...] * silu(za)
    yb = jnp.concatenate(mix, axis=1) * silu(zb)
    yc = ac_ref[...] * silu(zc)
    pa = _dot(ya.astype(BF16), wa_ref[...])
    pb = _dot(yb.astype(BF16), wb_ref[...])
    pc = _dot(yc.astype(BF16), wc_ref[...])
    merged = (jax.nn.sigmoid(mz[:, :D]) * pa + jax.nn.sigmoid(mz[:, D:2 * D]) * pb
              + jax.nn.sigmoid(mz[:, 2 * D:3 * D]) * pc)
    x1 = x_ref[...] + _dot(merged.astype(BF16), wo_ref[...])
    gate = jax.nn.sigmoid(_dot(x1.astype(BF16), wg_ref[...]))
    x2 = x1 + gate * _dot(p_ref[...].astype(BF16), wp_ref[...])
    y = x2 * lax.rsqrt(jnp.mean(x2 * x2, axis=-1, keepdims=True) + EPS) * gn_ref[...]
    if last:
        outs[0][...] = y
    else:
        outs[0][...] = x2
        outs[1][...] = y.astype(BF16)


def _post(aa, oc, osel, ow, ac, fg, h, x, p, layer, wz, wa, wb, wc, wo, wg, wp, g_next, last):
    T, D = x.shape
    tm = min(POST_TM, T)
    W = W_A
    row = lambda i: (i, 0)
    const = lambda i: (0, 0)

    def resident(arr):
        return pl.BlockSpec(arr.shape, const, pipeline_mode=pl.Buffered(1))

    in_specs = [pl.BlockSpec((tm, W), row)] * 5 + [
        pl.BlockSpec((tm, LANES), row),
        pl.BlockSpec((tm, D), row),
        pl.BlockSpec((tm, D), row),
        pl.BlockSpec((tm, p.shape[1]), lambda i: (i + layer * (T // tm), 0)),
        pl.BlockSpec((None,) + wz.shape[1:], lambda i: (layer, 0, 0), pipeline_mode=pl.Buffered(1)),
        resident(wa), resident(wb), resident(wc), resident(wo), resident(wg),
        resident(wp), pl.BlockSpec((1, D), const)]
    if last:
        out_specs = [pl.BlockSpec((tm, D), row)]
        out_shape = [jax.ShapeDtypeStruct((T, D), F32)]
    else:
        out_specs = [pl.BlockSpec((tm, D), row), pl.BlockSpec((tm, D), row)]
        out_shape = [jax.ShapeDtypeStruct((T, D), F32), jax.ShapeDtypeStruct((T, D), BF16)]
    return pl.pallas_call(
        functools.partial(_post_kernel, last=last),
        grid=(T // tm,),
        in_specs=in_specs,
        out_specs=out_specs,
        out_shape=out_shape,
        compiler_params=_cparams(("parallel",)),
        name="post",
    )(aa, oc, osel, ow, ac, fg, h, x, p, wz, wa, wb, wc, wo, wg, wp, g_next)


def _cover_matrix(R, n_sel):
    c_start = np.arange(R)[:, None] * CMP_STRIDE
    j_start = np.arange(HEAD_DIM)[None, :] * L_SEL
    cov = ((c_start < j_start + L_SEL) & (c_start + L_CMP > j_start)
           & (np.arange(HEAD_DIM)[None, :] < n_sel)).astype(np.float32)
    return jnp.asarray(np.concatenate([cov, cov], axis=1), BF16)


_SEG_OFFS = np.cumsum((0,) + SPLIT_SIZES)
(_QA, _KA, _VA, _FA, _ZA, _QB, _KCB, _VCB, _KSB, _VSB, _KWB, _VWB, _GB, _ZB, _QC, _KC, _VC, _ZC,
 _MG) = range(len(SPLIT_SIZES))
W1_COLS = 5 * W_A + 2 * 2 * KV_B
W2_COLS = 3 * W_A + 2 * 2 * KV_B
W3_COLS = LANES + 2 * KV_B
WZ_COLS = N_BRANCH * D_MODEL + 3 * W_A


def _pack_kernel(w_ref, w1_ref, w2_ref, w3_ref, wz_ref):
    qs = SCALE * LOG2E

    def seg32(k, scale=None, lo=0, hi=None):
        a = int(_SEG_OFFS[k]) + lo
        b = int(_SEG_OFFS[k + 1]) if hi is None else int(_SEG_OFFS[k]) + hi
        x = w_ref[0, a:b, :]
        return x if scale is None else x * scale

    def seg(k, scale=None, lo=0, hi=None):
        return seg32(k, scale, lo, hi).astype(BF16)

    def dup(k):
        g0, g1 = seg(k, hi=HEAD_DIM), seg(k, lo=HEAD_DIM)
        return [g0, g0, g1, g1]

    def put(o_ref, pieces):
        c = 0
        for x in pieces:
            o_ref[0, c:c + x.shape[0], :] = x
            c += x.shape[0]
        assert c == o_ref.shape[1]

    cols = w_ref.shape[2]
    put(w1_ref, [seg(_QA, qs), seg(_KA), seg(_VA), seg(_QB, SCALE), seg(_VC)] + dup(_VSB) + dup(_VWB))
    put(w2_ref, [seg(_QB, qs), seg(_QC, qs), seg(_KC)] + dup(_KWB) + dup(_KSB))
    gates = jnp.concatenate([seg32(_FA), seg32(_GB), jnp.zeros((LANES - H_A - 3 * H_B, cols), F32)], axis=0)
    put(w3_ref, [gates.astype(BF16), seg(_KCB), seg(_VCB)])
    put(wz_ref, [seg(_MG), seg(_ZA), seg(_ZB), seg(_ZC)])


def _pack_weights(w_in, tc=256):
    depth, D, N = w_in.shape
    wt = jnp.swapaxes(w_in, 1, 2)
    widths = (W1_COLS, W2_COLS, W3_COLS, WZ_COLS)
    return pl.pallas_call(
        _pack_kernel,
        grid=(depth, D // tc),
        in_specs=[pl.BlockSpec((1, N, tc), lambda l, i: (l, 0, i))],
        out_specs=[pl.BlockSpec((1, n, tc), lambda l, i: (l, 0, i)) for n in widths],
        out_shape=[jax.ShapeDtypeStruct((depth, n, D), BF16) for n in widths],
        compiler_params=_cparams(("parallel", "parallel")),
        name="pack_weights",
    )(wt)


O1_QA, O1_KA, O1_VA, O1_QBU, O1_VC, O1_VS, O1_VW = 0, 4, 8, 12, 16, 20, 22
O2_QBR, O2_QC, O2_KC, O2_KW, O2_KS = 0, 4, 8, 12, 14


def kernel(x, p, positions, norm_g, w_in, b_forget, cmp_pe_k, cmp_w1_k, cmp_b1_k, cmp_w2_k,
           cmp_pe_v, cmp_w1_v, cmp_b1_v, cmp_w2_v, diff_lam, diff_subln_g,
           w_br_a, w_br_b, w_br_c, w_out, w_ple, w_ple_gate, final_g):
    B, S, D = x.shape
    depth = w_in.shape[0]
    T = B * S
    R = S // CMP_STRIDE
    n_sel = S // L_SEL
    assert n_sel <= HEAD_DIM and S % min(TQ, S) == 0

    xf = x.reshape(T, D)
    cos_t, sin_t = _rope_tables(positions.astype(F32).reshape(T))
    cover = _cover_matrix(R, n_sel)
    h = _rmsnorm(xf, norm_g[0], BF16)
    w1, w2, w3, wz = _pack_weights(w_in)

    for i in range(depth):
        o1 = _proj(h, w1, i, BF16, 1024, name="proj_plain")
        o2 = _proj(h, w2, i, BF16, 1024, rope_tabs=(cos_t, sin_t), name="proj_rope")
        fg, kv4 = _proj_f32(h, w3, i)

        b_pad = jnp.pad(b_forget[i], (0, LANES - H_A)).reshape(1, LANES)
        ck = _forget_cumsum(fg, b_pad, B, S).reshape(B, H_A // 2, 2, S)
        att_a = _flash("fox", B, S, H_A // 2, o1, O1_QA, o1, lambda h_: O1_KA + h_,
                       o1, lambda h_: O1_VA + h_, ck=ck)

        lam_init = 0.8 - 0.6 * math.exp(-0.3 * i)
        att_c = _flash("diff", B, S, H_C, o2, O2_QC, o2, lambda h_: O2_KC + h_,
                       o1, lambda h_: O1_VC + h_, lam=diff_lam[i],
                       subln_g=diff_subln_g[i].reshape(1, LANES), lam_init=lam_init)

        tkv = kv4.reshape(2 * G_B, B, R, CMP_STRIDE * HEAD_DIM)
        pe =jnp.stack([cmp_pe_k[i].reshape(1, -1), cmp_pe_v[i].reshape(1, -1)])
        cw1 = jnp.stack([cmp_w1_k[i], cmp_w1_v[i]]).astype(BF16)
        cb1 = jnp.stack([cmp_b1_k[i].reshape(1, -1), cmp_b1_v[i].reshape(1, -1)])
        cw2 = jnp.stack([cmp_w2_k[i], cmp_w2_v[i]])
        cw2 = jnp.concatenate([cw2, cw2], axis=-1).astype(BF16)
        ckv = _compress(tkv, pe, cw1, cb1, cw2)
        o_cmp, mm = _cmp_topk(B, S, o1, O1_QBU, ckv, cover)
        o_sel = _flash("sel", B, S, H_B // 2, o2, O2_QBR, o2, lambda h_: O2_KS + h_ // 2,
                       o1, lambda h_: O1_VS + h_ // 2, mm=mm)
        o_win = _flash("win", B, S, H_B // 2, o2, O2_QBR, o2, lambda h_: O2_KW + h_ // 2,
                       o1, lambda h_: O1_VW + h_ // 2)

        last = i == depth - 1
        g_next = (final_g if last else norm_g[i + 1]).reshape(1, D)
        res = _post(att_a, o_cmp, o_sel, o_win, att_c, fg, h, xf, p.reshape(depth * T, -1), i,
                    wz, w_br_a[i].astype(BF16), w_br_b[i].astype(BF16), w_br_c[i].astype(BF16),
                    w_out[i].astype(BF16), w_ple_gate[i].astype(BF16), w_ple[i].astype(BF16),
                    g_next, last)
        if last:
            return res[0].reshape(B, S, D)
        xf, h = res
```

```python
import functools
import math

import numpy as np
import jax
import jax.numpy as jnp
from jax import lax
from jax.experimental import pallas as pl
from jax.experimental.pallas import tpu as pltpu

F32 = jnp.float32
BF16 = jnp.bfloat16

LANES = 128
HEAD_DIM = 64
NEG = -1e30
BIG = 1e30
LOWEST = -2.0 ** 127
MARKED = -2.0 ** 120
EPS = 1e-6
ROPE_THETA = 10000.0
H_A = 8
H_B = 8
G_B = 2
H_C = 4
L_CMP = 32
CMP_STRIDE = 16
CMP_HIDDEN = 256
L_SEL = 64
TOP_N = 16
WINDOW = 512
N_BRANCH = 3
D_MODEL = 1024
W_A = H_A * HEAD_DIM
W_B = H_B * HEAD_DIM
W_C = H_C * 2 * HEAD_DIM
KV_B = G_B * HEAD_DIM
SPLIT_SIZES = (W_A, W_A, W_A, H_A, W_A,
               W_B, KV_B, KV_B, KV_B, KV_B, KV_B, KV_B, 3 * H_B, W_B,
               2 * H_C * HEAD_DIM, 2 * H_C * HEAD_DIM, W_C, W_C,
               N_BRANCH * D_MODEL)
SCALE = HEAD_DIM ** -0.5
LOG2E = math.log2(math.e)

VMEM_LIMIT = 48 * 1024 * 1024

TQ = 1024
CMP_TQ = 1024
FLASH_ROWS = 1024
PROJ_TM = 1024
POST_TM = 256


def _cparams(sem):
    return pltpu.CompilerParams(dimension_semantics=sem, vmem_limit_bytes=VMEM_LIMIT)


def _dot(a, b):
    return jnp.dot(a, b, preferred_element_type=F32)


def _dot_nt(a, b):
    return lax.dot_general(a, b, (((1,), (1,)), ((), ())), preferred_element_type=F32)


def _split3(x):
    hi = x.astype(BF16)
    r1 = x - hi.astype(F32)
    mid = r1.astype(BF16)
    lo = (r1 - mid.astype(F32)).astype(BF16)
    return hi, mid, lo


def _dot_exact01(x, m01):
    hi, mid, lo = _split3(x)
    return _dot(hi, m01) + _dot(mid, m01) + _dot(lo, m01)


def _rmsnorm_kernel(x_ref, g_ref, o_ref):
    x = x_ref[...]
    y = x * lax.rsqrt(jnp.mean(x * x, axis=-1, keepdims=True) + EPS)
    o_ref[...] = (y * g_ref[...]).astype(o_ref.dtype)


def _rmsnorm(x, g, out_dtype, tm=1024):
    T, D = x.shape
    return pl.pallas_call(
        _rmsnorm_kernel,
        grid=(T // tm,),
        in_specs=[pl.BlockSpec((tm, D), lambda i: (i, 0)),
                  pl.BlockSpec((1, D), lambda i: (0, 0))],
        out_specs=pl.BlockSpec((tm, D), lambda i: (i, 0)),
        out_shape=jax.ShapeDtypeStruct((T, D), out_dtype),
        compiler_params=_cparams(("parallel",)),
        name="rmsnorm",
    )(x, g.reshape(1, D))


def _rope_table_kernel(pos_ref, invf_ref, sign_ref, cos_ref, sin_ref):
    ang = pos_ref[...] * invf_ref[...]
    cos_ref[...] = jnp.cos(ang)
    sin_ref[...] = jnp.sin(ang) * sign_ref[...]


def _rope_tables(pos_f32, tm=512):
    T = pos_f32.shape[0]
    half = HEAD_DIM // 2
    inv_freq = ROPE_THETA ** (-jnp.arange(half, dtype=F32) / half)
    invf = jnp.tile(inv_freq, LANES // half).reshape(1, LANES)
    sign = jnp.tile(jnp.concatenate([-jnp.ones((half,), F32), jnp.ones((half,), F32)]),
                    LANES // HEAD_DIM).reshape(1, LANES)
    return pl.pallas_call(
        _rope_table_kernel,
        grid=(T // tm,),
        in_specs=[pl.BlockSpec((tm, 1), lambda i: (i, 0)),
                  pl.BlockSpec((1, LANES), lambda i: (0, 0)),
                  pl.BlockSpec((1, LANES), lambda i: (0, 0))],
        out_specs=[pl.BlockSpec((tm, LANES), lambda i: (i, 0)),
                   pl.BlockSpec((tm, LANES), lambda i: (i, 0))],
        out_shape=[jax.ShapeDtypeStruct((T, LANES), F32),
                   jax.ShapeDtypeStruct((T, LANES), F32)],
        compiler_params=_cparams(("parallel",)),
        name="rope_table",
    )(pos_f32.reshape(T, 1), invf, sign)


def _proj_kernel(h_ref, w_ref, *rest, rope):
    acc = _dot_nt(h_ref[...], w_ref[...])
    if rope:
        cos_ref, sin_ref, o_ref = rest
        cos = cos_ref[...]
        sin = sin_ref[...]
        lane = lax.broadcasted_iota(jnp.int32, cos.shape, 1)
        first = (lane & (HEAD_DIM - 1)) < (HEAD_DIM // 2)
        for c in range(acc.shape[1] // LANES):
            t = acc[:, c * LANES:(c + 1) * LANES]
            sw = jnp.where(first, pltpu.roll(t, LANES - HEAD_DIM // 2, 1),
                           pltpu.roll(t, HEAD_DIM // 2, 1))
            o_ref[:, c * LANES:(c + 1) * LANES] = (t * cos + sw * sin).astype(o_ref.dtype)
    else:
        (o_ref,) = rest
        o_ref[...] = acc.astype(o_ref.dtype)


def _proj_f32_kernel(h_ref, w_ref, fg_ref, kv_ref, x_sc):
    acc = _dot_nt(h_ref[...], w_ref[...])
    fg_ref[...] = acc[:, :LANES]
    rows = kv_ref.shape[1]
    for c in range(G_B):
        x_sc[c] = acc[:, LANES * (1 + c):LANES * (2 + c)]
    for c in range(2 * G_B):
        for l in range(CMP_STRIDE):
            x = x_sc[c // 2, pl.ds(l, rows, stride=CMP_STRIDE), :]
            kv_ref[c, :, l * HEAD_DIM:(l + 1) * HEAD_DIM] = x[:, (c % 2) * HEAD_DIM:(c % 2 + 1) * HEAD_DIM]


def _proj_f32(h, w, layer):
    T, D = h.shape
    tm = min(PROJ_TM, T)
    return pl.pallas_call(
        _proj_f32_kernel,
        grid=(T // tm,),
        in_specs=[pl.BlockSpec((tm, D), lambda i: (i, 0)),
                  pl.BlockSpec((None,) + w.shape[1:], lambda i: (layer, 0, 0))],
        out_specs=[pl.BlockSpec((tm, LANES), lambda i: (i, 0)),
                   pl.BlockSpec((2 * G_B, tm // CMP_STRIDE, CMP_STRIDE * HEAD_DIM), lambda i: (0, i, 0))],
        out_shape=[jax.ShapeDtypeStruct((T, LANES), F32),
                   jax.ShapeDtypeStruct((2 * G_B, T // CMP_STRIDE, CMP_STRIDE * HEAD_DIM), F32)],
        scratch_shapes=[pltpu.VMEM((G_B, tm, LANES), F32)],
        compiler_params=_cparams(("parallel",)),
        name="proj_f32",
    )(h, w)


def _proj(h, w, layer, out_dtype, tn, rope_tabs=None, name="proj"):
    T, D = h.shape
    N = w.shape[1]
    tm = min(PROJ_TM, T)
    assert T % tm == 0 and N % tn == 0
    in_specs = [pl.BlockSpec((tm, D), lambda i, j: (i, 0)),
                pl.BlockSpec((None, tn, D), lambda i, j: (layer, j, 0))]
    args = [h, w]
    if rope_tabs is not None:
        in_specs += [pl.BlockSpec((tm, LANES), lambda i, j: (i, 0)),
                     pl.BlockSpec((tm, LANES), lambda i, j: (i, 0))]
        args += list(rope_tabs)
    return pl.pallas_call(
        functools.partial(_proj_kernel, rope=rope_tabs is not None),
        grid=(T // tm, N // tn),
        in_specs=in_specs,
        out_specs=pl.BlockSpec((tm, tn), lambda i, j: (i, j)),
        out_shape=jax.ShapeDtypeStruct((T, N), out_dtype),
        compiler_params=_cparams(("parallel", "arbitrary")),
        name=name,
    )(*args)


def _cumsum_kernel(fg_ref, b_ref, c_ref, carry_sc, *, tiles_per_seq):
    i = pl.program_id(0)

    @pl.when(i % tiles_per_seq == 0)
    def _():
        carry_sc[...] = jnp.zeros_like(carry_sc)

    z = fg_ref[...] + b_ref[...]
    logf = jnp.minimum(z, 0.0) - jnp.log1p(jnp.exp(-jnp.abs(z)))
    tm = z.shape[0]
    r = lax.broadcasted_iota(jnp.int32, (tm, tm), 0)
    c = lax.broadcasted_iota(jnp.int32, (tm, tm), 1)
    tri = jnp.where(c <= r, 1.0, 0.0).astype(BF16)
    cs = _dot_exact01_left(tri, logf) + carry_sc[...]
    carry_sc[...] = cs[tm - 1:tm, :]
    c_ref[0] = jnp.transpose(cs * LOG2E)[:H_A, :]


def _dot_exact01_left(m01, x):
    hi, mid, lo = _split3(x)
    return _dot(m01, hi) + _dot(m01, mid) + _dot(m01, lo)


def _forget_cumsum(fg, b_pad, B, S, tm=512):
    tps = S // tm
    return pl.pallas_call(
        functools.partial(_cumsum_kernel, tiles_per_seq=tps),
        grid=(B * tps,),
        in_specs=[pl.BlockSpec((tm, LANES), lambda i: (i, 0)),
                  pl.BlockSpec((1, LANES), lambda i: (0, 0))],
        out_specs=pl.BlockSpec((1, H_A, tm), lambda i: (i // tps, 0, i % tps)),
        out_shape=jax.ShapeDtypeStruct((B, H_A, S), F32),
        scratch_shapes=[pltpu.VMEM((1, LANES), F32)],
        compiler_params=_cparams(("arbitrary",)),
        name="forget_cumsum",
    )(fg, b_pad)


def _flash_kernel(*refs, mode, tq, tw, lam_init):
    it = iter(refs)
    q_ref = next(it)
    mm_ref = next(it) if mode == "sel" else None
    k_ref = next(it)
    v_ref = next(it)
    if mode == "fox":
        ck_ref = next(it)
    if mode == "diff":
        lam_ref = next(it)
        g_ref = next(it)
    o_ref = next(it)
    m_sc = next(it)
    acc_sc = next(it)
    v1_sc = next(it)
    s0_sc = next(it)
    s1_sc = next(it)
    s2_sc = next(it)
    if mode == "sel":
        ke_sc = next(it)

    qi = pl.program_id(2)
    S = k_ref.shape[0]
    lane = lax.broadcasted_iota(jnp.int32, (tq, LANES), 1)
    half0 = lane < HEAD_DIM

    @pl.when(qi == 0)
    def _():
        v1_sc[:, :LANES] = v_ref[...]
        v1_sc[:, LANES:] = jnp.ones((S, LANES), BF16)
        if mode == "sel":
            row = lax.broadcasted_iota(jnp.int32, (S, LANES), 0)
            ln = lax.broadcasted_iota(jnp.int32, (S, LANES), 1)
            blk = lax.shift_right_logical(row, int(math.log2(L_SEL)))
            ke_sc[:, :LANES] = k_ref[...]
            ke_sc[:, LANES:] = jnp.where(ln == blk, BIG, 0.0).astype(BF16)

    kk = ke_sc if mode == "sel" else k_ref

    def split_halves(qr, mr):
        q2 = qr[...].astype(F32)
        zero = jnp.zeros_like(q2)
        qa = jnp.where(half0, q2, zero).astype(BF16)
        qb = jnp.where(half0, zero, q2).astype(BF16)
        if mode == "sel":
            mmh = jnp.where(half0, mr[...].astype(F32), zero).astype(BF16)
            qa = jnp.concatenate([qa, mmh], axis=1)
            qb = jnp.concatenate([qb, mmh], axis=1)
        return qa, qb

    qm = split_halves(q_ref, mm_ref)

    m_sc[...] = jnp.full_like(m_sc, NEG)
    acc_sc[...] = jnp.zeros_like(acc_sc)
    q0 = qi * tq
    rb = LANES
    nrb = tq // rb
    nch = tw // LANES
    hrb = nrb // 2
    dsq = (lax.broadcasted_iota(jnp.int32, (rb, LANES), 0)
           - lax.broadcasted_iota(jnp.int32, (rb, LANES), 1))

    def qk(start, s_sc, r_lo=0, r_hi=tq):
        koff = pl.multiple_of(start, tw)
        kt = kk[pl.ds(koff, tw), :]
        for a in range(2):
            s = _dot_nt(qm[a][r_lo:r_hi], kt)
            if mode == "fox":
                s = s - ck_ref[0, 0, a:a + 1, pl.ds(koff, tw)]
            s_sc[a, r_lo:r_hi] = s

    FULL = [("full",)] * nrb
    DIAG_A = [("diag", r) for r in range(hrb)] + [("full",)] * hrb
    DIAG_B = [None] * hrb + [("diag", r) for r in range(hrb)]
    PREV_P = [("prev", r) for r in range(hrb)] + [None] * hrb
    WIN_A = [("diag", r) for r in range(hrb)] + [("prev", r) for r in range(hrb)]

    def soft_pv(s_sc, start, kinds):
        koff = pl.multiple_of(start, tw)
        vt = v1_sc[pl.ds(koff, tw), :]
        chains = []
        i = 0
        while i < nrb:
            j = i + 1
            if kinds[i] == ("full",):
                while j < nrb and kinds[j] == ("full",) and (j - i) * rb < FLASH_ROWS:
                    j += 1
            if kinds[i] is not None:
                chains.append((slice(i * rb, j * rb), kinds[i]))
            i = j
        for a in range(2):
            for rows, kind in chains:
                if kind[0] == "full":
                    c_lo, c_hi, c_edge = 0, nch, None
                elif kind[0] == "diag":
                    c_lo, c_hi, c_edge = 0, kind[1] + 1, kind[1]
                else:
                    c_lo, c_hi, c_edge = kind[1], nch, kind[1]
                ch = [s_sc[a, rows, c * LANES:(c + 1) * LANES] for c in range(c_lo, c_hi)]
                if c_edge is not None:
                    ok = dsq >= 0 if kind[0] == "diag" else dsq < 0
                    ch[c_edge - c_lo] = jnp.where(ok, ch[c_edge - c_lo], NEG)
                mx = functools.reduce(jnp.maximum, ch)
                m_old = m_sc[a, rows]
                m_new = jnp.maximum(m_old, jnp.max(mx, axis=1, keepdims=True))
                alpha = jnp.exp2(m_old - m_new)
                p = jnp.concatenate([jnp.exp2(c_ - m_new).astype(BF16) for c_ in ch], axis=1)
                m_sc[a, rows] = m_new
                al = jnp.concatenate([alpha, alpha], axis=1)
                acc_sc[a, rows] = (al * acc_sc[a, rows]
                                   + _dot(p, vt[c_lo * LANES:c_hi * LANES]))

    if mode == "win":
        @pl.when(qi > 0)
        def _():
            qk(q0 - tw, s0_sc, 0, tq // 2)
            qk(q0, s1_sc)
            qk(q0 + tw, s2_sc, tq // 2, tq)
            soft_pv(s0_sc, q0 - tw, PREV_P)
            soft_pv(s1_sc, q0, WIN_A)
            soft_pv(s2_sc, q0 + tw, DIAG_B)

        @pl.when(qi == 0)
        def _():
            qk(q0, s1_sc)
            qk(q0 + tw, s2_sc, tq // 2, tq)
            soft_pv(s1_sc, q0, WIN_A)
            soft_pv(s2_sc, q0 + tw, DIAG_B)
    else:
        qk(0, s0_sc)

        def pair_body(i, carry):
            t0 = 2 * i * tw
            qk(t0 + tw, s1_sc)
            soft_pv(s0_sc, t0, FULL)
            qk(t0 + 2 * tw, s0_sc)
            soft_pv(s1_sc, t0 + tw, FULL)
            return carry

        lax.fori_loop(0, qi, pair_body, 0)
        qk(q0 + tw, s1_sc, tq // 2, tq)
        soft_pv(s0_sc, q0, DIAG_A)
        soft_pv(s1_sc, q0 + tw, DIAG_B)

    o0 = acc_sc[0, :, :LANES] / acc_sc[0, :, LANES:]
    o1 = acc_sc[1, :, :LANES] / acc_sc[1, :, LANES:]
    if mode == "diff":
        lq = lam_ref[...]
        lam = (jnp.exp(jnp.sum(lq[0:1] * lq[1:2], keepdims=True))
               - jnp.exp(jnp.sum(lq[2:3] * lq[3:4], keepdims=True)) + lam_init)
        d = o0 - lam * o1
        y = d * lax.rsqrt(jnp.mean(d * d, axis=-1, keepdims=True) + EPS)
        o_ref[...] = (y * g_ref[...]) * (1.0 - lam_init)
    else:
        o_ref[...] = jnp.where(half0, o0, o1)


def _flash(mode, B, S, n_blk, q, q_blk0, k, k_blk, v, v_blk, *,
           mm=None, ck=None, lam=None, subln_g=None, lam_init=0.0):
    tq = min(TQ, S)
    tw = WINDOW
    nq = S // tq
    T = B * S
    assert S % tq == 0 and tq == 2 * tw and tw % LANES == 0
    in_specs = [pl.BlockSpec((tq, LANES), lambda b, h, i: (b * nq + i, q_blk0 + h))]
    args = [q]
    if mode == "sel":
        in_specs.append(pl.BlockSpec((tq, LANES), lambda b, h, i: (b * nq + i, h // 2)))
        args.append(mm)
    in_specs.append(pl.BlockSpec((S, LANES), lambda b, h, i: (b, k_blk(h))))
    args.append(k)
    in_specs.append(pl.BlockSpec((S, LANES), lambda b, h, i: (b, v_blk(h))))
    args.append(v)
    if mode == "fox":
        in_specs.append(pl.BlockSpec((1, 1, 2, S), lambda b, h, i: (b, h, 0, 0)))
        args += [ck]
    if mode == "diff":
        in_specs.append(pl.BlockSpec(lam.shape, lambda b, h, i: (0, 0)))
        in_specs.append(pl.BlockSpec((1, LANES), lambda b, h, i: (0, 0)))
        args += [lam, subln_g]
    scratch = [pltpu.VMEM((2, tq, LANES), F32), pltpu.VMEM((2, tq, 2 * LANES), F32),
               pltpu.VMEM((S, 2 * LANES), BF16)]
    scratch += [pltpu.VMEM((2, tq, tw), F32)] * 3
    if mode == "sel":
        scratch += [pltpu.VMEM((S, 2 * LANES), BF16)]
    return pl.pallas_call(
        functools.partial(_flash_kernel, mode=mode, tq=tq, tw=tw, lam_init=lam_init),
        grid=(B, n_blk, nq),
        in_specs=in_specs,
        out_specs=pl.BlockSpec((tq, LANES), lambda b, h, i: (b * nq + i, h)),
        out_shape=jax.ShapeDtypeStruct((T, n_blk * LANES), F32),
        scratch_shapes=scratch,
        compiler_params=_cparams(("parallel", "parallel", "arbitrary")),
        name="flash_" + mode,
    )(*args)


def _compress_kernel(t_ref, pe_ref, w1_ref, b1_ref, w2_ref, o_ref):
    t = t_ref[0, 0]
    R, half = t.shape
    pe = pe_ref[0]
    xa = (t + pe[:, :half]).astype(BF16)
    xb = (t + pe[:, half:]).astype(BF16)
    a = _dot(xa, w1_ref[0, :half, :])
    b = _dot(xb, w1_ref[0, half:, :])
    hp = a + pltpu.roll(b, R - 1, 0) + b1_ref[0]
    hid = hp * jax.nn.sigmoid(hp)
    o_ref[0, 0] = _dot(hid.astype(BF16), w2_ref[0]).astype(o_ref.dtype)


def _compress(tkv, pe, w1, b1, w2dup):
    _, B, R, W = tkv.shape
    G = G_B
    return pl.pallas_call(
        _compress_kernel,
        grid=(2, B, G),
        in_specs=[pl.BlockSpec((1, 1, R, W), lambda s, b, g: (s * G_B + g, b, 0, 0)),
                  pl.BlockSpec((1, 1, 2 * W), lambda s, b, g: (s, 0, 0)),
                  pl.BlockSpec((1, 2 * W, CMP_HIDDEN), lambda s, b, g: (s, 0, 0)),
                  pl.BlockSpec((1, 1, CMP_HIDDEN), lambda s, b, g: (s, 0, 0)),
                  pl.BlockSpec((1, CMP_HIDDEN, LANES), lambda s, b, g: (s, 0, 0))],
        out_specs=pl.BlockSpec((1, 1, R, LANES), lambda s, b, g: (s, b, 0, g)),
        out_shape=jax.ShapeDtypeStruct((2, B, R, G * LANES), BF16),
        compiler_params=_cparams(("parallel", "parallel", "parallel")),
        name="nsa_compress",
    )(tkv, pe, w1, b1, w2dup)


def _cmp_kernel(q_ref, kc_ref, vc_ref, cov_ref, o_ref, mm_ref, *, tq, n_sel, top_n):
    qi = pl.program_id(2)
    kc = kc_ref[0, 0]
    vc = vc_ref[0, 0]
    R = kc.shape[0]
    lane = lax.broadcasted_iota(jnp.int32, (tq, LANES), 1)
    half0 = lane < HEAD_DIM
    t_r = lax.broadcasted_iota(jnp.int32, (tq, R), 0) + qi * tq
    c_end = lax.broadcasted_iota(jnp.int32, (tq, R), 1) * CMP_STRIDE + (L_CMP - 1)
    ok = c_end <= t_r
    psum = jnp.zeros((tq, R), F32)
    outs = []
    for pr in range(2):
        q2 = q_ref[:, pr * LANES:(pr + 1) * LANES].astype(F32)
        zero = jnp.zeros_like(q2)
        pair = []
        for a in range(2):
            qm = (jnp.where(half0, q2, zero) if a == 0 else jnp.where(half0, zero, q2)).astype(BF16)
            s = jnp.where(ok, _dot_nt(qm, kc), NEG)
            m = jnp.max(s, axis=1, keepdims=True)
            e = jnp.where(ok, jnp.exp(s - m), 0.0)
            l = jnp.sum(e, axis=1, keepdims=True)
            p = e / jnp.where(l > 0.0, l, 1.0)
            psum = psum + p
            pair.append(_dot(p.astype(BF16), vc))
        outs.append(jnp.where(half0, pair[0], pair[1]))
    o_ref[...] = jnp.concatenate(outs, axis=1)

    imp = _dot_exact01(psum, cov_ref[...])
    jl = lane & (HEAD_DIM - 1)
    t_q = lax.broadcasted_iota(jnp.int32, (tq, LANES), 0) + qi * tq
    tb = lax.shift_right_logical(t_q, int(math.log2(L_SEL)))
    valid = jl <= tb
    forced = (jl == 0) | (jl == tb) | (jl == tb - 1)
    in_range = jl < n_sel
    score = jnp.where(forced, LOWEST, jnp.where(valid, imp, -BIG))
    score = jnp.where(in_range, score, LOWEST)
    for _ in range(top_n - 3):
        idx = jnp.argmax(score, axis=1, keepdims=True).astype(jnp.int32) & (HEAD_DIM - 1)
        score = jnp.where(jl == idx, LOWEST, score)
    mm_ref[...] = jnp.where((score < MARKED) & in_range, 0.0, -1.0).astype(mm_ref.dtype)


def _cmp_topk(B, S, q, q_blk0, ckv, cover_dup):
    tq = min(CMP_TQ, S)
    nq = S // tq
    T = B * S
    R = ckv.shape[2]
    n_sel = S // L_SEL
    gw = 2 * LANES
    return pl.pallas_call(
        functools.partial(_cmp_kernel, tq=tq, n_sel=n_sel, top_n=min(TOP_N, n_sel)),
        grid=(B, G_B, nq),
        in_specs=[pl.BlockSpec((tq, gw), lambda b, g, i: (b * nq + i, q_blk0 // 2 + g)),
                  pl.BlockSpec((1, 1, R, LANES), lambda b, g, i: (0, b, 0, g)),
                  pl.BlockSpec((1, 1, R, LANES), lambda b, g, i: (1, b, 0, g)),
                  pl.BlockSpec((R, LANES), lambda b, g, i: (0, 0))],
        out_specs=[pl.BlockSpec((tq, gw), lambda b, g, i: (b * nq + i, g)),
                   pl.BlockSpec((tq, LANES), lambda b, g, i: (b * nq + i, g))],
        out_shape=[jax.ShapeDtypeStruct((T, G_B * gw), F32),
                   jax.ShapeDtypeStruct((T, G_B * LANES), BF16)],
        compiler_params=_cparams(("parallel", "parallel", "parallel")),
        name="nsa_cmp_topk",
    )(q, ckv, ckv, cover_dup)


def _post_kernel(aa_ref, oc_ref, os_ref, ow_ref, ac_ref, fg_ref, h_ref, x_ref, p_ref,
                 wz_ref, wa_ref, wb_ref, wc_ref, wo_ref, wg_ref, wp_ref, gn_ref,
                 *outs, last):
    def silu(z):
        return z * jax.nn.sigmoid(z)

    W = W_B
    D = x_ref.shape[1]
    mz = _dot_nt(h_ref[...], wz_ref[...])
    za = mz[:, 3 * D:3 * D + W]
    zb = mz[:, 3 * D + W:3 * D + 2 * W]
    zc = mz[:, 3 * D + 2 * W:]
    sg = jax.nn.sigmoid(fg_ref[...])
    head = lax.shift_right_logical(lax.broadcasted_iota(jnp.int32, sg.shape, 1), int(math.log2(HEAD_DIM)))
    mix = []
    for j in range(W // LANES):
        cols = slice(j * LANES, (j + 1) * LANES)
        src = H_A + N_BRANCH * (2 * j + head)
        g = [jnp.take_along_axis(sg, src + r, axis=1) for r in range(N_BRANCH)]
        mix.append(g[0] * oc_ref[:, cols] + g[1] * os_ref[:, cols] + g[2] * ow_ref[:, cols])
    ya = aa_ref[...] * silu(za)
    yb = jnp.concatenate(mix, axis=1) * silu(zb)
    yc = ac_ref[...] * silu(zc)
    pa = _dot(ya.astype(BF16), wa_ref[...])
    pb = _dot(yb.astype(BF16), wb_ref[...])
    pc = _dot(yc.astype(BF16), wc_ref[...])
    merged = (jax.nn.sigmoid(mz[:, :D]) * pa + jax.nn.sigmoid(mz[:, D:2 * D]) * pb
              + jax.nn.sigmoid(mz[:, 2 * D:3 * D]) * pc)
    x1 = x_ref[...] + _dot(merged.astype(BF16), wo_ref[...])
    gate = jax.nn.sigmoid(_dot(x1.astype(BF16), wg_ref[...]))
    x2 = x1 + gate * _dot(p_ref[...].astype(BF16), wp_ref[...])
    y = x2 * lax.rsqrt(jnp.mean(x2 * x2, axis=-1, keepdims=True) + EPS) * gn_ref[...]
    if last:
        outs[0][...] = y
    else:
        outs[0][...] = x2
        outs[1][...] = y.astype(BF16)


def _post(aa, oc, osel, ow, ac, fg, h, x, p, layer, wz, wa, wb, wc, wo, wg, wp, g_next, last):
    T, D = x.shape
    tm = min(POST_TM, T)
    W = W_A
    row = lambda i: (i, 0)
    const = lambda i: (0, 0)

    def resident(arr):
        return pl.BlockSpec(arr.shape, const, pipeline_mode=pl.Buffered(1))

    in_specs = [pl.BlockSpec((tm, W), row)] * 5 + [
        pl.BlockSpec((tm, LANES), row),
        pl.BlockSpec((tm, D), row),
        pl.BlockSpec((tm, D), row),
        pl.BlockSpec((tm, p.shape[1]), lambda i: (i + layer * (T // tm), 0)),
        pl.BlockSpec((None,) + wz.shape[1:], lambda i: (layer, 0, 0), pipeline_mode=pl.Buffered(1)),
        resident(wa), resident(wb), resident(wc), resident(wo), resident(wg),
        resident(wp), pl.BlockSpec((1, D), const)]
    if last:
        out_specs = [pl.BlockSpec((tm, D), row)]
        out_shape = [jax.ShapeDtypeStruct((T, D), F32)]
    else:
        out_specs = [pl.BlockSpec((tm, D), row), pl.BlockSpec((tm, D), row)]
        out_shape = [jax.ShapeDtypeStruct((T, D), F32), jax.ShapeDtypeStruct((T, D), BF16)]
    return pl.pallas_call(
        functools.partial(_post_kernel, last=last),
        grid=(T // tm,),
        in_specs=in_specs,
        out_specs=out_specs,
        out_shape=out_shape,
        compiler_params=_cparams(("parallel",)),
        name="post",
    )(aa, oc, osel, ow, ac, fg, h, x, p, wz, wa, wb, wc, wo, wg, wp, g_next)


def _cover_matrix(R, n_sel):
    c_start = np.arange(R)[:, None] * CMP_STRIDE
    j_start = np.arange(HEAD_DIM)[None, :] * L_SEL
    cov = ((c_start < j_start + L_SEL) & (c_start + L_CMP > j_start)
           & (np.arange(HEAD_DIM)[None, :] < n_sel)).astype(np.float32)
    return jnp.asarray(np.concatenate([cov, cov], axis=1), BF16)


_SEG_OFFS = np.cumsum((0,) + SPLIT_SIZES)
(_QA, _KA, _VA, _FA, _ZA, _QB, _KCB, _VCB, _KSB, _VSB, _KWB, _VWB, _GB, _ZB, _QC, _KC, _VC, _ZC,
 _MG) = range(len(SPLIT_SIZES))
W1_COLS = 5 * W_A + 2 * 2 * KV_B
W2_COLS = 3 * W_A + 2 * 2 * KV_B
W3_COLS = LANES + 2 * KV_B
WZ_COLS = N_BRANCH * D_MODEL + 3 * W_A


def _pack_kernel(w_ref, w1_ref, w2_ref, w3_ref, wz_ref):
    qs = SCALE * LOG2E

    def seg32(k, scale=None, lo=0, hi=None):
        a = int(_SEG_OFFS[k]) + lo
        b = int(_SEG_OFFS[k + 1]) if hi is None else int(_SEG_OFFS[k]) + hi
        x = w_ref[0, a:b, :]
        return x if scale is None else x * scale

    def seg(k, scale=None, lo=0, hi=None):
        return seg32(k, scale, lo, hi).astype(BF16)

    def dup(k):
        g0, g1 = seg(k, hi=HEAD_DIM), seg(k, lo=HEAD_DIM)
        return [g0, g0, g1, g1]

    def put(o_ref, pieces):
        c = 0
        for x in pieces:
            o_ref[0, c:c + x.shape[0], :] = x
            c += x.shape[0]
        assert c == o_ref.shape[1]

    cols = w_ref.shape[2]
    put(w1_ref, [seg(_QA, qs), seg(_KA), seg(_VA), seg(_QB, SCALE), seg(_VC)] + dup(_VSB) + dup(_VWB))
    put(w2_ref, [seg(_QB, qs), seg(_QC, qs), seg(_KC)] + dup(_KWB) + dup(_KSB))
    gates = jnp.concatenate([seg32(_FA), seg32(_GB), jnp.zeros((LANES - H_A - 3 * H_B, cols), F32)], axis=0)
    put(w3_ref, [gates.astype(BF16), seg(_KCB), seg(_VCB)])
    put(wz_ref, [seg(_MG), seg(_ZA), seg(_ZB), seg(_ZC)])


def _pack_weights(w_in, tc=256):
    depth, D, N = w_in.shape
    wt = jnp.swapaxes(w_in, 1, 2)
    widths = (W1_COLS, W2_COLS, W3_COLS, WZ_COLS)
    return pl.pallas_call(
        _pack_kernel,
        grid=(depth, D // tc),
        in_specs=[pl.BlockSpec((1, N, tc), lambda l, i: (l, 0, i))],
        out_specs=[pl.BlockSpec((1, n, tc), lambda l, i: (l, 0, i)) for n in widths],
        out_shape=[jax.ShapeDtypeStruct((depth, n, D), BF16) for n in widths],
        compiler_params=_cparams(("parallel", "parallel")),
        name="pack_weights",
    )(wt)


O1_QA, O1_KA, O1_VA, O1_QBU, O1_VC, O1_VS, O1_VW = 0, 4, 8, 12, 16, 20, 22
O2_QBR, O2_QC, O2_KC, O2_KW, O2_KS = 0, 4, 8, 12, 14


def kernel(x, p, positions, norm_g, w_in, b_forget, cmp_pe_k, cmp_w1_k, cmp_b1_k, cmp_w2_k,
           cmp_pe_v, cmp_w1_v, cmp_b1_v, cmp_w2_v, diff_lam, diff_subln_g,
           w_br_a, w_br_b, w_br_c, w_out, w_ple, w_ple_gate, final_g):
    B, S, D = x.shape
    depth = w_in.shape[0]
    T = B * S
    R = S // CMP_STRIDE
    n_sel = S // L_SEL
    assert n_sel <= HEAD_DIM and S % min(TQ, S) == 0

    xf = x.reshape(T, D)
    cos_t, sin_t = _rope_tables(positions.astype(F32).reshape(T))
    cover = _cover_matrix(R, n_sel)
    h = _rmsnorm(xf, norm_g[0], BF16)
    w1, w2, w3, wz = _pack_weights(w_in)

    for i in range(depth):
        o1 = _proj(h, w1, i, BF16, 1024, name="proj_plain")
        o2 = _proj(h, w2, i, BF16, 1024, rope_tabs=(cos_t, sin_t), name="proj_rope")
        fg, kv4 = _proj_f32(h, w3, i)

        b_pad = jnp.pad(b_forget[i], (0, LANES - H_A)).reshape(1, LANES)
        ck = _forget_cumsum(fg, b_pad, B, S).reshape(B, H_A // 2, 2, S)
        att_a = _flash("fox", B, S, H_A // 2, o1, O1_QA, o1, lambda h_: O1_KA + h_,
                       o1, lambda h_: O1_VA + h_, ck=ck)

        lam_init = 0.8 - 0.6 * math.exp(-0.3 * i)
        att_c = _flash("diff", B, S, H_C, o2, O2_QC, o2, lambda h_: O2_KC + h_,
                       o1, lambda h_: O1_VC + h_, lam=diff_lam[i],
                       subln_g=diff_subln_g[i].reshape(1, LANES), lam_init=lam_init)

        tkv = kv4.reshape(2 * G_B, B, R, CMP_STRIDE * HEAD_DIM)
        pe =jnp.stack([cmp_pe_k[i].reshape(1, -1), cmp_pe_v[i].reshape(1, -1)])
        cw1 = jnp.stack([cmp_w1_k[i], cmp_w1_v[i]]).astype(BF16)
        cb1 = jnp.stack([cmp_b1_k[i].reshape(1, -1), cmp_b1_v[i].reshape(1, -1)])
        cw2 = jnp.stack([cmp_w2_k[i], cmp_w2_v[i]])
        cw2 = jnp.concatenate([cw2, cw2], axis=-1).astype(BF16)
        ckv = _compress(tkv, pe, cw1, cb1, cw2)
        o_cmp, mm = _cmp_topk(B, S, o1, O1_QBU, ckv, cover)
        o_sel = _flash("sel", B, S, H_B // 2, o2, O2_QBR, o2, lambda h_: O2_KS + h_ // 2,
                       o1, lambda h_: O1_VS + h_ // 2, mm=mm)
        o_win = _flash("win", B, S, H_B // 2, o2, O2_QBR, o2, lambda h_: O2_KW + h_ // 2,
                       o1, lambda h_: O1_VW + h_ // 2)

        last = i == depth - 1
        g_next = (final_g if last else norm_g[i + 1]).reshape(1, D)
        res = _post(att_a, o_cmp, o_sel, o_win, att_c, fg, h, xf, p.reshape(depth * T, -1), i,
                    wz, w_br_a[i].astype(BF16), w_br_b[i].astype(BF16), w_br_c[i].astype(BF16),
                    w_out[i].astype(BF16), w_ple_gate[i].astype(BF16), w_ple[i].astype(BF16),
                    g_next, last)
        if last:
            return res[0].reshape(B, S, D)
        xf, h = res
```

```python
import functools
import math

import numpy as np
import jax
import jax.numpy as jnp
from jax import lax
from jax.experimental import pallas as pl
from jax.experimental.pallas import tpu as pltpu

F32 = jnp.float32
BF16 = jnp.bfloat16

LANES = 128
HEAD_DIM = 64
NEG = -1e30
BIG = 1e30
LOWEST = -2.0 ** 127
MARKED = -2.0 ** 120
EPS = 1e-6
ROPE_THETA = 10000.0
H_A = 8
H_B = 8
G_B = 2
H_C = 4
L_CMP = 32
CMP_STRIDE = 16
CMP_HIDDEN = 256
L_SEL = 64
TOP_N = 16
WINDOW = 512
N_BRANCH = 3
D_MODEL = 1024
W_A = H_A * HEAD_DIM
W_B = H_B * HEAD_DIM
W_C = H_C * 2 * HEAD_DIM
KV_B = G_B * HEAD_DIM
SPLIT_SIZES = (W_A, W_A, W_A, H_A, W_A,
               W_B, KV_B, KV_B, KV_B, KV_B, KV_B, KV_B, 3 * H_B, W_B,
               2 * H_C * HEAD_DIM, 2 * H_C * HEAD_DIM, W_C, W_C,
               N_BRANCH * D_MODEL)
SCALE = HEAD_DIM ** -0.5
LOG2E = math.log2(math.e)

VMEM_LIMIT = 48 * 1024 * 1024

TQ = 1024
CMP_TQ = 1024
FLASH_ROWS = 1024
PROJ_TM = 1024
POST_TM = 256


def _cparams(sem):
    return pltpu.CompilerParams(dimension_semantics=sem, vmem_limit_bytes=VMEM_LIMIT)


def _dot(a, b):
    return jnp.dot(a, b, preferred_element_type=F32)


def _dot_nt(a, b):
    return lax.dot_general(a, b, (((1,), (1,)), ((), ())), preferred_element_type=F32)


def _split3(x):
    hi = x.astype(BF16)
    r1 = x - hi.astype(F32)
    mid = r1.astype(BF16)
    lo = (r1 - mid.astype(F32)).astype(BF16)
    return hi, mid, lo


def _dot_exact01(x, m01):
    hi, mid, lo = _split3(x)
    return _dot(hi, m01) + _dot(mid, m01) + _dot(lo, m01)


def _rmsnorm_kernel(x_ref, g_ref, o_ref):
    x = x_ref[...]
    y = x * lax.rsqrt(jnp.mean(x * x, axis=-1, keepdims=True) + EPS)
    o_ref[...] = (y * g_ref[...]).astype(o_ref.dtype)


def _rmsnorm(x, g, out_dtype, tm=1024):
    T, D = x.shape
    return pl.pallas_call(
        _rmsnorm_kernel,
        grid=(T // tm,),
        in_specs=[pl.BlockSpec((tm, D), lambda i: (i, 0)),
                  pl.BlockSpec((1, D), lambda i: (0, 0))],
        out_specs=pl.BlockSpec((tm, D), lambda i: (i, 0)),
        out_shape=jax.ShapeDtypeStruct((T, D), out_dtype),
        compiler_params=_cparams(("parallel",)),
        name="rmsnorm",
    )(x, g.reshape(1, D))


def _rope_table_kernel(pos_ref, invf_ref, sign_ref, cos_ref, sin_ref):
    ang = pos_ref[...] * invf_ref[...]
    cos_ref[...] = jnp.cos(ang)
    sin_ref[...] = jnp.sin(ang) * sign_ref[...]


def _rope_tables(pos_f32, tm=512):
    T = pos_f32.shape[0]
    half = HEAD_DIM // 2
    inv_freq = ROPE_THETA ** (-jnp.arange(half, dtype=F32) / half)
    invf = jnp.tile(inv_freq, LANES // half).reshape(1, LANES)
    sign = jnp.tile(jnp.concatenate([-jnp.ones((half,), F32), jnp.ones((half,), F32)]),
                    LANES // HEAD_DIM).reshape(1, LANES)
    return pl.pallas_call(
        _rope_table_kernel,
        grid=(T // tm,),
        in_specs=[pl.BlockSpec((tm, 1), lambda i: (i, 0)),
                  pl.BlockSpec((1, LANES), lambda i: (0, 0)),
                  pl.BlockSpec((1, LANES), lambda i: (0, 0))],
        out_specs=[pl.BlockSpec((tm, LANES), lambda i: (i, 0)),
                   pl.BlockSpec((tm, LANES), lambda i: (i, 0))],
        out_shape=[jax.ShapeDtypeStruct((T, LANES), F32),
                   jax.ShapeDtypeStruct((T, LANES), F32)],
        compiler_params=_cparams(("parallel",)),
        name="rope_table",
    )(pos_f32.reshape(T, 1), invf, sign)


def _proj_kernel(h_ref, w_ref, *rest, rope, n_plain):
    acc = _dot_nt(h_ref[...], w_ref[...])
    if rope:
        cos_ref, sin_ref, o_ref = rest

        @pl.when(pl.program_id(1) < n_plain)
        def _():
            o_ref[...] = acc.astype(o_ref.dtype)

    if rope:
        @pl.when(pl.program_id(1) >= n_plain)
        def _():
            _rope_store(acc, cos_ref, sin_ref, o_ref)
    else:
        (o_ref,) = rest
        o_ref[...] = acc.astype(o_ref.dtype)


def _rope_store(acc, cos_ref, sin_ref, o_ref):
    if True:
        cos = cos_ref[...]
        sin = sin_ref[...]
        lane = lax.broadcasted_iota(jnp.int32, cos.shape, 1)
        first = (lane & (HEAD_DIM - 1)) < (HEAD_DIM // 2)
        for c in range(acc.shape[1] // LANES):
            t = acc[:, c * LANES:(c + 1) * LANES]
            sw = jnp.where(first, pltpu.roll(t, LANES - HEAD_DIM // 2, 1),
                           pltpu.roll(t, HEAD_DIM // 2, 1))
            o_ref[:, c * LANES:(c + 1) * LANES] = (t * cos + sw * sin).astype(o_ref.dtype)


def _proj_f32_kernel(h_ref, w_ref, fg_ref, kv_ref, x_sc):
    acc = _dot_nt(h_ref[...], w_ref[...])
    fg_ref[...] = acc[:, :LANES]
    rows = kv_ref.shape[1]
    for c in range(G_B):
        x_sc[c] = acc[:, LANES * (1 + c):LANES * (2 + c)]
    for c in range(2 * G_B):
        for l in range(CMP_STRIDE):
            x = x_sc[c // 2, pl.ds(l, rows, stride=CMP_STRIDE), :]
            kv_ref[c, :, l * HEAD_DIM:(l + 1) * HEAD_DIM] = x[:, (c % 2) * HEAD_DIM:(c % 2 + 1) * HEAD_DIM]


def _proj_f32(h, w, layer):
    T, D = h.shape
    tm = min(PROJ_TM, T)
    return pl.pallas_call(
        _proj_f32_kernel,
        grid=(T // tm,),
        in_specs=[pl.BlockSpec((tm, D), lambda i: (i, 0)),
                  pl.BlockSpec((None,) + w.shape[1:], lambda i: (layer, 0, 0))],
        out_specs=[pl.BlockSpec((tm, LANES), lambda i: (i, 0)),
                   pl.BlockSpec((2 * G_B, tm // CMP_STRIDE, CMP_STRIDE * HEAD_DIM), lambda i: (0, i, 0))],
        out_shape=[jax.ShapeDtypeStruct((T, LANES), F32),
                   jax.ShapeDtypeStruct((2 * G_B, T // CMP_STRIDE, CMP_STRIDE * HEAD_DIM), F32)],
        scratch_shapes=[pltpu.VMEM((G_B, tm, LANES), F32)],
        compiler_params=_cparams(("parallel",)),
        name="proj_f32",
    )(h, w)


def _proj(h, w, layer, out_dtype, tn, rope_tabs=None, n_plain=0, name="proj"):
    T, D = h.shape
    N = w.shape[1]
    tm = min(PROJ_TM, T)
    assert T % tm == 0 and N % tn == 0
    in_specs = [pl.BlockSpec((tm, D), lambda i, j: (i, 0)),
                pl.BlockSpec((None, tn, D), lambda i, j: (layer, j, 0))]
    args = [h, w]
    if rope_tabs is not None:
        in_specs += [pl.BlockSpec((tm, LANES), lambda i, j: (i, 0)),
                     pl.BlockSpec((tm, LANES), lambda i, j: (i, 0))]
        args += list(rope_tabs)
    return pl.pallas_call(
        functools.partial(_proj_kernel, rope=rope_tabs is not None, n_plain=n_plain),
        grid=(T // tm, N // tn),
        in_specs=in_specs,
        out_specs=pl.BlockSpec((tm, tn), lambda i, j: (i, j)),
        out_shape=jax.ShapeDtypeStruct((T, N), out_dtype),
        compiler_params=_cparams(("parallel", "arbitrary")),
        name=name,
    )(*args)


def _cumsum_kernel(fg_ref, b_ref, c_ref, carry_sc, *, tiles_per_seq):
    i = pl.program_id(0)

    @pl.when(i % tiles_per_seq == 0)
    def _():
        carry_sc[...] = jnp.zeros_like(carry_sc)

    z = fg_ref[...] + b_ref[...]
    logf = jnp.minimum(z, 0.0) - jnp.log1p(jnp.exp(-jnp.abs(z)))
    tm = z.shape[0]
    r = lax.broadcasted_iota(jnp.int32, (tm, tm), 0)
    c = lax.broadcasted_iota(jnp.int32, (tm, tm), 1)
    tri = jnp.where(c <= r, 1.0, 0.0).astype(BF16)
    cs = _dot_exact01_left(tri, logf) + carry_sc[...]
    carry_sc[...] = cs[tm - 1:tm, :]
    c_ref[0] = jnp.transpose(cs * LOG2E)[:H_A, :]


def _dot_exact01_left(m01, x):
    hi, mid, lo = _split3(x)
    return _dot(m01, hi) + _dot(m01, mid) + _dot(m01, lo)


def _forget_cumsum(fg, b_pad, B, S, tm=512):
    tps = S // tm
    return pl.pallas_call(
        functools.partial(_cumsum_kernel, tiles_per_seq=tps),
        grid=(B * tps,),
        in_specs=[pl.BlockSpec((tm, LANES), lambda i: (i, 0)),
                  pl.BlockSpec((1, LANES), lambda i: (0, 0))],
        out_specs=pl.BlockSpec((1, H_A, tm), lambda i: (i // tps, 0, i % tps)),
        out_shape=jax.ShapeDtypeStruct((B, H_A, S), F32),
        scratch_shapes=[pltpu.VMEM((1, LANES), F32)],
        compiler_params=_cparams(("arbitrary",)),
        name="forget_cumsum",
    )(fg, b_pad)


def _flash_kernel(*refs, mode, tq, tw, lam_init):
    it = iter(refs)
    q_ref = next(it)
    mm_ref = next(it) if mode == "sel" else None
    k_ref = next(it)
    v_ref = next(it)
    if mode == "fox":
        ck_ref = next(it)
    if mode == "diff":
        lam_ref = next(it)
        g_ref = next(it)
    o_ref = next(it)
    m_sc = next(it)
    acc_sc = next(it)
    v1_sc = next(it)
    s0_sc = next(it)
    s1_sc = next(it)
    s2_sc = next(it)
    if mode == "sel":
        ke_sc = next(it)

    qi = pl.program_id(2)
    S = k_ref.shape[0]
    lane = lax.broadcasted_iota(jnp.int32, (tq, LANES), 1)
    half0 = lane < HEAD_DIM

    @pl.when(qi == 0)
    def _():
        v1_sc[:, :LANES] = v_ref[...]
        v1_sc[:, LANES:] = jnp.ones((S, LANES), BF16)
        if mode == "sel":
            row = lax.broadcasted_iota(jnp.int32, (S, LANES), 0)
            ln = lax.broadcasted_iota(jnp.int32, (S, LANES), 1)
            blk = lax.shift_right_logical(row, int(math.log2(L_SEL)))
            ke_sc[:, :LANES] = k_ref[...]
            ke_sc[:, LANES:] = jnp.where(ln == blk, BIG, 0.0).astype(BF16)

    kk = ke_sc if mode == "sel" else k_ref

    def split_halves(qr, mr):
        q2 = qr[...].astype(F32)
        zero = jnp.zeros_like(q2)
        qa = jnp.where(half0, q2, zero).astype(BF16)
        qb = jnp.where(half0, zero, q2).astype(BF16)
        if mode == "sel":
            mmh = jnp.where(half0, mr[...].astype(F32), zero).astype(BF16)
            qa = jnp.concatenate([qa, mmh], axis=1)
            qb = jnp.concatenate([qb, mmh], axis=1)
        return qa, qb

    qm = split_halves(q_ref, mm_ref)

    m_sc[...] = jnp.full_like(m_sc, NEG)
    acc_sc[...] = jnp.zeros_like(acc_sc)
    q0 = qi * tq
    rb = LANES
    nrb = tq // rb
    nch = tw // LANES
    hrb = nrb // 2
    dsq = (lax.broadcasted_iota(jnp.int32, (rb, LANES), 0)
           - lax.broadcasted_iota(jnp.int32, (rb, LANES), 1))

    def qk(start, s_sc, r_lo=0, r_hi=tq):
        koff = pl.multiple_of(start, tw)
        kt = kk[pl.ds(koff, tw), :]
        for a in range(2):
            s = _dot_nt(qm[a][r_lo:r_hi], kt)
            if mode == "fox":
                s = s - ck_ref[0, 0, a:a + 1, pl.ds(koff, tw)]
            s_sc[a, r_lo:r_hi] = s

    FULL = [("full",)] * nrb
    DIAG_A = [("diag", r) for r in range(hrb)] + [("full",)] * hrb
    DIAG_B = [None] * hrb + [("diag", r) for r in range(hrb)]
    PREV_P = [("prev", r) for r in range(hrb)] + [None] * hrb
    WIN_A = [("diag", r) for r in range(hrb)] + [("prev", r) for r in range(hrb)]

    def soft_pv(s_sc, start, kinds):
        koff = pl.multiple_of(start, tw)
        vt = v1_sc[pl.ds(koff, tw), :]
        chains = []
        i = 0
        while i < nrb:
            j = i + 1
            if kinds[i] == ("full",):
                while j < nrb and kinds[j] == ("full",) and (j - i) * rb < FLASH_ROWS:
                    j += 1
            if kinds[i] is not None:
                chains.append((slice(i * rb, j * rb), kinds[i]))
            i = j
        for a in range(2):
            for rows, kind in chains:
                if kind[0] == "full":
                    c_lo, c_hi, c_edge = 0, nch, None
                elif kind[0] == "diag":
                    c_lo, c_hi, c_edge = 0, kind[1] + 1, kind[1]
                else:
                    c_lo, c_hi, c_edge = kind[1], nch, kind[1]
                ch = [s_sc[a, rows, c * LANES:(c + 1) * LANES] for c in range(c_lo, c_hi)]
                if c_edge is not None:
                    ok = dsq >= 0 if kind[0] == "diag" else dsq < 0
                    ch[c_edge - c_lo] = jnp.where(ok, ch[c_edge - c_lo], NEG)
                mx = functools.reduce(jnp.maximum, ch)
                m_old = m_sc[a, rows]
                m_new = jnp.maximum(m_old, jnp.max(mx, axis=1, keepdims=True))
                alpha = jnp.exp2(m_old - m_new)
                p = jnp.concatenate([jnp.exp2(c_ - m_new).astype(BF16) for c_ in ch], axis=1)
                m_sc[a, rows] = m_new
                al = jnp.concatenate([alpha, alpha], axis=1)
                acc_sc[a, rows] = (al * acc_sc[a, rows]
                                   + _dot(p, vt[c_lo * LANES:c_hi * LANES]))

    if mode == "win":
        @pl.when(qi > 0)
        def _():
            qk(q0 - tw, s0_sc, 0, tq // 2)
            qk(q0, s1_sc)
            qk(q0 + tw, s2_sc, tq // 2, tq)
            soft_pv(s0_sc, q0 - tw, PREV_P)
            soft_pv(s1_sc, q0, WIN_A)
            soft_pv(s2_sc, q0 + tw, DIAG_B)

        @pl.when(qi == 0)
        def _():
            qk(q0, s1_sc)
            qk(q0 + tw, s2_sc, tq // 2, tq)
            soft_pv(s1_sc, q0, WIN_A)
            soft_pv(s2_sc, q0 + tw, DIAG_B)
    else:
        qk(0, s0_sc)

        def pair_body(i, carry):
            t0 = 2 * i * tw
            qk(t0 + tw, s1_sc)
            soft_pv(s0_sc, t0, FULL)
            qk(t0 + 2 * tw, s0_sc)
            soft_pv(s1_sc, t0 + tw, FULL)
            return carry

        lax.fori_loop(0, qi, pair_body, 0)
        qk(q0 + tw, s1_sc, tq // 2, tq)
        soft_pv(s0_sc, q0, DIAG_A)
        soft_pv(s1_sc, q0 + tw, DIAG_B)

    o0 = acc_sc[0, :, :LANES] / acc_sc[0, :, LANES:]
    o1 = acc_sc[1, :, :LANES] / acc_sc[1, :, LANES:]
    if mode == "diff":
        lq = lam_ref[...]
        lam = (jnp.exp(jnp.sum(lq[0:1] * lq[1:2], keepdims=True))
               - jnp.exp(jnp.sum(lq[2:3] * lq[3:4], keepdims=True)) + lam_init)
        d = o0 - lam * o1
        y = d * lax.rsqrt(jnp.mean(d * d, axis=-1, keepdims=True) + EPS)
        o_ref[...] = (y * g_ref[...]) * (1.0 - lam_init)
    else:
        o_ref[...] = jnp.where(half0, o0, o1)


def _flash(mode, B, S, n_blk, q, q_blk0, k, k_blk, v, v_blk, *,
           mm=None, ck=None, lam=None, subln_g=None, lam_init=0.0):
    tq = min(TQ, S)
    tw = WINDOW
    nq = S // tq
    T = B * S
    assert S % tq == 0 and tq == 2 * tw and tw % LANES == 0
    in_specs = [pl.BlockSpec((tq, LANES), lambda b, h, i: (b * nq + i, q_blk0 + h))]
    args = [q]
    if mode == "sel":
        in_specs.append(pl.BlockSpec((tq, LANES), lambda b, h, i: (b * nq + i, h // 2)))
        args.append(mm)
    in_specs.append(pl.BlockSpec((S, LANES), lambda b, h, i: (b, k_blk(h))))
    args.append(k)
    in_specs.append(pl.BlockSpec((S, LANES), lambda b, h, i: (b, v_blk(h))))
    args.append(v)
    if mode == "fox":
        in_specs.append(pl.BlockSpec((1, 1, 2, S), lambda b, h, i: (b, h, 0, 0)))
        args += [ck]
    if mode == "diff":
        in_specs.append(pl.BlockSpec(lam.shape, lambda b, h, i: (0, 0)))
        in_specs.append(pl.BlockSpec((1, LANES), lambda b, h, i: (0, 0)))
        args += [lam, subln_g]
    scratch = [pltpu.VMEM((2, tq, LANES), F32), pltpu.VMEM((2, tq, 2 * LANES), F32),
               pltpu.VMEM((S, 2 * LANES), BF16)]
    scratch += [pltpu.VMEM((2, tq, tw), F32)] * 3
    if mode == "sel":
        scratch += [pltpu.VMEM((S, 2 * LANES), BF16)]
    return pl.pallas_call(
        functools.partial(_flash_kernel, mode=mode, tq=tq, tw=tw, lam_init=lam_init),
        grid=(B, n_blk, nq),
        in_specs=in_specs,
        out_specs=pl.BlockSpec((tq, LANES), lambda b, h, i: (b * nq + i, h)),
        out_shape=jax.ShapeDtypeStruct((T, n_blk * LANES), F32),
        scratch_shapes=scratch,
        compiler_params=_cparams(("parallel", "parallel", "arbitrary")),
        name="flash_" + mode,
    )(*args)


def _compress_kernel(t_ref, pe_ref, w1_ref, b1_ref, w2_ref, o_ref):
    t = t_ref[0, 0]
    R, half = t.shape
    pe = pe_ref[0]
    xa = (t + pe[:, :half]).astype(BF16)
    xb = (t + pe[:, half:]).astype(BF16)
    a = _dot(xa, w1_ref[0, :half, :])
    b = _dot(xb, w1_ref[0, half:, :])
    hp = a + pltpu.roll(b, R - 1, 0) + b1_ref[0]
    hid = hp * jax.nn.sigmoid(hp)
    o_ref[0, 0] = _dot(hid.astype(BF16), w2_ref[0]).astype(o_ref.dtype)


def _compress(tkv, pe, w1, b1, w2dup):
    _, B, R, W = tkv.shape
    G = G_B
    return pl.pallas_call(
        _compress_kernel,
        grid=(2, B, G),
        in_specs=[pl.BlockSpec((1, 1, R, W), lambda s, b, g: (s * G_B + g, b, 0, 0)),
                  pl.BlockSpec((1, 1, 2 * W), lambda s, b, g: (s, 0, 0)),
                  pl.BlockSpec((1, 2 * W, CMP_HIDDEN), lambda s, b, g: (s, 0, 0)),
                  pl.BlockSpec((1, 1, CMP_HIDDEN), lambda s, b, g: (s, 0, 0)),
                  pl.BlockSpec((1, CMP_HIDDEN, LANES), lambda s, b, g: (s, 0, 0))],
        out_specs=pl.BlockSpec((1, 1, R, LANES), lambda s, b, g: (s, b, 0, g)),
        out_shape=jax.ShapeDtypeStruct((2, B, R, G * LANES), BF16),
        compiler_params=_cparams(("parallel", "parallel", "parallel")),
        name="nsa_compress",
    )(tkv, pe, w1, b1, w2dup)


def _cmp_kernel(q_ref, kc_ref, vc_ref, cov_ref, o_ref, mm_ref, *, tq, n_sel, top_n):
    qi = pl.program_id(2)
    kc = kc_ref[0, 0]
    vc = vc_ref[0, 0]
    R = kc.shape[0]
    lane = lax.broadcasted_iota(jnp.int32, (tq, LANES), 1)
    half0 = lane < HEAD_DIM
    t_r = lax.broadcasted_iota(jnp.int32, (tq, R), 0) + qi * tq
    c_end = lax.broadcasted_iota(jnp.int32, (tq, R), 1) * CMP_STRIDE + (L_CMP - 1)
    ok = c_end <= t_r
    psum = jnp.zeros((tq, R), F32)
    outs = []
    for pr in range(2):
        q2 = q_ref[:, pr * LANES:(pr + 1) * LANES].astype(F32)
        zero = jnp.zeros_like(q2)
        pair = []
        for a in range(2):
            qm = (jnp.where(half0, q2, zero) if a == 0 else jnp.where(half0, zero, q2)).astype(BF16)
            s = jnp.where(ok, _dot_nt(qm, kc), NEG)
            m = jnp.max(s, axis=1, keepdims=True)
            e = jnp.where(ok, jnp.exp(s - m), 0.0)
            l = jnp.sum(e, axis=1, keepdims=True)
            p = e / jnp.where(l > 0.0, l, 1.0)
            psum = psum + p
            pair.append(_dot(p.astype(BF16), vc))
        outs.append(jnp.where(half0, pair[0], pair[1]))
    o_ref[...] = jnp.concatenate(outs, axis=1)

    imp = _dot_exact01(psum, cov_ref[...])
    jl = lane & (HEAD_DIM - 1)
    t_q = lax.broadcasted_iota(jnp.int32, (tq, LANES), 0) + qi * tq
    tb = lax.shift_right_logical(t_q, int(math.log2(L_SEL)))
    valid = jl <= tb
    forced = (jl == 0) | (jl == tb) | (jl == tb - 1)
    in_range = jl < n_sel
    score = jnp.where(forced, LOWEST, jnp.where(valid, imp, -BIG))
    score = jnp.where(in_range, score, LOWEST)
    for _ in range(top_n - 3):
        idx = jnp.argmax(score, axis=1, keepdims=True).astype(jnp.int32) & (HEAD_DIM - 1)
        score = jnp.where(jl == idx, LOWEST, score)
    mm_ref[...] = jnp.where((score < MARKED) & in_range, 0.0, -1.0).astype(mm_ref.dtype)


def _cmp_topk(B, S, q, q_blk0, ckv, cover_dup):
    tq = min(CMP_TQ, S)
    nq = S // tq
    T = B * S
    R = ckv.shape[2]
    n_sel = S // L_SEL
    gw = 2 * LANES
    return pl.pallas_call(
        functools.partial(_cmp_kernel, tq=tq, n_sel=n_sel, top_n=min(TOP_N, n_sel)),
        grid=(B, G_B, nq),
        in_specs=[pl.BlockSpec((tq, gw), lambda b, g, i: (b * nq + i, q_blk0 // 2 + g)),
                  pl.BlockSpec((1, 1, R, LANES), lambda b, g, i: (0, b, 0, g)),
                  pl.BlockSpec((1, 1, R, LANES), lambda b, g, i: (1, b, 0, g)),
                  pl.BlockSpec((R, LANES), lambda b, g, i: (0, 0))],
        out_specs=[pl.BlockSpec((tq, gw), lambda b, g, i: (b * nq + i, g)),
                   pl.BlockSpec((tq, LANES), lambda b, g, i: (b * nq + i, g))],
        out_shape=[jax.ShapeDtypeStruct((T, G_B * gw), F32),
                   jax.ShapeDtypeStruct((T, G_B * LANES), BF16)],
        compiler_params=_cparams(("parallel", "parallel", "parallel")),
        name="nsa_cmp_topk",
    )(q, ckv, ckv, cover_dup)


def _post_kernel(aa_ref, oc_ref, os_ref, ow_ref, ac_ref, fg_ref, h_ref, x_ref, p_ref,
                 wz_ref, wa_ref, wb_ref, wc_ref, wo_ref, wg_ref, wp_ref, gn_ref,
                 *outs, last):
    def silu(z):
        return z * jax.nn.sigmoid(z)

    W = W_B
    D = x_ref.shape[1]
    mz = _dot_nt(h_ref[...], wz_ref[...])
    za = mz[:, 3 * D:3 * D + W]
    zb = mz[:, 3 * D + W:3 * D + 2 * W]
    zc = mz[:, 3 * D + 2 * W:]
    sg = jax.nn.sigmoid(fg_ref[...])
    head = lax.shift_right_logical(lax.broadcasted_iota(jnp.int32, sg.shape, 1), int(math.log2(HEAD_DIM)))
    mix = []
    for j in range(W // LANES):
        cols = slice(j * LANES, (j + 1) * LANES)
        src = H_A + N_BRANCH * (2 * j + head)
        g = [jnp.take_along_axis(sg, src + r, axis=1) for r in range(N_BRANCH)]
        mix.append(g[0] * oc_ref[:, cols] + g[1] * os_ref[:, cols] + g[2] * ow_ref[:, cols])
    ya = aa_ref[...] * silu(za)
    yb = jnp.concatenate(mix, axis=1) * silu(zb)
    yc = ac_ref[...] * silu(zc)
    pa = _dot(ya.astype(BF16), wa_ref[...])
    pb = _dot(yb.astype(BF16), wb_ref[...])
    pc = _dot(yc.astype(BF16), wc_ref[...])
    merged = (jax.nn.sigmoid(mz[:, :D]) * pa + jax.nn.sigmoid(mz[:, D:2 * D]) * pb
              + jax.nn.sigmoid(mz[:, 2 * D:3 * D]) * pc)
    x1 = x_ref[...] + _dot(merged.astype(BF16), wo_ref[...])
    gate = jax.nn.sigmoid(_dot(x1.astype(BF16), wg_ref[...]))
    x2 = x1 + gate * _dot(p_ref[...].astype(BF16), wp_ref[...])
    y = x2 * lax.rsqrt(jnp.mean(x2 * x2, axis=-1, keepdims=True) + EPS) * gn_ref[...]
    if last:
        outs[0][...] = y
    else:
        outs[0][...] = x2
        outs[1][...] = y.astype(BF16)


def _post(aa, oc, osel, ow, ac, fg, h, x, p, layer, wz, wa, wb, wc, wo, wg, wp, g_next, last):
    T, D = x.shape
    tm = min(POST_TM, T)
    W = W_A
    row = lambda i: (i, 0)
    const = lambda i: (0, 0)

    def resident(arr):
        return pl.BlockSpec(arr.shape, const, pipeline_mode=pl.Buffered(1))

    in_specs = [pl.BlockSpec((tm, W), row)] * 5 + [
        pl.BlockSpec((tm, LANES), row),
        pl.BlockSpec((tm, D), row),
        pl.BlockSpec((tm, D), row),
        pl.BlockSpec((tm, p.shape[1]), lambda i: (i + layer * (T // tm), 0)),
        pl.BlockSpec((None,) + wz.shape[1:], lambda i: (layer, 0, 0), pipeline_mode=pl.Buffered(1)),
        resident(wa), resident(wb), resident(wc), resident(wo), resident(wg),
        resident(wp), pl.BlockSpec((1, D), const)]
    if last:
        out_specs = [pl.BlockSpec((tm, D), row)]
        out_shape = [jax.ShapeDtypeStruct((T, D), F32)]
    else:
        out_specs = [pl.BlockSpec((tm, D), row), pl.BlockSpec((tm, D), row)]
        out_shape = [jax.ShapeDtypeStruct((T, D), F32), jax.ShapeDtypeStruct((T, D), BF16)]
    return pl.pallas_call(
        functools.partial(_post_kernel, last=last),
        grid=(T // tm,),
        in_specs=in_specs,
        out_specs=out_specs,
        out_shape=out_shape,
        compiler_params=_cparams(("parallel",)),
        name="post",
    )(aa, oc, osel, ow, ac, fg, h, x, p, wz, wa, wb, wc, wo, wg, wp, g_next)


def _cover_matrix(R, n_sel):
    c_start = np.arange(R)[:, None] * CMP_STRIDE
    j_start = np.arange(HEAD_DIM)[None, :] * L_SEL
    cov = ((c_start < j_start + L_SEL) & (c_start + L_CMP > j_start)
           & (np.arange(HEAD_DIM)[None, :] < n_sel)).astype(np.float32)
    return jnp.asarray(np.concatenate([cov, cov], axis=1), BF16)


_SEG_OFFS = np.cumsum((0,) + SPLIT_SIZES)
(_QA, _KA, _VA, _FA, _ZA, _QB, _KCB, _VCB, _KSB, _VSB, _KWB, _VWB, _GB, _ZB, _QC, _KC, _VC, _ZC,
 _MG) = range(len(SPLIT_SIZES))
W1_COLS = 5 * W_A + 2 * 2 * KV_B
W2_COLS = 3 * W_A + 2 * 2 * KV_B
W3_COLS = LANES + 2 * KV_B
WZ_COLS = N_BRANCH * D_MODEL + 3 * W_A


def _pack_kernel(w_ref, w1_ref, w3_ref, wz_ref):
    qs = SCALE * LOG2E

    def seg32(k, scale=None, lo=0, hi=None):
        a = int(_SEG_OFFS[k]) + lo
        b = int(_SEG_OFFS[k + 1]) if hi is None else int(_SEG_OFFS[k]) + hi
        x = w_ref[0, a:b, :]
        return x if scale is None else x * scale

    def seg(k, scale=None, lo=0, hi=None):
        return seg32(k, scale, lo, hi).astype(BF16)

    def dup(k):
        g0, g1 = seg(k, hi=HEAD_DIM), seg(k, lo=HEAD_DIM)
        return [g0, g0, g1, g1]

    def put(o_ref, pieces):
        c = 0
        for x in pieces:
            o_ref[0, c:c + x.shape[0], :] = x
            c += x.shape[0]
        assert c == o_ref.shape[1]

    cols = w_ref.shape[2]
    put(w1_ref, [seg(_QA, qs), seg(_KA), seg(_VA), seg(_QB, SCALE), seg(_VC)] + dup(_VSB) + dup(_VWB)
        + [seg(_QB, qs), seg(_QC, qs), seg(_KC)] + dup(_KWB) + dup(_KSB))
    gates = jnp.concatenate([seg32(_FA), seg32(_GB), jnp.zeros((LANES - H_A - 3 * H_B, cols), F32)], axis=0)
    put(w3_ref, [gates.astype(BF16), seg(_KCB), seg(_VCB)])
    put(wz_ref, [seg(_MG), seg(_ZA), seg(_ZB), seg(_ZC)])


def _pack_weights(w_in, tc=256):
    depth, D, N = w_in.shape
    wt = jnp.swapaxes(w_in, 1, 2)
    widths = (W1_COLS + W2_COLS, W3_COLS, WZ_COLS)
    return pl.pallas_call(
        _pack_kernel,
        grid=(depth, D // tc),
        in_specs=[pl.BlockSpec((1, N, tc), lambda l, i: (l, 0, i))],
        out_specs=[pl.BlockSpec((1, n, tc), lambda l, i: (l, 0, i)) for n in widths],
        out_shape=[jax.ShapeDtypeStruct((depth, n, D), BF16) for n in widths],
        compiler_params=_cparams(("parallel", "parallel")),
        name="pack_weights",
    )(wt)


O1_QA, O1_KA, O1_VA, O1_QBU, O1_VC, O1_VS, O1_VW = 0, 4, 8, 12, 16, 20, 22
O2_QBR, O2_QC, O2_KC, O2_KW, O2_KS = 24, 28, 32, 36, 38


def kernel(x, p, positions, norm_g, w_in, b_forget, cmp_pe_k, cmp_w1_k, cmp_b1_k, cmp_w2_k,
           cmp_pe_v, cmp_w1_v, cmp_b1_v, cmp_w2_v, diff_lam, diff_subln_g,
           w_br_a, w_br_b, w_br_c, w_out, w_ple, w_ple_gate, final_g):
    B, S, D = x.shape
    depth = w_in.shape[0]
    T = B * S
    R = S // CMP_STRIDE
    n_sel = S // L_SEL
    assert n_sel <= HEAD_DIM and S % min(TQ, S) == 0

    xf = x.reshape(T, D)
    cos_t, sin_t = _rope_tables(positions.astype(F32).reshape(T))
    cover = _cover_matrix(R, n_sel)
    h = _rmsnorm(xf, norm_g[0], BF16)
    w1, w3, wz = _pack_weights(w_in)

    for i in range(depth):
        o1 = _proj(h, w1, i, BF16, 1024, rope_tabs=(cos_t, sin_t), n_plain=W1_COLS // 1024, name="proj_qkv")
        o2 = o1
        fg, kv4 = _proj_f32(h, w3, i)

        b_pad = jnp.pad(b_forget[i], (0, LANES - H_A)).reshape(1, LANES)
        ck = _forget_cumsum(fg, b_pad, B, S).reshape(B, H_A // 2, 2, S)
        att_a = _flash("fox", B, S, H_A // 2, o1, O1_QA, o1, lambda h_: O1_KA + h_,
                       o1, lambda h_: O1_VA + h_, ck=ck)

        lam_init = 0.8 - 0.6 * math.exp(-0.3 * i)
        att_c = _flash("diff", B, S, H_C, o2, O2_QC, o2, lambda h_: O2_KC + h_,
                       o1, lambda h_: O1_VC + h_, lam=diff_lam[i],
                       subln_g=diff_subln_g[i].reshape(1, LANES), lam_init=lam_init)

        tkv = kv4.reshape(2 * G_B, B, R, CMP_STRIDE * HEAD_DIM)
        pe =jnp.stack([cmp_pe_k[i].reshape(1, -1), cmp_pe_v[i].reshape(1, -1)])
        cw1 = jnp.stack([cmp_w1_k[i], cmp_w1_v[i]]).astype(BF16)
        cb1 = jnp.stack([cmp_b1_k[i].reshape(1, -1), cmp_b1_v[i].reshape(1, -1)])
        cw2 = jnp.stack([cmp_w2_k[i], cmp_w2_v[i]])
        cw2 = jnp.concatenate([cw2, cw2], axis=-1).astype(BF16)
        ckv = _compress(tkv, pe, cw1, cb1, cw2)
        o_cmp, mm = _cmp_topk(B, S, o1, O1_QBU, ckv, cover)
        o_sel = _flash("sel", B, S, H_B // 2, o2, O2_QBR, o2, lambda h_: O2_KS + h_ // 2,
                       o1, lambda h_: O1_VS + h_ // 2, mm=mm)
        o_win = _flash("win", B, S, H_B // 2, o2, O2_QBR, o2, lambda h_: O2_KW + h_ // 2,
                       o1, lambda h_: O1_VW + h_ // 2)

        last = i == depth - 1
        g_next = (final_g if last else norm_g[i + 1]).reshape(1, D)
        res = _post(att_a, o_cmp, o_sel, o_win, att_c, fg, h, xf, p.reshape(depth * T, -1), i,
                    wz, w_br_a[i].astype(BF16), w_br_b[i].astype(BF16), w_br_c[i].astype(BF16),
                    w_out[i].astype(BF16), w_ple_gate[i].astype(BF16), w_ple[i].astype(BF16),
                    g_next, last)
        if last:
            return res[0].reshape(B, S, D)
        xf, h = res
```
